```python
import math
import jax
import jax.numpy as jnp
from jax import lax
import numpy as np

D_MODEL = 1024
BATCH = 4
SEQ = 8192
DEPTH = 2
DEC_BATCH = 32
DEC_SEQ = 1
PAST_LEN = 16384
PAGE_SIZE = 128

WINDOWS = (128, 512, 2048)
DILATIONS = (1, 4, 16)
N_GROUPS = 3
H_G = 8
DH_A = 64
A_GROUP = H_G * DH_A
A_QKV = N_GROUPS * A_GROUP
Q_BLOCK = 128
NH_B = 4
DK_B = 256
B_WIDTH = NH_B * DK_B
CONV_B = 4
MLSTM_CHUNK = 64
D_FF = 2816
CONV_F = 3
NUM_BUCKETS = 32
MAX_DISTANCE = 2048
RMS_EPS = 1e-6
IN_SIZES = (A_QKV, A_QKV, A_QKV, B_WIDTH, B_WIDTH, B_WIDTH, B_WIDTH, NH_B, NH_B, D_MODEL, D_MODEL)
IN_COLS = 3 * A_QKV + 4 * B_WIDTH + 2 * NH_B + 2 * D_MODEL

kernel_name = 'dilated_attn_mlstm_gated_hybrid_step'


def _rms_norm(x, g):
    xf = x.astype(jnp.float32)
    y = xf * lax.rsqrt(jnp.mean(xf * xf, axis=-1, keepdims=True) + RMS_EPS)
    return (y * g.astype(jnp.float32)).astype(x.dtype)


def _t5_bucket(dist):
    max_exact = NUM_BUCKETS // 2
    df = jnp.maximum(dist, 1).astype(jnp.float32)
    large = max_exact + (jnp.log(df / max_exact) / math.log(MAX_DISTANCE / max_exact)
                         * (NUM_BUCKETS - max_exact)).astype(jnp.int32)
    large = jnp.minimum(large, NUM_BUCKETS - 1)
    return jnp.where(dist < max_exact, dist, large)


def _causal_dwconv(u, buf, w, b):
    K = w.shape[0]
    T = u.shape[1]
    full = jnp.concatenate([buf.astype(u.dtype), u], axis=1)
    y = b + sum(full[:, i:i + T] * w[i] for i in range(K))
    return y, full[:, T:]


def _dilated_attn_prompt(q, k, v, bias_g, window, dil):
    B, S, H, Dh = q.shape
    span = window // dil
    U = S // dil
    nb = -(-U // Q_BLOCK)
    Up = nb * Q_BLOCK

    def to_blocks(t):
        t = t.reshape(B, U, dil, H, Dh).transpose(0, 2, 1, 3, 4)
        t = jnp.pad(t, ((0, 0), (0, 0), (0, Up - U), (0, 0), (0, 0)))
        return t.reshape(B, dil, nb, Q_BLOCK, H, Dh)

    def with_prev(t):
        prev = jnp.pad(t, ((0, 0), (0, 0), (1, 0), (0, 0), (0, 0), (0, 0)))[:, :, :-1]
        return jnp.concatenate([prev, t], axis=3)

    qb = to_blocks(q)
    kb = with_prev(to_blocks(k))
    vb = with_prev(to_blocks(v))
    qi = jnp.arange(Q_BLOCK)[:, None]
    ki = jnp.arange(2 * Q_BLOCK)[None, :]
    rel = qi + Q_BLOCK - ki
    band = (rel >= 0) & (rel <= span)
    has_prev = (jnp.arange(nb) > 0)[:, None, None]
    mask = band[None] & (has_prev | (ki >= Q_BLOCK)[None])
    bias = bias_g[_t5_bucket(jnp.maximum(rel, 0) * dil)].transpose(2, 0, 1)
    s = jnp.einsum('brnqhc,brnkhc->brnhqk', qb, kb).astype(jnp.float32) * (Dh ** -0.5) + bias
    s = jnp.where(mask[None, None, :, None], s, -jnp.inf)
    lse = jax.nn.logsumexp(s, axis=-1)
    p = jnp.exp(s - lse[..., None]).astype(v.dtype)
    o = jnp.einsum('brnhqk,brnkhc->brnqhc', p, vb)
    o = o.reshape(B, dil, Up, H, Dh)[:, :, :U].transpose(0, 2, 1, 3, 4).reshape(B, S, H, Dh)
    lse = lse.transpose(0, 1, 2, 4, 3).reshape(B, dil, Up, H)[:, :, :U]
    lse = lse.transpose(0, 2, 1, 3).reshape(B, S, H)
    return o, lse


def _dilated_attn_sample(q, k_all, v_all, bias_g, window, dil, n_buf):
    T = q.shape[1]
    Dh = q.shape[-1]
    span = window // dil
    j = jnp.arange(span + 1)
    idx = (n_buf + jnp.arange(T))[:, None] - j[None, :] * dil
    valid = idx >= 0
    idxc = jnp.maximum(idx, 0)
    kg = k_all[:, idxc]
    vg = v_all[:, idxc]
    bias = bias_g[_t5_bucket(j * dil)].T
    s = jnp.einsum('bthc,btjhc->bthj', q, kg).astype(jnp.float32) * (Dh ** -0.5) + bias
    s = jnp.where(valid[None, :, None, :], s, -jnp.inf)
    lse = jax.nn.logsumexp(s, axis=-1)
    p = jnp.exp(s - lse[..., None]).astype(vg.dtype)
    o = jnp.einsum('bthj,btjhc->bthc', p, vg)
    return o, lse


def _combine_groups(outs, lses):
    alpha = jax.nn.softmax(jnp.stack(lses, 0), axis=0)
    o = jnp.stack(outs, 0).astype(jnp.float32)
    return jnp.sum(alpha[..., None] * o, axis=0)


def _mlstm_chunk(carry, inp):
    C, n, m = carry
    q, k, v, ig, lf = inp
    L = q.shape[2]
    b = jnp.cumsum(lf, axis=-1)
    causal = jnp.tril(jnp.ones((L, L), dtype=bool))
    logd = jnp.where(causal, b[..., :, None] - b[..., None, :] + ig[..., None, :], -jnp.inf)
    inter = b + m[..., None]
    m_t = jnp.maximum(inter, jnp.max(logd, axis=-1))
    dw = jnp.exp(logd - m_t[..., None])
    iw = jnp.exp(inter - m_t)
    qk = jnp.einsum('bhtc,bhsc->bhts', q, k) * dw
    num = iw[..., None] * jnp.einsum('bhtc,bhcv->bhtv', q, C) + jnp.einsum('bhts,bhsv->bhtv', qk, v)
    den = iw * jnp.einsum('bhtc,bhc->bht', q, n) + jnp.sum(qk, axis=-1)
    h = num / jnp.maximum(jnp.abs(den), jnp.exp(-m_t))[..., None]
    m_new = m_t[..., -1]
    wc = jnp.exp(b[..., -1:] - b + ig - m_new[..., None])
    decay = jnp.exp(b[..., -1] + m - m_new)
    C_new = decay[..., None, None] * C + jnp.einsum('bhs,bhsc,bhsv->bhcv', wc, k, v)
    n_new = decay[..., None] * n + jnp.einsum('bhs,bhsc->bhc', wc, k)
    return (C_new, n_new, m_new), h


def _mlstm(q, k, v, ig, lf, C, n, m, chunk):
    B, T = q.shape[:2]
    nc = T // chunk

    def to_chunks(t):
        t = t.reshape((B, nc, chunk) + t.shape[2:])
        return jnp.moveaxis(jnp.moveaxis(t, 3, 2), 1, 0)

    xs = (to_chunks(q), to_chunks(k), to_chunks(v), to_chunks(ig), to_chunks(lf))
    (C, n, m), h = lax.scan(_mlstm_chunk, (C, n, m), xs)
    h = jnp.moveaxis(jnp.moveaxis(h, 0, 1), 2, 3).reshape(B, T, NH_B, DK_B)
    return h, C, n, m


def _layer(x, prm, rel_bias, past):
    B, T, _ = x.shape
    f32 = jnp.float32
    h = _rms_norm(x, prm['norm1_g'])
    proj = h @ prm['w_in']
    pts = np.cumsum(IN_SIZES)[:-1].tolist()
    q_a, k_a, v_a, q_b, k_b, v_b, o_b, i_b, f_b, g_a, g_b = jnp.split(proj, pts, axis=-1)

    qa = q_a.reshape(B, T, N_GROUPS, H_G, DH_A)
    ka = k_a.reshape(B, T, N_GROUPS, H_G, DH_A)
    va = v_a.reshape(B, T, N_GROUPS, H_G, DH_A)
    outs, lses, new_kv = [], [], []
    for g in range(N_GROUPS):
        window, dil = WINDOWS[g], DILATIONS[g]
        bias_g = rel_bias[:, g * H_G:(g + 1) * H_G]
        qg, kg, vg = qa[:, :, g], ka[:, :, g], va[:, :, g]
        if past is None:
            o, lse = _dilated_attn_prompt(qg, kg, vg, bias_g, window, dil)
            keep = min(window, T)
            new_kv.append(jnp.stack([kg[:, T - keep:], vg[:, T - keep:]], axis=2))
        else:
            cache = past['kv'][g]
            k_all = jnp.concatenate([cache[:, :, 0].astype(kg.dtype), kg], axis=1)
            v_all = jnp.concatenate([cache[:, :, 1].astype(vg.dtype), vg], axis=1)
            o, lse = _dilated_attn_sample(qg, k_all, v_all, bias_g, window, dil, cache.shape[1])
            new_kv.append(jnp.stack([kg, vg], axis=2))
        outs.append(o)
        lses.append(lse)
    a_out = _combine_groups(outs, lses).reshape(B, T, A_GROUP).astype(x.dtype)

    qk_pre = jnp.concatenate([q_b, k_b], axis=-1)
    if past is None:
        mbuf = jnp.zeros((B, CONV_B - 1, 2 * B_WIDTH), x.dtype)
    else:
        mbuf = past['mconv']
    qk_c, new_mconv = _causal_dwconv(qk_pre, mbuf, prm['mconv_w'], prm['mconv_b'])
    qk_c = jax.nn.silu(qk_c)
    qm = qk_c[..., :B_WIDTH].reshape(B, T, NH_B, DK_B).astype(f32)
    km = qk_c[..., B_WIDTH:].reshape(B, T, NH_B, DK_B).astype(f32) * (DK_B ** -0.5)
    vm = v_b.reshape(B, T, NH_B, DK_B).astype(f32)
    ig = i_b.astype(f32) + prm['mgate_b'][0]
    lf = jax.nn.log_sigmoid(f_b.astype(f32) + prm['mgate_b'][1])
    if past is None:
        C0 = jnp.zeros((B, NH_B, DK_B, DK_B), f32)
        n0 = jnp.zeros((B, NH_B, DK_B), f32)
        m0 = jnp.zeros((B, NH_B), f32)
        chunk = MLSTM_CHUNK if T % MLSTM_CHUNK == 0 else T
    else:
        C0 = past['C'].astype(f32)
        n0 = past['n'].astype(f32)
        m0 = past['m'].astype(f32)
        chunk = T
    hm, C1, n1, m1 = _mlstm(qm, km, vm, ig, lf, C0, n0, m0, chunk)
    o_gate = jax.nn.sigmoid(o_b.astype(f32)).reshape(B, T, NH_B, DK_B)
    b_out = (o_gate * hm).reshape(B, T, B_WIDTH).astype(x.dtype)

    merged = jax.nn.sigmoid(g_a) * (a_out @ prm['w_pa']) + jax.nn.sigmoid(g_b) * (b_out @ prm['w_pb'])
    x = x + merged @ prm['w_o']

    h2 = _rms_norm(x, prm['norm2_g'])
    u = h2 @ prm['w_up']
    if past is None:
        fbuf = jnp.zeros((B, CONV_F - 1, 2 * D_FF), x.dtype)
    else:
        fbuf = past['fconv']
    u, new_fconv = _causal_dwconv(u, fbuf, prm['fconv_w'], prm['fconv_b'])
    x = x + (jax.nn.gelu(u[..., :D_FF]) * u[..., D_FF:]) @ prm['w_down']
    return x, (new_kv[0], new_kv[1], new_kv[2], new_mconv, C1, n1, m1, new_fconv)


def setup_inputs(seed: int = 0) -> dict:
    key = jax.random.key(seed)
    ks = jax.random.split(key, 32)
    f32 = jnp.float32

    def nrm(i, shape, scale):
        return scale * jax.random.normal(ks[i], shape, f32)

    lens = [min(w, PAST_LEN) for w in WINDOWS]
    i_bias = nrm(20, (DEPTH, NH_B), 0.1)
    f_bias = jnp.linspace(3.0, 6.0, NH_B, dtype=f32)[None, :] + nrm(21, (DEPTH, NH_B), 0.1)
    return {
        'x_prompt': nrm(0, (BATCH, SEQ, D_MODEL), 1.0),
        'x_sample': nrm(1, (DEC_BATCH, DEC_SEQ, D_MODEL), 1.0),
        'cache_kv_w128': nrm(2, (DEPTH, DEC_BATCH, lens[0], 2, H_G, DH_A), 1.0),
        'cache_kv_w512': nrm(3, (DEPTH, DEC_BATCH, lens[1], 2, H_G, DH_A), 1.0),
        'cache_kv_w2048': nrm(4, (DEPTH, DEC_BATCH, lens[2], 2, H_G, DH_A), 1.0),
        'state_mlstm_conv': nrm(5, (DEPTH, DEC_BATCH, CONV_B - 1, 2 * B_WIDTH), 1.0),
        'state_mlstm_C': nrm(6, (DEPTH, DEC_BATCH, NH_B, DK_B, DK_B), 0.1),
        'state_mlstm_n': nrm(7, (DEPTH, DEC_BATCH, NH_B, DK_B), 0.1),
        'state_mlstm_m': nrm(8, (DEPTH, DEC_BATCH, NH_B), 1.0),
        'state_ffn_conv': nrm(9, (DEPTH, DEC_BATCH, CONV_F - 1, 2 * D_FF), 1.0),
        'rel_bias': nrm(10, (NUM_BUCKETS, N_GROUPS * H_G), 0.2),
        'norm1_g': 1.0 + nrm(11, (DEPTH, D_MODEL), 0.02),
        'w_in': nrm(12, (DEPTH, D_MODEL, IN_COLS), D_MODEL ** -0.5),
        'mconv_w': nrm(13, (DEPTH, CONV_B, 2 * B_WIDTH), CONV_B ** -0.5),
        'mconv_b': nrm(14, (DEPTH, 2 * B_WIDTH), 0.02),
        'mgate_b': jnp.stack([i_bias, f_bias], axis=1),
        'w_pa': nrm(15, (DEPTH, A_GROUP, D_MODEL), A_GROUP ** -0.5),
        'w_pb': nrm(16, (DEPTH, B_WIDTH, D_MODEL), B_WIDTH ** -0.5),
        'w_o': nrm(17, (DEPTH, D_MODEL, D_MODEL), D_MODEL ** -0.5),
        'norm2_g': 1.0 + nrm(18, (DEPTH, D_MODEL), 0.02),
        'w_up': nrm(19, (DEPTH, D_MODEL, 2 * D_FF), D_MODEL ** -0.5),
        'fconv_w': nrm(22, (DEPTH, CONV_F, 2 * D_FF), CONV_F ** -0.5),
        'fconv_b': nrm(23, (DEPTH, 2 * D_FF), 0.02),
        'w_down': nrm(24, (DEPTH, D_FF, D_MODEL), D_FF ** -0.5),
        'final_norm_g': 1.0 + nrm(25, (D_MODEL,), 0.02),
    }


def reference(x_prompt, x_sample, cache_kv_w128, cache_kv_w512, cache_kv_w2048, state_mlstm_conv,
              state_mlstm_C, state_mlstm_n, state_mlstm_m, state_ffn_conv, rel_bias, norm1_g, w_in,
              mconv_w, mconv_b, mgate_b, w_pa, w_pb, w_o, norm2_g, w_up, fconv_w, fconv_b, w_down,
              final_norm_g):
    yp, ys = x_prompt, x_sample
    p_st = [[] for _ in range(8)]
    s_st = [[] for _ in range(8)]
    for l in range(DEPTH):
        prm = {'norm1_g': norm1_g[l], 'w_in': w_in[l], 'mconv_w': mconv_w[l], 'mconv_b': mconv_b[l],
               'mgate_b': mgate_b[l], 'w_pa': w_pa[l], 'w_pb': w_pb[l], 'w_o': w_o[l],
               'norm2_g': norm2_g[l], 'w_up': w_up[l], 'fconv_w': fconv_w[l], 'fconv_b': fconv_b[l],
               'w_down': w_down[l]}
        past = {'kv': (cache_kv_w128[l], cache_kv_w512[l], cache_kv_w2048[l]),
                'mconv': state_mlstm_conv[l], 'C': state_mlstm_C[l], 'n': state_mlstm_n[l],
                'm': state_mlstm_m[l], 'fconv': state_ffn_conv[l]}
        yp, st_p = _layer(yp, prm, rel_bias, None)
        ys, st_s = _layer(ys, prm, rel_bias, past)
        for i in range(8):
            p_st[i].append(st_p[i])
            s_st[i].append(st_s[i])
    y_prompt = _rms_norm(yp, final_norm_g)
    y_sample = _rms_norm(ys, final_norm_g)
    kv_w128_prompt = jnp.stack(p_st[0], 0)
    kv_w128_sample = jnp.stack(s_st[0], 0)
    kv_w512_prompt = jnp.stack(p_st[1], 0)
    kv_w512_sample = jnp.stack(s_st[1], 0)
    kv_w2048_prompt = jnp.stack(p_st[2], 0)
    kv_w2048_sample = jnp.stack(s_st[2], 0)
    mlstm_conv_prompt = jnp.stack(p_st[3], 0)
    mlstm_conv_sample = jnp.stack(s_st[3], 0)
    mlstm_C_prompt = jnp.stack(p_st[4], 0)
    mlstm_C_sample = jnp.stack(s_st[4], 0)
    mlstm_n_prompt = jnp.stack(p_st[5], 0)
    mlstm_n_sample = jnp.stack(s_st[5], 0)
    mlstm_m_prompt = jnp.stack(p_st[6], 0)
    mlstm_m_sample = jnp.stack(s_st[6], 0)
    ffn_conv_prompt = jnp.stack(p_st[7], 0)
    ffn_conv_sample = jnp.stack(s_st[7], 0)
    return (y_prompt, y_sample, kv_w128_prompt, kv_w128_sample, kv_w512_prompt, kv_w512_sample,
            kv_w2048_prompt, kv_w2048_sample, mlstm_conv_prompt, mlstm_conv_sample,
            mlstm_C_prompt, mlstm_C_sample, mlstm_n_prompt, mlstm_n_sample,
            mlstm_m_prompt, mlstm_m_sample, ffn_conv_prompt, ffn_conv_sample)
```

```python
import functools
import math

import numpy as np
import jax
import jax.numpy as jnp
from jax import lax
from jax.experimental import pallas as pl
from jax.experimental.pallas import tpu as pltpu

F32 = jnp.float32
BF16 = jnp.bfloat16
HIGHEST = lax.Precision.HIGHEST

D_MODEL = 1024
WINDOWS = (128, 512, 2048)
DILATIONS = (1, 4, 16)
N_GROUPS = 3
H_G = 8
DH_A = 64
A_GROUP = H_G * DH_A
A_QKV = N_GROUPS * A_GROUP
Q_BLOCK = 128
SPAN = 128
NH_B = 4
DK_B = 256
B_WIDTH = NH_B * DK_B
CONV_B = 4
D_FF = 2816
CONV_F = 3
NUM_BUCKETS = 32
MAX_DISTANCE = 2048
RMS_EPS = 1e-6
NEG = -1e30

LANES = 128
SUBLANES = 8
FF_CHUNK = 256
MLSTM_CHUNK = 256
VMEM_LIMIT = 56 * 1024 * 1024

COLS_QKV = 3 * A_QKV
COLS_QK = 2 * B_WIDTH
COLS_VO = 2 * B_WIDTH
COLS_GAB = 2 * D_MODEL
GATE_COL0 = COLS_QKV + COLS_QK + COLS_VO


def _dot(a, b):
    return jnp.dot(a, b, preferred_element_type=F32)


def _hdot(a, b):
    return jnp.dot(a, b, precision=HIGHEST, preferred_element_type=F32)


def _dot_nt(a, b):
    return lax.dot_general(a, b, (((1,), (1,)), ((), ())), preferred_element_type=F32)


def _dot_tn(a, b):
    return lax.dot_general(a, b, (((0,), (0,)), ((), ())), preferred_element_type=F32)


def _sigmoid(x):
    return 1.0 / (1.0 + jnp.exp(-x))


def _log_sigmoid(x):
    return jnp.minimum(x, 0.0) - jnp.log1p(jnp.exp(-jnp.abs(x)))


def _gelu_tanh(x):
    return 0.5 * x * (1.0 + jnp.tanh(math.sqrt(2.0 / math.pi) * (x + 0.044715 * (x * x * x))))


def _rms(x, g):
    return x * lax.rsqrt(jnp.mean(x * x, axis=-1, keepdims=True) + RMS_EPS) * g


def _resident(shape):
    nd = len(shape)
    return pl.BlockSpec(shape, lambda *_: (0,) * nd, pipeline_mode=pl.Buffered(1))


def _params(n_grid):
    return pltpu.CompilerParams(dimension_semantics=("arbitrary",) * n_grid,
                                vmem_limit_bytes=VMEM_LIMIT)


def _inproj_kernel(x_ref, g_ref, w_ref, wg_ref, qkv_ref, qk_ref, vo_ref, gab_ref, gcol_ref,
                   *grow_ref):
    h = _rms(x_ref[...], g_ref[...]).astype(BF16)
    col = 0
    for ref in (qkv_ref, qk_ref, vo_ref, gab_ref):
        width = ref.shape[-1]
        for c in range(0, width, 512):
            ref[:, c:c + 512] = _dot(h, w_ref[:, col + c:col + c + 512]).astype(ref.dtype)
        col += width
    gates = _dot(h, wg_ref[...])
    gcol_ref[...] = gates
    if grow_ref:
        grow_ref[0][...] = gates.T[:SUBLANES, :]


def _inproj(x, g, w_main, w_gate, *, tm, act_dtype, emit_grow):
    n = x.shape[0]
    out_shape = [
        jax.ShapeDtypeStruct((n, COLS_QKV), act_dtype),
        jax.ShapeDtypeStruct((n, COLS_QK), F32),
        jax.ShapeDtypeStruct((n, COLS_VO), act_dtype),
        jax.ShapeDtypeStruct((n, COLS_GAB), act_dtype),
        jax.ShapeDtypeStruct((n, LANES), F32),
    ]
    out_specs = [pl.BlockSpec((tm, s.shape[1]), lambda i: (i, 0)) for s in out_shape]
    if emit_grow:
        out_shape.append(jax.ShapeDtypeStruct((SUBLANES, n), F32))
        out_specs.append(pl.BlockSpec((SUBLANES, tm), lambda i: (0, i)))
    return pl.pallas_call(
        _inproj_kernel,
        grid=(n // tm,),
        in_specs=[
            pl.BlockSpec((tm, D_MODEL), lambda i: (i, 0)),
            _resident((1, D_MODEL)),
            _resident(w_main.shape),
            _resident(w_gate.shape),
        ],
        out_specs=out_specs,
        out_shape=out_shape,
        compiler_params=_params(1),
        name="inproj",
    )(x, g, w_main, w_gate)


def _attn_kernel(q_ref, kp_ref, kc_ref, vp_ref, vc_ref, bias_ref, o_ref, lse_ref):
    has_prev = pl.program_id(2) > 0
    lane = lax.broadcasted_iota(jnp.int32, (Q_BLOCK, LANES), 1)
    prev_mask = jnp.where(has_prev, 0.0, NEG)
    lse_all = jnp.zeros((Q_BLOCK, LANES), F32)
    for h in range(H_G):
        hs = slice(h * DH_A, (h + 1) * DH_A)
        q = q_ref[:, hs]
        bias = bias_ref[h]
        sp = _dot_nt(q, kp_ref[:, hs]) * (DH_A ** -0.5) + bias[:, :Q_BLOCK] + prev_mask
        sc = _dot_nt(q, kc_ref[:, hs]) * (DH_A ** -0.5) + bias[:, Q_BLOCK:]
        m = jnp.maximum(jnp.max(sp, axis=1, keepdims=True), jnp.max(sc, axis=1, keepdims=True))
        pp = jnp.exp(sp - m)
        pc = jnp.exp(sc - m)
        l = jnp.sum(pp, axis=1, keepdims=True) + jnp.sum(pc, axis=1, keepdims=True)
        o = _dot(pp.astype(BF16), vp_ref[:, hs]) + _dot(pc.astype(BF16), vc_ref[:, hs])
        o_ref[:, hs] = (o / l).astype(o_ref.dtype)
        lse_all = jnp.where(lane == h, m + jnp.log(l), lse_all)
    lse_ref[...] = lse_all


def _attn_prompt(qkv, bias, g, batch, seq):
    dil = DILATIONS[g]
    u_len = seq // dil
    nb = u_len // Q_BLOCK
    n_cb = COLS_QKV // A_GROUP
    view = qkv.reshape(batch, u_len, dil * COLS_QKV)

    def spec(col_block, prev):
        if prev:
            return pl.BlockSpec((None, Q_BLOCK, A_GROUP),
                                lambda b, r, j: (b, jnp.maximum(j - 1, 0), r * n_cb + col_block))
        return pl.BlockSpec((None, Q_BLOCK, A_GROUP), lambda b, r, j: (b, j, r * n_cb + col_block))

    o, lse = pl.pallas_call(
        _attn_kernel,
        grid=(batch, dil, nb),
        in_specs=[
            spec(g, False),
            spec(N_GROUPS + g, True), spec(N_GROUPS + g, False),
            spec(2 * N_GROUPS + g, True), spec(2 * N_GROUPS + g, False),
            _resident(bias.shape),
        ],
        out_specs=[
            pl.BlockSpec((None, Q_BLOCK, A_GROUP), lambda b, r, j: (b, j, r)),
            pl.BlockSpec((None, Q_BLOCK, LANES), lambda b, r, j: (b, j, r)),
        ],
        out_shape=[
            jax.ShapeDtypeStruct((batch, u_len, dil * A_GROUP), BF16),
            jax.ShapeDtypeStruct((batch, u_len, dil * LANES), F32),
        ],
        compiler_params=_params(3),
        name=f"attn_prompt_g{g}",
    )(view, view, view, view, view, bias)
    return o.reshape(batch * seq, A_GROUP), lse.reshape(batch * seq, LANES)


def _sattn_kernel(row_ref, kv0_ref, kv1_ref, kv2_ref, bm_ref, b0_ref, e_ref, et_ref,
                  o_ref, lse_ref):
    row = row_ref[...]
    e = e_ref[...]
    et = et_ref[...]
    for g, kv_ref in enumerate((kv0_ref, kv1_ref, kv2_ref)):
        gs = slice(g * A_GROUP, (g + 1) * A_GROUP)
        q = row[:, gs]
        k_new = row[:, A_QKV + g * A_GROUP:A_QKV + (g + 1) * A_GROUP]
        v_new = row[:, 2 * A_QKV + g * A_GROUP:2 * A_QKV + (g + 1) * A_GROUP]
        kv = kv_ref[...]
        k_old = kv[:, :A_GROUP]
        v_old = kv[:, A_GROUP:]
        s = _hdot(k_old * q, et) * (DH_A ** -0.5) + bm_ref[g]
        s0 = _hdot(k_new * q, et) * (DH_A ** -0.5) + b0_ref[g]
        m = jnp.maximum(jnp.max(s, axis=0, keepdims=True), s0)
        p = jnp.exp(s - m)
        p0 = jnp.exp(s0 - m)
        l = jnp.sum(p, axis=0, keepdims=True) + p0
        num = jnp.sum(_hdot(p, e) * v_old, axis=0, keepdims=True) + _hdot(p0, e) * v_new
        o_ref[:, gs] = num / _hdot(l, e)
        lse_ref[:, g * LANES:(g + 1) * LANES] = m + jnp.log(l)


def _attn_sample(qkv, caches, layer, bias_m, bias_0, e, et):
    nreq = qkv.shape[0]
    views = []
    specs = []
    for g, cache in enumerate(caches):
        dil = DILATIONS[g]
        n_buf = cache.shape[2]
        assert n_buf == SPAN * dil, "cache must hold exactly one window"
        views.append(cache.reshape(cache.shape[0], nreq, SPAN, dil * 2 * A_GROUP))
        specs.append(pl.BlockSpec((None, None, SPAN, 2 * A_GROUP),
                                  lambda b, layer=layer: (layer, b, 0, 0)))
    o, lse = pl.pallas_call(
        _sattn_kernel,
        grid=(nreq,),
        in_specs=[pl.BlockSpec((None, 1, COLS_QKV), lambda b: (b, 0, 0))] + specs + [
            _resident(bias_m.shape), _resident(bias_0.shape), _resident(e.shape),
            _resident(et.shape)],
        out_specs=[
            pl.BlockSpec((None, 1, A_QKV), lambda b: (b, 0, 0)),
            pl.BlockSpec((None, 1, N_GROUPS * LANES), lambda b: (b, 0, 0)),
        ],
        out_shape=[
            jax.ShapeDtypeStruct((nreq, 1, A_QKV), F32),
            jax.ShapeDtypeStruct((nreq, 1, N_GROUPS * LANES), F32),
        ],
        compiler_params=_params(1),
        name="attn_sample",
    )(qkv.reshape(nreq, 1, COLS_QKV), *views, bias_m, bias_0, e, et)
    return o.reshape(nreq, A_QKV), lse.reshape(nreq, N_GROUPS * LANES)


def _mlstm_kernel(qk_ref, vo_ref, gcol_ref, grow_ref, cw_ref, cb_ref, gb_row_ref, gb_col_ref,
                  bout_ref, c_ref, n_ref, m_ref, ext_s):
    chunk = qk_ref.shape[0]
    pad = SUBLANES

    @pl.when(pl.program_id(1) == 0)
    def _():
        c_ref[...] = jnp.zeros_like(c_ref)
        n_ref[...] = jnp.zeros_like(n_ref)
        m_ref[...] = jnp.zeros_like(m_ref)
        ext_s[0:pad, :] = jnp.zeros((pad, ext_s.shape[1]), F32)

    u = qk_ref[...]
    ext_s[pad:pad + chunk, :] = u
    y = cb_ref[...] + u * cw_ref[CONV_B - 1:CONV_B, :]
    for i in range(CONV_B - 1):
        lo = pad - (CONV_B - 1) + i
        y = y + ext_s[lo:lo + chunk, :] * cw_ref[i:i + 1, :]
    ext_s[0:pad, :] = u[chunk - pad:chunk, :]
    y = y * _sigmoid(y)
    qf = y[:, :B_WIDTH]
    kf = y[:, B_WIDTH:] * (DK_B ** -0.5)

    z_col = gcol_ref[...] + gb_row_ref[...]
    z_row = grow_ref[...] + gb_col_ref[...]
    ti = lax.broadcasted_iota(jnp.int32, (chunk, chunk), 0)
    si = lax.broadcasted_iota(jnp.int32, (chunk, chunk), 1)
    causal = ti >= si
    b_col = _hdot(causal.astype(F32), _log_sigmoid(z_col))
    b_row = _hdot(_log_sigmoid(z_row), (si >= ti).astype(F32))

    for h in range(NH_B):
        hs = slice(h * DK_B, (h + 1) * DK_B)
        fh = NH_B + h
        a_row = z_row[h:h + 1, :] - b_row[fh:fh + 1, :]
        a_col = z_col[:, h:h + 1] - b_col[:, fh:fh + 1]
        bh = b_col[:, fh:fh + 1]
        m_prev = m_ref[:, h:h + 1]
        a_mat = jnp.where(causal, a_row, NEG)
        gmax = jnp.maximum(m_prev, jnp.max(a_mat, axis=1, keepdims=True))
        dw = jnp.exp(a_mat - gmax)
        iw = jnp.exp(m_prev - gmax)
        qh = qf[:, hs]
        kh = kf[:, hs]
        qb = qh.astype(BF16)
        vh = vo_ref[:, hs]
        qk = _dot_nt(qb, kh.astype(BF16)) * dw
        c_old = c_ref[h]
        n_old = n_ref[h:h + 1, :]
        num = iw * _dot(qb, c_old.astype(BF16)) + _dot(qk.astype(BF16), vh)
        den = iw * jnp.sum(qh * n_old, axis=1, keepdims=True) + jnp.sum(qk, axis=1, keepdims=True)
        hid = num / jnp.maximum(jnp.abs(den), jnp.exp(-(bh + gmax)))
        o_gate = _sigmoid(vo_ref[:, B_WIDTH + h * DK_B:B_WIDTH + (h + 1) * DK_B].astype(F32))
        bout_ref[:, hs] = (o_gate * hid).astype(bout_ref.dtype)

        g_last = gmax[chunk - 1:chunk, :]
        decay = jnp.exp(m_prev - g_last)
        kw = kh * jnp.exp(a_col - g_last)
        c_ref[h] = decay * c_old + _dot_tn(kw.astype(BF16), vh)
        n_ref[h:h + 1, :] = decay * n_old + jnp.sum(kw, axis=0, keepdims=True)
        m_ref[:, h:h + 1] = bh[chunk - 1:chunk, :] + g_last


def _mlstm_prompt(qk_pre, vo, gcol, grow, cw, cb, gb_row, gb_col, batch, seq):
    chunk = MLSTM_CHUNK
    nc = seq // chunk
    n = batch * seq
    row = lambda b, c: (b * nc + c, 0)
    return pl.pallas_call(
        _mlstm_kernel,
        grid=(batch, nc),
        in_specs=[
            pl.BlockSpec((chunk, COLS_QK), row),
            pl.BlockSpec((chunk, COLS_VO), row),
            pl.BlockSpec((chunk, LANES), row),
            pl.BlockSpec((SUBLANES, chunk), lambda b, c: (0, b * nc + c)),
            _resident(cw.shape), _resident(cb.shape), _resident(gb_row.shape),
            _resident(gb_col.shape),
        ],
        out_specs=[
            pl.BlockSpec((chunk, B_WIDTH), row),
            pl.BlockSpec((None, NH_B, DK_B, DK_B), lambda b, c: (b, 0, 0, 0)),
            pl.BlockSpec((None, NH_B, DK_B), lambda b, c: (b, 0, 0)),
            pl.BlockSpec((None, 1, LANES), lambda b, c: (b, 0, 0)),
        ],
        out_shape=[
            jax.ShapeDtypeStruct((n, B_WIDTH), BF16),
            jax.ShapeDtypeStruct((batch, NH_B, DK_B, DK_B), F32),
            jax.ShapeDtypeStruct((batch, NH_B, DK_B), F32),
            jax.ShapeDtypeStruct((batch, 1, LANES), F32),
        ],
        scratch_shapes=[pltpu.VMEM((chunk + SUBLANES, COLS_QK), F32)],
        compiler_params=_params(2),
        name="mlstm_prompt",
    )(qk_pre, vo, gcol, grow, cw, cb, gb_row, gb_col)


def _smlstm_kernel(qk_ref, vo_ref, g_ref, gb_ref, cs_ref, cw_ref, cb_ref, c0_ref, n0_ref, m0_ref,
                   bout_ref, c1_ref, n1_ref, m1_ref, ncs_ref):
    u = qk_ref[...]
    y = cb_ref[...] + u * cw_ref[CONV_B - 1:CONV_B, :]
    for i in range(CONV_B - 1):
        y = y + cs_ref[i:i + 1, :] * cw_ref[i:i + 1, :]
    ncs_ref[0:CONV_B - 2, :] = cs_ref[1:CONV_B - 1, :]
    ncs_ref[CONV_B - 2:CONV_B - 1, :] = u
    y = y * _sigmoid(y)
    qf = y[:, :B_WIDTH]
    kf = y[:, B_WIDTH:] * (DK_B ** -0.5)
    z = g_ref[...] + gb_ref[...]
    lf_all = _log_sigmoid(z)
    m0 = m0_ref[...]
    eye = (lax.broadcasted_iota(jnp.int32, (DK_B, DK_B), 0)
           == lax.broadcasted_iota(jnp.int32, (DK_B, DK_B), 1))
    for h in range(NH_B):
        hs = slice(h * DK_B, (h + 1) * DK_B)
        ig = z[:, h:h + 1]
        inter = lf_all[:, NH_B + h:NH_B + h + 1] + m0[:, h:h + 1]
        m_t = jnp.maximum(inter, ig)
        dw = jnp.exp(ig - m_t)
        iw = jnp.exp(inter - m_t)
        qh = qf[:, hs]
        kh = kf[:, hs]
        vh = vo_ref[:, hs]
        c_old = c0_ref[h]
        n_old = n0_ref[h:h + 1, :]
        qk = jnp.sum(qh * kh, axis=1, keepdims=True) * dw
        num = iw * _hdot(qh, c_old) + qk * vh
        den = iw * jnp.sum(qh * n_old, axis=1, keepdims=True) + qk
        hid = num / jnp.maximum(jnp.abs(den), jnp.exp(-m_t))
        o_gate = _sigmoid(vo_ref[:, B_WIDTH + h * DK_B:B_WIDTH + (h + 1) * DK_B])
        bout_ref[:, hs] = o_gate * hid
        k_col = jnp.sum(jnp.where(eye, kh, 0.0), axis=1, keepdims=True)
        c1_ref[h] = iw * c_old + (dw * k_col) * vh
        n1_ref[h:h + 1, :] = iw * n_old + dw * kh
        m1_ref[:, h:h + 1] = m_t


def _mlstm_sample(qk_pre, vo, gcol, gb_row, conv_state, cw, cb, c0, n0, m0, layer):
    nreq = qk_pre.shape[0]
    one = lambda b: (b, 0, 0)
    lay3 = lambda b, layer=layer: (layer, b, 0, 0)
    lay4 = lambda b, layer=layer: (layer, b, 0, 0, 0)
    return pl.pallas_call(
        _smlstm_kernel,
        grid=(nreq,),
        in_specs=[
            pl.BlockSpec((None, 1, COLS_QK), one),
            pl.BlockSpec((None, 1, COLS_VO), one),
            pl.BlockSpec((None, 1, LANES), one),
            _resident(gb_row.shape),
            pl.BlockSpec((None, None, CONV_B - 1, COLS_QK), lay3),
            _resident(cw.shape), _resident(cb.shape),
            pl.BlockSpec((None, None, NH_B, DK_B, DK_B), lay4),
            pl.BlockSpec((None, None, NH_B, DK_B), lay3),
            pl.BlockSpec((None, None, 1, LANES), lay3),
        ],
        out_specs=[
            pl.BlockSpec((None, 1, B_WIDTH), one),
            pl.BlockSpec((None, NH_B, DK_B, DK_B), lambda b: (b, 0, 0, 0)),
            pl.BlockSpec((None, NH_B, DK_B), one),
            pl.BlockSpec((None, 1, LANES), one),
            pl.BlockSpec((None, CONV_B - 1, COLS_QK), one),
        ],
        out_shape=[
            jax.ShapeDtypeStruct((nreq, 1, B_WIDTH), F32),
            jax.ShapeDtypeStruct((nreq, NH_B, DK_B, DK_B), F32),
            jax.ShapeDtypeStruct((nreq, NH_B, DK_B), F32),
            jax.ShapeDtypeStruct((nreq, 1, LANES), F32),
            jax.ShapeDtypeStruct((nreq, CONV_B - 1, COLS_QK), F32),
        ],
        compiler_params=_params(1),
        name="mlstm_sample",
    )(qk_pre.reshape(nreq, 1, COLS_QK), vo.reshape(nreq, 1, COLS_VO),
      gcol.reshape(nreq, 1, LANES), gb_row, conv_state, cw, cb, c0, n0, m0)


def _merge_kernel(o0_ref, o1_ref, o2_ref, l0_ref, l1_ref, l2_ref, bo_ref, gab_ref, x_ref,
                  wpa_ref, wpb_ref, wo_ref, g2_ref, e_ref, x2_ref, h2_ref):
    lses = [r[...] for r in (l0_ref, l1_ref, l2_ref)]
    top = jnp.maximum(jnp.maximum(lses[0], lses[1]), lses[2])
    ws = [jnp.exp(l - top) for l in lses]
    inv = 1.0 / (ws[0] + ws[1] + ws[2])
    e = e_ref[...]
    a = None
    for w, o_ref in zip(ws, (o0_ref, o1_ref, o2_ref)):
        alpha = w * inv
        hi = alpha.astype(BF16)
        lo = (alpha - hi.astype(F32)).astype(BF16)
        term = (_dot(hi, e) + _dot(lo, e)) * o_ref[...].astype(F32)
        a = term if a is None else a + term
    pa = _dot(a.astype(BF16), wpa_ref[...])
    pb = _dot(bo_ref[...].astype(BF16), wpb_ref[...])
    merged = (_sigmoid(gab_ref[:, :D_MODEL].astype(F32)) * pa
              + _sigmoid(gab_ref[:, D_MODEL:].astype(F32)) * pb)
    x2 = x_ref[...] + _dot(merged.astype(BF16), wo_ref[...])
    x2_ref[...] = x2
    h2_ref[...] = _rms(x2, g2_ref[...]).astype(BF16)


def _merge(os_, lses, bout, gab, x, wpa, wpb, wo, g2, e_bf, *, tm):
    n = x.shape[0]
    row = lambda i: (i, 0)
    return pl.pallas_call(
        _merge_kernel,
        grid=(n // tm,),
        in_specs=[pl.BlockSpec((tm, A_GROUP), row)] * 3 + [pl.BlockSpec((tm, LANES), row)] * 3 + [
            pl.BlockSpec((tm, B_WIDTH), row),
            pl.BlockSpec((tm, COLS_GAB), row),
            pl.BlockSpec((tm, D_MODEL), row),
            _resident(wpa.shape), _resident(wpb.shape), _resident(wo.shape),
            _resident(g2.shape), _resident(e_bf.shape),
        ],
        out_specs=[pl.BlockSpec((tm, D_MODEL), row), pl.BlockSpec((tm, D_MODEL), row)],
        out_shape=[jax.ShapeDtypeStruct((n, D_MODEL), F32), jax.ShapeDtypeStruct((n, D_MODEL), BF16)],
        compiler_params=_params(1),
        name="merge",
    )(*os_, *lses, bout, gab, x, wpa, wpb, wo, g2, e_bf)


def _ffn_kernel(*refs, tiles_per_batch, from_state, final_norm):
    h2_ref, x2_ref, wup_ref, wdn_ref, cw_ref, cb_ref = refs[:6]
    pos = 6
    if from_state:
        prev_refs = refs[pos:pos + CONV_F - 1]
        pos += CONV_F - 1
    if final_norm:
        fg_ref = refs[pos]
        pos += 1
    x3_ref, u_ref = refs[pos:pos + 2]
    tm = h2_ref.shape[0]
    pad = SUBLANES
    if not from_state:
        ext_s = refs[pos + 2]

        @pl.when(pl.program_id(0) % tiles_per_batch == 0)
        def _():
            ext_s[0:pad, :] = jnp.zeros((pad, ext_s.shape[1]), F32)

    h2 = h2_ref[...]
    acc = x2_ref[...]
    for c in range(D_FF // FF_CHUNK):
        ys = []
        for off in (c * FF_CHUNK, D_FF + c * FF_CHUNK):
            cs = slice(off, off + FF_CHUNK)
            u = _dot(h2, wup_ref[:, cs])
            y = cb_ref[:, cs] + u * cw_ref[CONV_F - 1:CONV_F, cs]
            if from_state:
                u_ref[:, cs] = u
                for i in range(CONV_F - 1):
                    y = y + prev_refs[i][:, cs] * cw_ref[i:i + 1, cs]
            else:
                ext_s[pad:pad + tm, cs] = u
                for i in range(CONV_F - 1):
                    lo = pad - (CONV_F - 1) + i
                    y = y + ext_s[lo:lo + tm, cs] * cw_ref[i:i + 1, cs]
            ys.append(y)
        act = _gelu_tanh(ys[0]) * ys[1]
        acc = acc + _dot(act.astype(BF16), wdn_ref[c * FF_CHUNK:(c + 1) * FF_CHUNK, :])
    if not from_state:
        tail = ext_s[tm:tm + pad, :]
        u_ref[...] = tail
        ext_s[0:pad, :] = tail
    if final_norm:
        acc = _rms(acc, fg_ref[...])
    x3_ref[...] = acc


def _ffn(h2, x2, wup, wdn, cw, cb, *, tm, tiles_per_batch=None, prev_rows=None, final_g=None):
    n = h2.shape[0]
    from_state = prev_rows is not None
    row = lambda i: (i, 0)
    in_specs = [
        pl.BlockSpec((tm, D_MODEL), row), pl.BlockSpec((tm, D_MODEL), row),
        _resident(wup.shape), _resident(wdn.shape), _resident(cw.shape), _resident(cb.shape),
    ]
    args = [h2, x2, wup, wdn, cw, cb]
    scratch = []
    if from_state:
        in_specs += [pl.BlockSpec((tm, 2 * D_FF), row)] * (CONV_F - 1)
        args += list(prev_rows)
        u_shape = jax.ShapeDtypeStruct((n, 2 * D_FF), F32)
        u_spec = pl.BlockSpec((tm, 2 * D_FF), row)
    else:
        n_batch = n // (tm * tiles_per_batch)
        u_shape = jax.ShapeDtypeStruct((n_batch, SUBLANES, 2 * D_FF), F32)
        u_spec = pl.BlockSpec((None, SUBLANES, 2 * D_FF), lambda i: (i // tiles_per_batch, 0, 0))
        scratch = [pltpu.VMEM((tm + SUBLANES, 2 * D_FF), F32)]
    if final_g is not None:
        in_specs.append(_resident(final_g.shape))
        args.append(final_g)
    return pl.pallas_call(
        functools.partial(_ffn_kernel, tiles_per_batch=tiles_per_batch, from_state=from_state,
                          final_norm=final_g is not None),
        grid=(n // tm,),
        in_specs=in_specs,
        out_specs=[pl.BlockSpec((tm, D_MODEL), row), u_spec],
        out_shape=[jax.ShapeDtypeStruct((n, D_MODEL), F32), u_shape],
        scratch_shapes=scratch,
        compiler_params=_params(1),
        name="ffn",
    )(*args)


def _t5_bucket(dist):
    max_exact = NUM_BUCKETS // 2
    df = np.maximum(dist, 1).astype(np.float32)
    large = max_exact + (np.log(df / max_exact) / np.float32(math.log(MAX_DISTANCE / max_exact))
                         * (NUM_BUCKETS - max_exact)).astype(np.int32)
    return np.where(dist < max_exact, dist, np.minimum(large, NUM_BUCKETS - 1))


def _prompt_bias(rel_bias, g):
    qi = np.arange(Q_BLOCK)[:, None]
    ki = np.arange(2 * Q_BLOCK)[None, :]
    rel = qi + Q_BLOCK - ki
    band = (rel >= 0) & (rel <= SPAN)
    bucket = _t5_bucket(np.maximum(rel, 0) * DILATIONS[g])
    bias = rel_bias[:, g * H_G:(g + 1) * H_G][bucket]
    return jnp.where(band[None], bias.transpose(2, 0, 1), NEG)


def _sample_bias(rel_bias):
    rows, news = [], []
    for g in range(N_GROUPS):
        bucket = _t5_bucket(np.arange(SPAN + 1) * DILATIONS[g])
        bias = rel_bias[:, g * H_G:(g + 1) * H_G][bucket]
        bias = jnp.pad(bias, ((0, 0), (0, LANES - H_G)))
        rows.append(bias[:0:-1])
        news.append(bias[0:1])
    return jnp.stack(rows), jnp.stack(news)


def _head_indicator():
    e = np.zeros((LANES, A_GROUP), np.float32)
    for h in range(H_G):
        e[h, h * DH_A:(h + 1) * DH_A] = 1.0
    return e


def kernel(x_prompt, x_sample, cache_kv_w128, cache_kv_w512, cache_kv_w2048, state_mlstm_conv,
           state_mlstm_C, state_mlstm_n, state_mlstm_m, state_ffn_conv, rel_bias, norm1_g, w_in,
           mconv_w, mconv_b, mgate_b, w_pa, w_pb, w_o, norm2_g, w_up, fconv_w, fconv_b, w_down,
           final_norm_g):
    batch, seq, _ = x_prompt.shape
    nreq = x_sample.shape[0]
    depth = w_in.shape[0]
    n_p = batch * seq
    caches = (cache_kv_w128, cache_kv_w512, cache_kv_w2048)

    e_np = _head_indicator()
    e_f32 = jnp.asarray(e_np)
    et_f32 = jnp.asarray(e_np.T)
    e_bf = jnp.asarray(e_np, BF16)
    prompt_bias = [_prompt_bias(rel_bias, g) for g in range(N_GROUPS)]
    sbias_m, sbias_0 = _sample_bias(rel_bias)
    m0_all = jnp.pad(state_mlstm_m, ((0, 0), (0, 0), (0, LANES - NH_B)))[:, :, None, :]
    fg = final_norm_g.reshape(1, D_MODEL)

    xp = x_prompt.reshape(n_p, D_MODEL)
    xs = x_sample.reshape(nreq, D_MODEL)
    p_st = [[] for _ in range(8)]
    s_st = [[] for _ in range(8)]

    for l in range(depth):
        last = l == depth - 1
        w_l = w_in[l]
        w_main = jnp.concatenate([w_l[:, :GATE_COL0], w_l[:, GATE_COL0 + 2 * NH_B:]], axis=1).astype(BF16)
        w_gate = jnp.pad(w_l[:, GATE_COL0:GATE_COL0 + 2 * NH_B], ((0, 0), (0, LANES - 2 * NH_B))).astype(BF16)
        g1 = norm1_g[l].reshape(1, D_MODEL)
        g2 = norm2_g[l].reshape(1, D_MODEL)
        gate_bias = mgate_b[l].reshape(1, 2 * NH_B)
        gb_row = jnp.pad(gate_bias, ((0, 0), (0, LANES - 2 * NH_B)))
        gb_col = gate_bias.reshape(2 * NH_B, 1)
        mcw, mcb = mconv_w[l], mconv_b[l].reshape(1, COLS_QK)
        fcw, fcb = fconv_w[l], fconv_b[l].reshape(1, 2 * D_FF)
        wpa, wpb, wo = w_pa[l].astype(BF16), w_pb[l].astype(BF16), w_o[l].astype(BF16)
        wup, wdn = w_up[l].astype(BF16), w_down[l].astype(BF16)

        qkv, qk_pre, vo, gab, gcol, grow = _inproj(xp, g1, w_main, w_gate, tm=256, act_dtype=BF16,
                                                   emit_grow=True)
        os_, lses = zip(*[_attn_prompt(qkv, prompt_bias[g], g, batch, seq) for g in range(N_GROUPS)])
        bout, c_p, n_p_state, m_p = _mlstm_prompt(qk_pre, vo, gcol, grow, mcw, mcb, gb_row, gb_col,
                                                  batch, seq)
        x2, h2 = _merge(os_, lses, bout, gab, xp, wpa, wpb, wo, g2, e_bf, tm=512)
        xp, u_tail = _ffn(h2, x2, wup, wdn, fcw, fcb, tm=256, tiles_per_batch=seq // 256,
                          final_g=fg if last else None)

        kv5 = qkv.reshape(batch, seq, 3, N_GROUPS, H_G, DH_A)
        for g in range(N_GROUPS):
            keep = min(WINDOWS[g], seq)
            p_st[g].append(jnp.stack([kv5[:, seq - keep:, 1, g], kv5[:, seq - keep:, 2, g]],
                                     axis=2).astype(F32))
        p_st[3].append(qk_pre.reshape(batch, seq, COLS_QK)[:, seq - (CONV_B - 1):])
        p_st[4].append(c_p)
        p_st[5].append(n_p_state)
        p_st[6].append(m_p[:, 0, :NH_B])
        p_st[7].append(u_tail[:, SUBLANES - (CONV_F - 1):])

        qkv_s, qk_s, vo_s, gab_s, gcol_s = _inproj(xs, g1, w_main, w_gate, tm=nreq, act_dtype=F32,
                                                   emit_grow=False)
        o_s, lse_s = _attn_sample(qkv_s, caches, l, sbias_m, sbias_0, e_f32, et_f32)
        bout_s, c_s, n_s, m_s, conv_s = _mlstm_sample(qk_s, vo_s, gcol_s, gb_row, state_mlstm_conv,
                                                      mcw, mcb, state_mlstm_C, state_mlstm_n, m0_all, l)
        x2_s, h2_s = _merge([o_s[:, g * A_GROUP:(g + 1) * A_GROUP] for g in range(N_GROUPS)],
                            [lse_s[:, g * LANES:(g + 1) * LANES] for g in range(N_GROUPS)],
                            bout_s.reshape(nreq, B_WIDTH), gab_s, xs, wpa, wpb, wo, g2, e_bf, tm=nreq)
        fbuf = state_ffn_conv[l]
        xs, u_s = _ffn(h2_s, x2_s, wup, wdn, fcw, fcb, tm=nreq,
                       prev_rows=[fbuf[:, i] for i in range(CONV_F - 1)],
                       final_g=fg if last else None)

        kv5s = qkv_s.reshape(nreq, 1, 3, N_GROUPS, H_G, DH_A)
        for g in range(N_GROUPS):
            s_st[g].append(jnp.stack([kv5s[:, :, 1, g], kv5s[:, :, 2, g]], axis=2))
        s_st[3].append(conv_s)
        s_st[4].append(c_s)
        s_st[5].append(n_s)
        s_st[6].append(m_s[:, 0, :NH_B])
        s_st[7].append(jnp.concatenate([fbuf[:, 1:], u_s[:, None, :]], axis=1))

    outs = [xp.reshape(batch, seq, D_MODEL), xs.reshape(nreq, 1, D_MODEL)]
    for i in range(8):
        outs.append(jnp.stack(p_st[i], 0))
        outs.append(jnp.stack(s_st[i], 0))
    return tuple(outs)
```

```python
import functools
import math

import numpy as np
import jax
import jax.numpy as jnp
from jax import lax
from jax.experimental import pallas as pl
from jax.experimental.pallas import tpu as pltpu

F32 = jnp.float32
BF16 = jnp.bfloat16
HIGHEST = lax.Precision.HIGHEST

D_MODEL = 1024
WINDOWS = (128, 512, 2048)
DILATIONS = (1, 4, 16)
N_GROUPS = 3
H_G = 8
DH_A = 64
A_GROUP = H_G * DH_A
A_QKV = N_GROUPS * A_GROUP
Q_BLOCK = 128
SPAN = 128
NH_B = 4
DK_B = 256
B_WIDTH = NH_B * DK_B
CONV_B = 4
D_FF = 2816
CONV_F = 3
NUM_BUCKETS = 32
MAX_DISTANCE = 2048
RMS_EPS = 1e-6
NEG = -1e30

LANES = 128
SUBLANES = 8
FF_CHUNK = 256
MLSTM_CHUNK = 256
VMEM_LIMIT = 56 * 1024 * 1024

COLS_QKV_G = 3 * A_GROUP
COLS_QKV = N_GROUPS * COLS_QKV_G
COLS_QK = 2 * B_WIDTH
COLS_VO = 2 * B_WIDTH
COLS_GAB = 2 * D_MODEL
GATE_COL0 = COLS_QKV + COLS_QK + COLS_VO


def _dot(a, b):
    return jnp.dot(a, b, preferred_element_type=F32)


def _hdot(a, b):
    return jnp.dot(a, b, precision=HIGHEST, preferred_element_type=F32)


def _dot_nt(a, b):
    return lax.dot_general(a, b, (((1,), (1,)), ((), ())), preferred_element_type=F32)


def _dot_tn(a, b):
    return lax.dot_general(a, b, (((0,), (0,)), ((), ())), preferred_element_type=F32)


def _sigmoid(x):
    return 1.0 / (1.0 + jnp.exp(-x))


def _log_sigmoid(x):
    return jnp.minimum(x, 0.0) - jnp.log1p(jnp.exp(-jnp.abs(x)))


def _gelu_tanh(x):
    return 0.5 * x * (1.0 + jnp.tanh(math.sqrt(2.0 / math.pi) * (x + 0.044715 * (x * x * x))))


def _rms(x, g):
    return x * lax.rsqrt(jnp.mean(x * x, axis=-1, keepdims=True) + RMS_EPS) * g


def _resident(shape):
    nd = len(shape)
    return pl.BlockSpec(shape, lambda *_: (0,) * nd, pipeline_mode=pl.Buffered(1))


def _params(n_grid):
    return pltpu.CompilerParams(dimension_semantics=("arbitrary",) * n_grid,
                                vmem_limit_bytes=VMEM_LIMIT)


def _inproj_kernel(x_ref, g_ref, w_ref, wg_ref, qkv0_ref, qkv1_ref, qkv2_ref, qk_ref, vo_ref,
                   gab_ref, gcol_ref, *rest, dilate):
    hf = _rms(x_ref[...], g_ref[...])
    h = hf.astype(BF16)
    tm = hf.shape[0]

    def project(lhs, col0, width, store):
        for c in range(0, width, 512):
            store(c, _dot(lhs, w_ref[:, col0 + c:col0 + c + 512]))

    def store_rows(ref):
        def store(c, res):
            ref[:, c:c + 512] = res.astype(ref.dtype)
        return store

    project(h, 0, COLS_QKV_G, store_rows(qkv0_ref))
    if dilate:
        grow_ref, hs_ref = rest
        n_slab = hs_ref.shape[0]
        for k in range(n_slab):
            hs_ref[k] = hf[:, k * LANES:(k + 1) * LANES]
        for g, ref in ((1, qkv1_ref), (2, qkv2_ref)):
            d = DILATIONS[g]
            rows = tm // d
            hp = jnp.concatenate(
                [jnp.concatenate([hs_ref[k, pl.ds(r, rows, stride=d), :] for r in range(d)], axis=0)
                 for k in range(n_slab)], axis=1).astype(BF16)

            def store(c, res, ref=ref, d=d, rows=rows):
                res = res.astype(ref.dtype)
                for r in range(d):
                    ref[r, :, c:c + 512] = res[r * rows:(r + 1) * rows]
            project(hp, g * COLS_QKV_G, COLS_QKV_G, store)
    else:
        project(h, COLS_QKV_G, COLS_QKV_G, store_rows(qkv1_ref))
        project(h, 2 * COLS_QKV_G, COLS_QKV_G, store_rows(qkv2_ref))
    col = COLS_QKV
    for ref in (qk_ref, vo_ref, gab_ref):
        project(h, col, ref.shape[-1], store_rows(ref))
        col += ref.shape[-1]
    gates = _dot(h, wg_ref[...])
    gcol_ref[...] = gates
    if dilate:
        grow_ref[...] = gates.T[:SUBLANES, :]


def _inproj(x, gain, w_main, w_gate, *, tm, act_dtype, seq=None):
    n = x.shape[0]
    dilate = seq is not None
    row = lambda i: (i, 0)
    out_shape = [jax.ShapeDtypeStruct((n, COLS_QKV_G), act_dtype)]
    out_specs = [pl.BlockSpec((tm, COLS_QKV_G), row)]
    for g in (1, 2):
        if dilate:
            d = DILATIONS[g]
            tpb = seq // tm
            out_shape.append(jax.ShapeDtypeStruct((n // seq, d, seq // d, COLS_QKV_G), act_dtype))
            out_specs.append(pl.BlockSpec((None, d, tm // d, COLS_QKV_G),
                                          lambda i, tpb=tpb: (i // tpb, 0, i % tpb, 0)))
        else:
            out_shape.append(jax.ShapeDtypeStruct((n, COLS_QKV_G), act_dtype))
            out_specs.append(pl.BlockSpec((tm, COLS_QKV_G), row))
    for cols, dt in ((COLS_QK, F32), (COLS_VO, act_dtype), (COLS_GAB, act_dtype), (LANES, F32)):
        out_shape.append(jax.ShapeDtypeStruct((n, cols), dt))
        out_specs.append(pl.BlockSpec((tm, cols), row))
    scratch = []
    if dilate:
        out_shape.append(jax.ShapeDtypeStruct((SUBLANES, n), F32))
        out_specs.append(pl.BlockSpec((SUBLANES, tm), lambda i: (0, i)))
        scratch = [pltpu.VMEM((D_MODEL // LANES, tm, LANES), F32)]
    return pl.pallas_call(
        functools.partial(_inproj_kernel, dilate=dilate),
        grid=(n // tm,),
        in_specs=[
            pl.BlockSpec((tm, D_MODEL), row),
            _resident((1, D_MODEL)),
            _resident(w_main.shape),
            _resident(w_gate.shape),
        ],
        out_specs=out_specs,
        out_shape=out_shape,
        scratch_shapes=scratch,
        compiler_params=_params(1),
        name="inproj",
    )(x, gain, w_main, w_gate)


def _attn_kernel(q_ref, kp_ref, kc_ref, vp_ref, vc_ref, bias_ref, o_ref, lse_ref):
    has_prev = pl.program_id(1) > 0
    lane = lax.broadcasted_iota(jnp.int32, (Q_BLOCK, LANES), 1)
    prev_mask = jnp.where(has_prev, 0.0, NEG)
    lse_all = jnp.zeros((Q_BLOCK, LANES), F32)
    for h in range(H_G):
        hs = slice(h * DH_A, (h + 1) * DH_A)
        q = q_ref[:, hs]
        bias = bias_ref[h]
        sp = _dot_nt(q, kp_ref[:, hs]) * (DH_A ** -0.5) + bias[:, :Q_BLOCK] + prev_mask
        sc = _dot_nt(q, kc_ref[:, hs]) * (DH_A ** -0.5) + bias[:, Q_BLOCK:]
        m = jnp.maximum(jnp.max(sp, axis=1, keepdims=True), jnp.max(sc, axis=1, keepdims=True))
        pp = jnp.exp(sp - m)
        pc = jnp.exp(sc - m)
        l = jnp.sum(pp, axis=1, keepdims=True) + jnp.sum(pc, axis=1, keepdims=True)
        o = _dot(pp.astype(BF16), vp_ref[:, hs]) + _dot(pc.astype(BF16), vc_ref[:, hs])
        o_ref[:, hs] = (o / l).astype(o_ref.dtype)
        lse_all = jnp.where(lane == h, m + jnp.log(l), lse_all)
    lse_ref[...] = lse_all


def _attn_prompt(qkv, bias):
    nsub, u_len, _ = qkv.shape
    nb = u_len // Q_BLOCK

    def spec(col_block, prev):
        if prev:
            return pl.BlockSpec((None, Q_BLOCK, A_GROUP),
                                lambda s, j: (s, jnp.maximum(j - 1, 0), col_block))
        return pl.BlockSpec((None, Q_BLOCK, A_GROUP), lambda s, j: (s, j, col_block))

    return pl.pallas_call(
        _attn_kernel,
        grid=(nsub, nb),
        in_specs=[spec(0, False), spec(1, True), spec(1, False), spec(2, True), spec(2, False),
                  _resident(bias.shape)],
        out_specs=[
            pl.BlockSpec((None, Q_BLOCK, A_GROUP), lambda s, j: (s, j, 0)),
            pl.BlockSpec((None, Q_BLOCK, LANES), lambda s, j: (s, j, 0)),
        ],
        out_shape=[
            jax.ShapeDtypeStruct((nsub, u_len, A_GROUP), BF16),
            jax.ShapeDtypeStruct((nsub, u_len, LANES), F32),
        ],
        compiler_params=_params(2),
        name="attn_prompt",
    )(qkv, qkv, qkv, qkv, qkv, bias)


def _sattn_kernel(new_ref, kv0_ref, kv1_ref, kv2_ref, bm_ref, b0_ref, o_ref, lse_ref):
    for g, kv_ref in enumerate((kv0_ref, kv1_ref, kv2_ref)):
        q = new_ref[g, 0]
        k_new = new_ref[g, 1]
        v_new = new_ref[g, 2]
        k_old = kv_ref[:, 0]
        v_old = kv_ref[:, 1]
        s = jnp.sum(k_old * q, axis=-1, keepdims=True) * (DH_A ** -0.5) + bm_ref[g]
        s0 = jnp.sum(k_new * q, axis=-1, keepdims=True) * (DH_A ** -0.5) + b0_ref[g]
        m = jnp.maximum(jnp.max(s, axis=0), s0)
        p = jnp.exp(s - m)
        p0 = jnp.exp(s0 - m)
        l = jnp.sum(p, axis=0) + p0
        o_ref[g] = (jnp.sum(p * v_old, axis=0) + p0 * v_new) / l
        lse_ref[g] = m + jnp.log(l)


def _attn_sample(new_qkv, caches, layer, bias_m, bias_0):
    nreq = new_qkv.shape[0]
    views = []
    specs = []
    for g, cache in enumerate(caches):
        dil = DILATIONS[g]
        depth, _, n_buf = cache.shape[:3]
        assert n_buf == SPAN * dil, "cache must hold exactly one window"
        views.append(cache.reshape(depth, nreq, SPAN, dil, 2, H_G, DH_A))
        specs.append(pl.BlockSpec((None, None, SPAN, None, 2, H_G, DH_A),
                                  lambda b, layer=layer: (layer, b, 0, 0, 0, 0, 0)))
    out = jax.ShapeDtypeStruct((nreq, N_GROUPS, H_G, DH_A), F32)
    out_spec = pl.BlockSpec((None, N_GROUPS, H_G, DH_A), lambda b: (b, 0, 0, 0))
    return pl.pallas_call(
        _sattn_kernel,
        grid=(nreq,),
        in_specs=[pl.BlockSpec((None, N_GROUPS, 3, H_G, DH_A), lambda b: (b, 0, 0, 0, 0))] + specs
        + [_resident(bias_m.shape), _resident(bias_0.shape)],
        out_specs=[out_spec, out_spec],
        out_shape=[out, out],
        compiler_params=_params(1),
        name="attn_sample",
    )(new_qkv, *views, bias_m, bias_0)


def _mlstm_kernel(qk_ref, vo_ref, gcol_ref, grow_ref, cw_ref, cb_ref, gb_row_ref, gb_col_ref,
                  bout_ref, c_ref, n_ref, m_ref, ext_s):
    chunk = qk_ref.shape[0]
    pad = SUBLANES

    @pl.when(pl.program_id(1) == 0)
    def _():
        c_ref[...] = jnp.zeros_like(c_ref)
        n_ref[...] = jnp.zeros_like(n_ref)
        m_ref[...] = jnp.zeros_like(m_ref)
        ext_s[0:pad, :] = jnp.zeros((pad, ext_s.shape[1]), F32)

    u = qk_ref[...]
    ext_s[pad:pad + chunk, :] = u
    y = cb_ref[...] + u * cw_ref[CONV_B - 1:CONV_B, :]
    for i in range(CONV_B - 1):
        lo = pad - (CONV_B - 1) + i
        y = y + ext_s[lo:lo + chunk, :] * cw_ref[i:i + 1, :]
    ext_s[0:pad, :] = u[chunk - pad:chunk, :]
    y = y * _sigmoid(y)
    qf = y[:, :B_WIDTH]
    kf = y[:, B_WIDTH:] * (DK_B ** -0.5)

    z_col = gcol_ref[...] + gb_row_ref[...]
    z_row = grow_ref[...] + gb_col_ref[...]
    ti = lax.broadcasted_iota(jnp.int32, (chunk, chunk), 0)
    si = lax.broadcasted_iota(jnp.int32, (chunk, chunk), 1)
    causal = ti >= si
    b_col = _hdot(causal.astype(F32), _log_sigmoid(z_col))
    b_row = _hdot(_log_sigmoid(z_row), (si >= ti).astype(F32))

    for h in range(NH_B):
        hs = slice(h * DK_B, (h + 1) * DK_B)
        fh = NH_B + h
        a_row = z_row[h:h + 1, :] - b_row[fh:fh + 1, :]
        a_col = z_col[:, h:h + 1] - b_col[:, fh:fh + 1]
        bh = b_col[:, fh:fh + 1]
        m_prev = m_ref[:, h:h + 1]
        a_mat = jnp.where(causal, a_row, NEG)
        gmax = jnp.maximum(m_prev, jnp.max(a_mat, axis=1, keepdims=True))
        dw = jnp.exp(a_mat - gmax)
        iw = jnp.exp(m_prev - gmax)
        qh = qf[:, hs]
        kh = kf[:, hs]
        qb = qh.astype(BF16)
        vh = vo_ref[:, hs]
        qk = _dot_nt(qb, kh.astype(BF16)) * dw
        c_old = c_ref[h]
        n_old = n_ref[h:h + 1, :]
        num = iw * _dot(qb, c_old.astype(BF16)) + _dot(qk.astype(BF16), vh)
        den = iw * jnp.sum(qh * n_old, axis=1, keepdims=True) + jnp.sum(qk, axis=1, keepdims=True)
        hid = num / jnp.maximum(jnp.abs(den), jnp.exp(-(bh + gmax)))
        o_gate = _sigmoid(vo_ref[:, B_WIDTH + h * DK_B:B_WIDTH + (h + 1) * DK_B].astype(F32))
        bout_ref[:, hs] = (o_gate * hid).astype(bout_ref.dtype)

        g_last = gmax[chunk - 1:chunk, :]
        decay = jnp.exp(m_prev - g_last)
        kw = kh * jnp.exp(a_col - g_last)
        c_ref[h] = decay * c_old + _dot_tn(kw.astype(BF16), vh)
        n_ref[h:h + 1, :] = decay * n_old + jnp.sum(kw, axis=0, keepdims=True)
        m_ref[:, h:h + 1] = bh[chunk - 1:chunk, :] + g_last


def _mlstm_prompt(qk_pre, vo, gcol, grow, cw, cb, gb_row, gb_col, batch, seq):
    chunk = MLSTM_CHUNK
    nc = seq // chunk
    n = batch * seq
    row = lambda b, c: (b * nc + c, 0)
    return pl.pallas_call(
        _mlstm_kernel,
        grid=(batch, nc),
        in_specs=[
            pl.BlockSpec((chunk, COLS_QK), row),
            pl.BlockSpec((chunk, COLS_VO), row),
            pl.BlockSpec((chunk, LANES), row),
            pl.BlockSpec((SUBLANES, chunk), lambda b, c: (0, b * nc + c)),
            _resident(cw.shape), _resident(cb.shape), _resident(gb_row.shape),
            _resident(gb_col.shape),
        ],
        out_specs=[
            pl.BlockSpec((chunk, B_WIDTH), row),
            pl.BlockSpec((None, NH_B, DK_B, DK_B), lambda b, c: (b, 0, 0, 0)),
            pl.BlockSpec((None, NH_B, DK_B), lambda b, c: (b, 0, 0)),
            pl.BlockSpec((None, 1, LANES), lambda b, c: (b, 0, 0)),
        ],
        out_shape=[
            jax.ShapeDtypeStruct((n, B_WIDTH), BF16),
            jax.ShapeDtypeStruct((batch, NH_B, DK_B, DK_B), F32),
            jax.ShapeDtypeStruct((batch, NH_B, DK_B), F32),
            jax.ShapeDtypeStruct((batch, 1, LANES), F32),
        ],
        scratch_shapes=[pltpu.VMEM((chunk + SUBLANES, COLS_QK), F32)],
        compiler_params=_params(2),
        name="mlstm_prompt",
    )(qk_pre, vo, gcol, grow, cw, cb, gb_row, gb_col)


def _smlstm_kernel(qk_ref, vo_ref, g_ref, gb_ref, cs_ref, cw_ref, cb_ref, c0_ref, n0_ref, m0_ref,
                   bout_ref, c1_ref, n1_ref, m1_ref, ncs_ref):
    u = qk_ref[...]
    y = cb_ref[...] + u * cw_ref[CONV_B - 1:CONV_B, :]
    for i in range(CONV_B - 1):
        y = y + cs_ref[i:i + 1, :] * cw_ref[i:i + 1, :]
    ncs_ref[0:CONV_B - 2, :] = cs_ref[1:CONV_B - 1, :]
    ncs_ref[CONV_B - 2:CONV_B - 1, :] = u
    y = y * _sigmoid(y)
    qf = y[:, :B_WIDTH]
    kf = y[:, B_WIDTH:] * (DK_B ** -0.5)
    z = g_ref[...] + gb_ref[...]
    lf_all = _log_sigmoid(z)
    m0 = m0_ref[...]
    eye = (lax.broadcasted_iota(jnp.int32, (DK_B, DK_B), 0)
           == lax.broadcasted_iota(jnp.int32, (DK_B, DK_B), 1))
    for h in range(NH_B):
        hs = slice(h * DK_B, (h + 1) * DK_B)
        ig = z[:, h:h + 1]
        inter = lf_all[:, NH_B + h:NH_B + h + 1] + m0[:, h:h + 1]
        m_t = jnp.maximum(inter, ig)
        dw = jnp.exp(ig - m_t)
        iw = jnp.exp(inter - m_t)
        qh = qf[:, hs]
        kh = kf[:, hs]
        vh = vo_ref[:, hs]
        c_old = c0_ref[h]
        n_old = n0_ref[h:h + 1, :]
        qk = jnp.sum(qh * kh, axis=1, keepdims=True) * dw
        num = iw * _hdot(qh, c_old) + qk * vh
        den = iw * jnp.sum(qh * n_old, axis=1, keepdims=True) + qk
        hid = num / jnp.maximum(jnp.abs(den), jnp.exp(-m_t))
        o_gate = _sigmoid(vo_ref[:, B_WIDTH + h * DK_B:B_WIDTH + (h + 1) * DK_B])
        bout_ref[:, hs] = o_gate * hid
        k_col = jnp.sum(jnp.where(eye, kh, 0.0), axis=1, keepdims=True)
        c1_ref[h] = iw * c_old + (dw * k_col) * vh
        n1_ref[h:h + 1, :] = iw * n_old + dw * kh
        m1_ref[:, h:h + 1] = m_t


def _mlstm_sample(qk_pre, vo, gcol, gb_row, conv_state, cw, cb, c0, n0, m0, layer):
    nreq = qk_pre.shape[0]
    one = lambda b: (b, 0, 0)
    lay3 = lambda b, layer=layer: (layer, b, 0, 0)
    lay4 = lambda b, layer=layer: (layer, b, 0, 0, 0)
    return pl.pallas_call(
        _smlstm_kernel,
        grid=(nreq,),
        in_specs=[
            pl.BlockSpec((None, 1, COLS_QK), one),
            pl.BlockSpec((None, 1, COLS_VO), one),
            pl.BlockSpec((None, 1, LANES), one),
            _resident(gb_row.shape),
            pl.BlockSpec((None, None, CONV_B - 1, COLS_QK), lay3),
            _resident(cw.shape), _resident(cb.shape),
            pl.BlockSpec((None, None, NH_B, DK_B, DK_B), lay4),
            pl.BlockSpec((None, None, NH_B, DK_B), lay3),
            pl.BlockSpec((None, None, 1, LANES), lay3),
        ],
        out_specs=[
            pl.BlockSpec((None, 1, B_WIDTH), one),
            pl.BlockSpec((None, NH_B, DK_B, DK_B), lambda b: (b, 0, 0, 0)),
            pl.BlockSpec((None, NH_B, DK_B), one),
            pl.BlockSpec((None, 1, LANES), one),
            pl.BlockSpec((None, CONV_B - 1, COLS_QK), one),
        ],
        out_shape=[
            jax.ShapeDtypeStruct((nreq, 1, B_WIDTH), F32),
            jax.ShapeDtypeStruct((nreq, NH_B, DK_B, DK_B), F32),
            jax.ShapeDtypeStruct((nreq, NH_B, DK_B), F32),
            jax.ShapeDtypeStruct((nreq, 1, LANES), F32),
            jax.ShapeDtypeStruct((nreq, CONV_B - 1, COLS_QK), F32),
        ],
        compiler_params=_params(1),
        name="mlstm_sample",
    )(qk_pre.reshape(nreq, 1, COLS_QK), vo.reshape(nreq, 1, COLS_VO),
      gcol.reshape(nreq, 1, LANES), gb_row, conv_state, cw, cb, c0, n0, m0)


def _merge_kernel(o0_ref, o1_ref, o2_ref, l0_ref, l1_ref, l2_ref, bo_ref, gab_ref, x_ref,
                  wpa_ref, wpb_ref, wo_ref, g2_ref, e_ref, x2_ref, h2_ref, *scratch):
    tm = x_ref.shape[0]

    def natural(ref, buf):
        dil, rows, _ = ref.shape
        n_slab = buf.shape[0]
        for r in range(dil):
            val = ref[r].astype(F32)
            for k in range(n_slab):
                buf[k, pl.ds(r, rows, stride=dil), :] = val[:, k * LANES:(k + 1) * LANES]
        return jnp.concatenate([buf[k] for k in range(n_slab)], axis=1)

    if scratch:
        os_ = [o0_ref[...].astype(F32), natural(o1_ref, scratch[0]), natural(o2_ref, scratch[1])]
        lses = [l0_ref[...], natural(l1_ref, scratch[2]), natural(l2_ref, scratch[3])]
    else:
        os_ = [r[...].astype(F32) for r in (o0_ref, o1_ref, o2_ref)]
        lses = [r[...] for r in (l0_ref, l1_ref, l2_ref)]
    top = jnp.maximum(jnp.maximum(lses[0], lses[1]), lses[2])
    ws = [jnp.exp(l - top) for l in lses]
    inv = 1.0 / (ws[0] + ws[1] + ws[2])
    e = e_ref[...]
    a = None
    for w, o in zip(ws, os_):
        alpha = w * inv
        hi = alpha.astype(BF16)
        lo = (alpha - hi.astype(F32)).astype(BF16)
        term = (_dot(hi, e) + _dot(lo, e)) * o
        a = term if a is None else a + term
    pa = _dot(a.astype(BF16), wpa_ref[...])
    pb = _dot(bo_ref[...].astype(BF16), wpb_ref[...])
    merged = (_sigmoid(gab_ref[:, :D_MODEL].astype(F32)) * pa
              + _sigmoid(gab_ref[:, D_MODEL:].astype(F32)) * pb)
    x2 = x_ref[...] + _dot(merged.astype(BF16), wo_ref[...])
    x2_ref[...] = x2
    h2_ref[...] = _rms(x2, g2_ref[...]).astype(BF16)


def _merge(os_, lses, bout, gab, x, wpa, wpb, wo, g2, e_bf, *, tm, seq=None):
    n = x.shape[0]
    row = lambda i: (i, 0)

    def group_specs(cols):
        specs = [pl.BlockSpec((tm, cols), row)]
        for g in (1, 2):
            if seq is None:
                specs.append(pl.BlockSpec((tm, cols), row))
            else:
                d = DILATIONS[g]
                tpb = seq // tm
                specs.append(pl.BlockSpec((None, d, tm // d, cols),
                                          lambda i, tpb=tpb: (i // tpb, 0, i % tpb, 0)))
        return specs

    scratch = []
    if seq is not None:
        scratch = ([pltpu.VMEM((A_GROUP // LANES, tm, LANES), F32)] * 2
                   + [pltpu.VMEM((1, tm, LANES), F32)] * 2)
    return pl.pallas_call(
        _merge_kernel,
        grid=(n // tm,),
        scratch_shapes=scratch,
        in_specs=group_specs(A_GROUP) + group_specs(LANES) + [
            pl.BlockSpec((tm, B_WIDTH), row),
            pl.BlockSpec((tm, COLS_GAB), row),
            pl.BlockSpec((tm, D_MODEL), row),
            _resident(wpa.shape), _resident(wpb.shape), _resident(wo.shape),
            _resident(g2.shape), _resident(e_bf.shape),
        ],
        out_specs=[pl.BlockSpec((tm, D_MODEL), row), pl.BlockSpec((tm, D_MODEL), row)],
        out_shape=[jax.ShapeDtypeStruct((n, D_MODEL), F32), jax.ShapeDtypeStruct((n, D_MODEL), BF16)],
        compiler_params=_params(1),
        name="merge",
    )(*os_, *lses, bout, gab, x, wpa, wpb, wo, g2, e_bf)


def _ffn_kernel(*refs, tiles_per_batch, from_state, final_norm):
    h2_ref, x2_ref, wup_ref, wdn_ref, cw_ref, cb_ref = refs[:6]
    pos = 6
    if from_state:
        prev_refs = refs[pos:pos + CONV_F - 1]
        pos += CONV_F - 1
    if final_norm:
        fg_ref = refs[pos]
        pos += 1
    x3_ref, u_ref = refs[pos:pos + 2]
    tm = h2_ref.shape[0]
    pad = SUBLANES
    if not from_state:
        ext_s = refs[pos + 2]

        @pl.when(pl.program_id(0) % tiles_per_batch == 0)
        def _():
            ext_s[0:pad, :] = jnp.zeros((pad, ext_s.shape[1]), F32)

    h2 = h2_ref[...]
    acc = x2_ref[...]
    for c in range(D_FF // FF_CHUNK):
        ys = []
        for off in (c * FF_CHUNK, D_FF + c * FF_CHUNK):
            cs = slice(off, off + FF_CHUNK)
            u = _dot(h2, wup_ref[:, cs])
            y = cb_ref[:, cs] + u * cw_ref[CONV_F - 1:CONV_F, cs]
            if from_state:
                u_ref[:, cs] = u
                for i in range(CONV_F - 1):
                    y = y + prev_refs[i][:, cs] * cw_ref[i:i + 1, cs]
            else:
                ext_s[pad:pad + tm, cs] = u
                for i in range(CONV_F - 1):
                    lo = pad - (CONV_F - 1) + i
                    y = y + ext_s[lo:lo + tm, cs] * cw_ref[i:i + 1, cs]
            ys.append(y)
        act = _gelu_tanh(ys[0]) * ys[1]
        acc = acc + _dot(act.astype(BF16), wdn_ref[c * FF_CHUNK:(c + 1) * FF_CHUNK, :])
    if not from_state:
        tail = ext_s[tm:tm + pad, :]
        u_ref[...] = tail
        ext_s[0:pad, :] = tail
    if final_norm:
        acc = _rms(acc, fg_ref[...])
    x3_ref[...] = acc


def _ffn(h2, x2, wup, wdn, cw, cb, *, tm, tiles_per_batch=None, prev_rows=None, final_g=None):
    n = h2.shape[0]
    from_state = prev_rows is not None
    row = lambda i: (i, 0)
    in_specs = [
        pl.BlockSpec((tm, D_MODEL), row), pl.BlockSpec((tm, D_MODEL), row),
        _resident(wup.shape), _resident(wdn.shape), _resident(cw.shape), _resident(cb.shape),
    ]
    args = [h2, x2, wup, wdn, cw, cb]
    scratch = []
    if from_state:
        in_specs += [pl.BlockSpec((tm, 2 * D_FF), row)] * (CONV_F - 1)
        args += list(prev_rows)
        u_shape = jax.ShapeDtypeStruct((n, 2 * D_FF), F32)
        u_spec = pl.BlockSpec((tm, 2 * D_FF), row)
    else:
        n_batch = n // (tm * tiles_per_batch)
        u_shape = jax.ShapeDtypeStruct((n_batch, SUBLANES, 2 * D_FF), F32)
        u_spec = pl.BlockSpec((None, SUBLANES, 2 * D_FF), lambda i: (i // tiles_per_batch, 0, 0))
        scratch = [pltpu.VMEM((tm + SUBLANES, 2 * D_FF), F32)]
    if final_g is not None:
        in_specs.append(_resident(final_g.shape))
        args.append(final_g)
    return pl.pallas_call(
        functools.partial(_ffn_kernel, tiles_per_batch=tiles_per_batch, from_state=from_state,
                          final_norm=final_g is not None),
        grid=(n // tm,),
        in_specs=in_specs,
        out_specs=[pl.BlockSpec((tm, D_MODEL), row), u_spec],
        out_shape=[jax.ShapeDtypeStruct((n, D_MODEL), F32), u_shape],
        scratch_shapes=scratch,
        compiler_params=_params(1),
        name="ffn",
    )(*args)


def _t5_bucket(dist):
    max_exact = NUM_BUCKETS // 2
    df = jnp.maximum(dist, 1).astype(F32)
    large = max_exact + (jnp.log(df / max_exact) / math.log(MAX_DISTANCE / max_exact)
                         * (NUM_BUCKETS - max_exact)).astype(jnp.int32)
    large = jnp.minimum(large, NUM_BUCKETS - 1)
    return jnp.where(dist < max_exact, dist, large)


def _bias_table(rel_bias, g, dist):
    bucket = _t5_bucket(jnp.asarray(dist, jnp.int32))
    table = rel_bias[:, g * H_G:(g + 1) * H_G].reshape((NUM_BUCKETS, H_G) + (1,) * bucket.ndim)
    ids = jnp.arange(NUM_BUCKETS).reshape((NUM_BUCKETS, 1) + (1,) * bucket.ndim)
    return jnp.sum(jnp.where(bucket[None, None] == ids, table, 0.0), axis=0)


def _prompt_bias(rel_bias, g):
    qi = np.arange(Q_BLOCK)[:, None]
    ki = np.arange(2 * Q_BLOCK)[None, :]
    rel = qi + Q_BLOCK - ki
    band = (rel >= 0) & (rel <= SPAN)
    bias = _bias_table(rel_bias, g, np.maximum(rel, 0) * DILATIONS[g])
    return jnp.where(band[None], bias, NEG)


def _sample_bias(rel_bias):
    rows, news = [], []
    for g in range(N_GROUPS):
        bias = _bias_table(rel_bias, g, np.arange(SPAN + 1) * DILATIONS[g])
        rows.append(jnp.broadcast_to(bias[:, :0:-1].T[:, :, None], (SPAN, H_G, DH_A)))
        news.append(jnp.broadcast_to(bias[:, 0:1], (H_G, DH_A)))
    return jnp.stack(rows), jnp.stack(news)


def _head_indicator():
    e = np.zeros((LANES, A_GROUP), np.float32)
    for h in range(H_G):
        e[h, h * DH_A:(h + 1) * DH_A] = 1.0
    return e


def kernel(x_prompt, x_sample, cache_kv_w128, cache_kv_w512, cache_kv_w2048, state_mlstm_conv,
           state_mlstm_C, state_mlstm_n, state_mlstm_m, state_ffn_conv, rel_bias, norm1_g, w_in,
           mconv_w, mconv_b, mgate_b, w_pa, w_pb, w_o, norm2_g, w_up, fconv_w, fconv_b, w_down,
           final_norm_g):
    batch, seq, _ = x_prompt.shape
    nreq = x_sample.shape[0]
    depth = w_in.shape[0]
    n_p = batch * seq
    caches = (cache_kv_w128, cache_kv_w512, cache_kv_w2048)

    assert seq % (Q_BLOCK * DILATIONS[-1]) == 0 and seq >= WINDOWS[-1]
    e_bf = jnp.asarray(_head_indicator(), BF16)
    prompt_bias = [_prompt_bias(rel_bias, g) for g in range(N_GROUPS)]
    sbias_m, sbias_0 = _sample_bias(rel_bias)
    m0_all = jnp.pad(state_mlstm_m, ((0, 0), (0, 0), (0, LANES - NH_B)))[:, :, None, :]
    fg = final_norm_g.reshape(1, D_MODEL)

    xp = x_prompt.reshape(n_p, D_MODEL)
    xs = x_sample.reshape(nreq, D_MODEL)
    p_st = [[] for _ in range(8)]
    s_st = [[] for _ in range(8)]

    for l in range(depth):
        last = l == depth - 1
        w_l = w_in[l]
        w_qkv = w_l[:, :COLS_QKV].reshape(D_MODEL, 3, N_GROUPS, A_GROUP).transpose(0, 2, 1, 3)
        w_main = jnp.concatenate([w_qkv.reshape(D_MODEL, COLS_QKV), w_l[:, COLS_QKV:GATE_COL0],
                                  w_l[:, GATE_COL0 + 2 * NH_B:]], axis=1).astype(BF16)
        w_gate = jnp.pad(w_l[:, GATE_COL0:GATE_COL0 + 2 * NH_B], ((0, 0), (0, LANES - 2 * NH_B))).astype(BF16)
        g1 = norm1_g[l].reshape(1, D_MODEL)
        g2 = norm2_g[l].reshape(1, D_MODEL)
        gate_bias = mgate_b[l].reshape(1, 2 * NH_B)
        gb_row = jnp.pad(gate_bias, ((0, 0), (0, LANES - 2 * NH_B)))
        gb_col = gate_bias.reshape(2 * NH_B, 1)
        mcw, mcb = mconv_w[l], mconv_b[l].reshape(1, COLS_QK)
        fcw, fcb = fconv_w[l], fconv_b[l].reshape(1, 2 * D_FF)
        wpa, wpb, wo = w_pa[l].astype(BF16), w_pb[l].astype(BF16), w_o[l].astype(BF16)
        wup, wdn = w_up[l].astype(BF16), w_down[l].astype(BF16)

        *qkvs, qk_pre, vo, gab, gcol, grow = _inproj(xp, g1, w_main, w_gate, tm=256, act_dtype=BF16,
                                                     seq=seq)
        os_, lses = [], []
        for g in range(N_GROUPS):
            d = DILATIONS[g]
            o_g, lse_g = _attn_prompt(qkvs[g].reshape(batch * d, seq // d, COLS_QKV_G), prompt_bias[g])
            shape = (n_p,) if g == 0 else (batch, d, seq // d)
            os_.append(o_g.reshape(shape + (A_GROUP,)))
            lses.append(lse_g.reshape(shape + (LANES,)))
        bout, c_p, n_p_state, m_p = _mlstm_prompt(qk_pre, vo, gcol, grow, mcw, mcb, gb_row, gb_col,
                                                  batch, seq)
        x2, h2 = _merge(os_, lses, bout, gab, xp, wpa, wpb, wo, g2, e_bf, tm=512, seq=seq)
        xp, u_tail = _ffn(h2, x2, wup, wdn, fcw, fcb, tm=256, tiles_per_batch=seq // 256,
                          final_g=fg if last else None)

        for g in range(N_GROUPS):
            d = DILATIONS[g]
            tail = qkvs[g].reshape(batch, d, seq // d, COLS_QKV_G)[:, :, seq // d - SPAN:, A_GROUP:]
            tail = tail.transpose(0, 2, 1, 3)
            p_st[g].append(tail.reshape(batch, SPAN * d, 2, H_G, DH_A).astype(F32))
        p_st[3].append(qk_pre.reshape(batch, seq, COLS_QK)[:, seq - (CONV_B - 1):])
        p_st[4].append(c_p)
        p_st[5].append(n_p_state)
        p_st[6].append(m_p[:, 0, :NH_B])
        p_st[7].append(u_tail[:, SUBLANES - (CONV_F - 1):])

        *qkvs_s, qk_s, vo_s, gab_s, gcol_s = _inproj(xs, g1, w_main, w_gate, tm=nreq, act_dtype=F32)
        new_qkv = jnp.stack(qkvs_s, axis=1).reshape(nreq, N_GROUPS, 3, H_G, DH_A)
        o_s, lse_s = _attn_sample(new_qkv, caches, l, sbias_m, sbias_0)
        bout_s, c_s, n_s, m_s, conv_s = _mlstm_sample(qk_s, vo_s, gcol_s, gb_row, state_mlstm_conv,
                                                      mcw, mcb, state_mlstm_C, state_mlstm_n, m0_all, l)
        lse_pad = jnp.pad(lse_s[..., 0], ((0, 0), (0, 0), (0, LANES - H_G)))
        x2_s, h2_s = _merge([o_s[:, g].reshape(nreq, A_GROUP) for g in range(N_GROUPS)],
                            [lse_pad[:, g] for g in range(N_GROUPS)],
                            bout_s.reshape(nreq, B_WIDTH), gab_s, xs, wpa, wpb, wo, g2, e_bf, tm=nreq)
        fbuf = state_ffn_conv[l]
        xs, u_s = _ffn(h2_s, x2_s, wup, wdn, fcw, fcb, tm=nreq,
                       prev_rows=[fbuf[:, i] for i in range(CONV_F - 1)],
                       final_g=fg if last else None)

        for g in range(N_GROUPS):
            s_st[g].append(new_qkv[:, g, 1:][:, None])
        s_st[3].append(conv_s)
        s_st[4].append(c_s)
        s_st[5].append(n_s)
        s_st[6].append(m_s[:, 0, :NH_B])
        s_st[7].append(jnp.concatenate([fbuf[:, 1:], u_s[:, None, :]], axis=1))

    outs = [xp.reshape(batch, seq, D_MODEL), xs.reshape(nreq, 1, D_MODEL)]
    for i in range(8):
        outs.append(jnp.stack(p_st[i], 0))
        outs.append(jnp.stack(s_st[i], 0))
    return tuple(outs)
```

```python
import functools
import math

import numpy as np
import jax
import jax.numpy as jnp
from jax import lax
from jax.experimental import pallas as pl
from jax.experimental.pallas import tpu as pltpu

F32 = jnp.float32
BF16 = jnp.bfloat16
HIGHEST = lax.Precision.HIGHEST

D_MODEL = 1024
WINDOWS = (128, 512, 2048)
DILATIONS = (1, 4, 16)
N_GROUPS = 3
H_G = 8
DH_A = 64
A_GROUP = H_G * DH_A
A_QKV = N_GROUPS * A_GROUP
Q_BLOCK = 128
SPAN = 128
NH_B = 4
DK_B = 256
B_WIDTH = NH_B * DK_B
CONV_B = 4
D_FF = 2816
CONV_F = 3
NUM_BUCKETS = 32
MAX_DISTANCE = 2048
RMS_EPS = 1e-6
NEG = -1e30

LANES = 128
SUBLANES = 8
FF_CHUNK = 256
MLSTM_CHUNK = 256
VMEM_LIMIT = 56 * 1024 * 1024

COLS_QKV_G = 3 * A_GROUP
COLS_QKV = N_GROUPS * COLS_QKV_G
COLS_QK = 2 * B_WIDTH
COLS_VO = 2 * B_WIDTH
COLS_GAB = 2 * D_MODEL
GATE_COL0 = COLS_QKV + COLS_QK + COLS_VO


def _dot(a, b):
    return jnp.dot(a, b, preferred_element_type=F32)


def _hdot(a, b):
    return jnp.dot(a, b, precision=HIGHEST, preferred_element_type=F32)


def _dot_nt(a, b):
    return lax.dot_general(a, b, (((1,), (1,)), ((), ())), preferred_element_type=F32)


def _dot_tn(a, b):
    return lax.dot_general(a, b, (((0,), (0,)), ((), ())), preferred_element_type=F32)


def _sigmoid(x):
    return 1.0 / (1.0 + jnp.exp(-x))


def _log_sigmoid(x):
    return jnp.minimum(x, 0.0) - jnp.log1p(jnp.exp(-jnp.abs(x)))


def _gelu_tanh(x):
    return 0.5 * x * (1.0 + jnp.tanh(math.sqrt(2.0 / math.pi) * (x + 0.044715 * (x * x * x))))


def _rms(x, g):
    return x * lax.rsqrt(jnp.mean(x * x, axis=-1, keepdims=True) + RMS_EPS) * g


def _resident(shape):
    nd = len(shape)
    return pl.BlockSpec(shape, lambda *_: (0,) * nd, pipeline_mode=pl.Buffered(1))


def _params(n_grid):
    return pltpu.CompilerParams(dimension_semantics=("arbitrary",) * n_grid,
                                vmem_limit_bytes=VMEM_LIMIT)


def _inproj_kernel(*refs, prompt, tiles_per_batch):
    if prompt:
        (x_ref, g_ref, w_ref, wg_ref, cw_ref, cb_ref, qkv0_ref, qkv1_ref, qkv2_ref, qk_ref, vo_ref,
         gab_ref, gcol_ref, grow_ref, ctail_ref, hs_ref, ext_ref) = refs
    else:
        (x_ref, g_ref, w_ref, wg_ref, qkv0_ref, qkv1_ref, qkv2_ref, qk_ref, vo_ref, gab_ref,
         gcol_ref) = refs
    hf = _rms(x_ref[...], g_ref[...])
    h = hf.astype(BF16)
    tm = hf.shape[0]

    def project(lhs, col0, width, store):
        for c in range(0, width, 512):
            store(c, _dot(lhs, w_ref[:, col0 + c:col0 + c + 512]))

    def store_rows(ref):
        def store(c, res):
            ref[:, c:c + 512] = res.astype(ref.dtype)
        return store

    project(h, 0, COLS_QKV_G, store_rows(qkv0_ref))
    if prompt:
        n_slab = hs_ref.shape[0]
        for k in range(n_slab):
            hs_ref[k] = hf[:, k * LANES:(k + 1) * LANES]
        for g, ref in ((1, qkv1_ref), (2, qkv2_ref)):
            d = DILATIONS[g]
            rows = tm // d
            hp = jnp.concatenate(
                [jnp.concatenate([hs_ref[k, pl.ds(r, rows, stride=d), :] for r in range(d)], axis=0)
                 for k in range(n_slab)], axis=1).astype(BF16)

            def store(c, res, ref=ref, d=d, rows=rows):
                res = res.astype(ref.dtype)
                for r in range(d):
                    ref[r, :, c:c + 512] = res[r * rows:(r + 1) * rows]
            project(hp, g * COLS_QKV_G, COLS_QKV_G, store)
    else:
        project(h, COLS_QKV_G, COLS_QKV_G, store_rows(qkv1_ref))
        project(h, 2 * COLS_QKV_G, COLS_QKV_G, store_rows(qkv2_ref))
    if prompt:
        pad = SUBLANES

        @pl.when(pl.program_id(0) % tiles_per_batch == 0)
        def _():
            ext_ref[:, 0:pad, :] = jnp.zeros((ext_ref.shape[0], pad, LANES), F32)

        def store_slabs(c, res):
            for k in range(512 // LANES):
                ext_ref[c // LANES + k, pad:pad + tm, :] = res[:, k * LANES:(k + 1) * LANES]
        project(h, COLS_QKV, COLS_QK, store_slabs)
        for k in range(COLS_QK // LANES):
            ks = slice(k * LANES, (k + 1) * LANES)
            y = cb_ref[:, ks] + ext_ref[k, pad:pad + tm, :] * cw_ref[CONV_B - 1:CONV_B, ks]
            for i in range(CONV_B - 1):
                lo = pad - (CONV_B - 1) + i
                y = y + ext_ref[k, lo:lo + tm, :] * cw_ref[i:i + 1, ks]
            y = y * _sigmoid(y)
            if k * LANES >= B_WIDTH:
                y = y * (DK_B ** -0.5)
            qk_ref[:, ks] = y.astype(qk_ref.dtype)
            tail = ext_ref[k, tm:tm + pad, :]
            ctail_ref[:, ks] = tail
            ext_ref[k, 0:pad, :] = tail
    else:
        project(h, COLS_QKV, COLS_QK, store_rows(qk_ref))
    col = COLS_QKV + COLS_QK
    for ref in (vo_ref, gab_ref):
        project(h, col, ref.shape[-1], store_rows(ref))
        col += ref.shape[-1]
    gates = _dot(h, wg_ref[...])
    gcol_ref[...] = gates
    if prompt:
        grow_ref[...] = gates.T[:SUBLANES, :]


def _inproj(x, gain, w_main, w_gate, *, tm, act_dtype, seq=None, conv=None):
    n = x.shape[0]
    dilate = seq is not None
    row = lambda i: (i, 0)
    out_shape = [jax.ShapeDtypeStruct((n, COLS_QKV_G), act_dtype)]
    out_specs = [pl.BlockSpec((tm, COLS_QKV_G), row)]
    for g in (1, 2):
        if dilate:
            d = DILATIONS[g]
            tpb = seq // tm
            out_shape.append(jax.ShapeDtypeStruct((n // seq, d, seq // d, COLS_QKV_G), act_dtype))
            out_specs.append(pl.BlockSpec((None, d, tm // d, COLS_QKV_G),
                                          lambda i, tpb=tpb: (i // tpb, 0, i % tpb, 0)))
        else:
            out_shape.append(jax.ShapeDtypeStruct((n, COLS_QKV_G), act_dtype))
            out_specs.append(pl.BlockSpec((tm, COLS_QKV_G), row))
    for cols, dt in ((COLS_QK, act_dtype if dilate else F32), (COLS_VO, act_dtype),
                     (COLS_GAB, act_dtype), (LANES, F32)):
        out_shape.append(jax.ShapeDtypeStruct((n, cols), dt))
        out_specs.append(pl.BlockSpec((tm, cols), row))
    in_specs = [
        pl.BlockSpec((tm, D_MODEL), row),
        _resident((1, D_MODEL)),
        _resident(w_main.shape),
        _resident(w_gate.shape),
    ]
    args = [x, gain, w_main, w_gate]
    scratch = []
    tpb = None
    if dilate:
        tpb = seq // tm
        in_specs += [_resident(conv[0].shape), _resident(conv[1].shape)]
        args += list(conv)
        out_shape.append(jax.ShapeDtypeStruct((SUBLANES, n), F32))
        out_specs.append(pl.BlockSpec((SUBLANES, tm), lambda i: (0, i)))
        out_shape.append(jax.ShapeDtypeStruct((n // seq, SUBLANES, COLS_QK), F32))
        out_specs.append(pl.BlockSpec((None, SUBLANES, COLS_QK), lambda i: (i // tpb, 0, 0)))
        scratch = [pltpu.VMEM((D_MODEL // LANES, tm, LANES), F32),
                   pltpu.VMEM((COLS_QK // LANES, tm + SUBLANES, LANES), F32)]
    return pl.pallas_call(
        functools.partial(_inproj_kernel, prompt=dilate, tiles_per_batch=tpb),
        grid=(n // tm,),
        in_specs=in_specs,
        out_specs=out_specs,
        out_shape=out_shape,
        scratch_shapes=scratch,
        compiler_params=_params(1),
        name="inproj",
    )(*args)


def _attn_kernel(q_ref, kp_ref, kc_ref, vp_ref, vc_ref, bias_ref, o_ref, lse_ref):
    n_blk = q_ref.shape[0] // Q_BLOCK
    has_prev = pl.program_id(1) > 0
    key_lane = lax.broadcasted_iota(jnp.int32, (1, 1, 2 * Q_BLOCK), 2)
    first_mask = jnp.where((key_lane < Q_BLOCK) & jnp.logical_not(has_prev), NEG, 0.0)
    lane = lax.broadcasted_iota(jnp.int32, (Q_BLOCK, LANES), 1)
    low_half = lane < DH_A
    pair = 2 * DH_A
    for i in range(n_blk):
        rows = slice(i * Q_BLOCK, (i + 1) * Q_BLOCK)
        if i == 0:
            kprev_ref, vprev_ref, prows = kp_ref, vp_ref, slice(0, Q_BLOCK)
        else:
            kprev_ref, vprev_ref, prows = kc_ref, vc_ref, slice((i - 1) * Q_BLOCK, i * Q_BLOCK)
        scores = []
        for hp in range(H_G // 2):
            cols = slice(hp * pair, (hp + 1) * pair)
            qp = q_ref[rows, cols]
            kk = jnp.concatenate([kprev_ref[prows, cols], kc_ref[rows, cols]], axis=0)
            scores.append(_dot_nt(jnp.where(low_half, qp, jnp.zeros_like(qp)), kk))
            scores.append(_dot_nt(jnp.where(low_half, jnp.zeros_like(qp), qp), kk))
        s = jnp.stack(scores) * (DH_A ** -0.5) + bias_ref[...]
        if i == 0:
            s = s + first_mask
        m = jnp.max(s, axis=-1, keepdims=True)
        p = jnp.exp(s - m)
        l = jnp.sum(p, axis=-1, keepdims=True)
        pb = p.astype(BF16)
        inv = 1.0 / l
        lse = m + jnp.log(l)
        lse_all = jnp.zeros((Q_BLOCK, LANES), F32)
        for hp in range(H_G // 2):
            cols = slice(hp * pair, (hp + 1) * pair)
            vv = jnp.concatenate([vprev_ref[prows, cols], vc_ref[rows, cols]], axis=0)
            o_lo = _dot(pb[2 * hp], vv) * inv[2 * hp]
            o_hi = _dot(pb[2 * hp + 1], vv) * inv[2 * hp + 1]
            o_ref[rows, cols] = jnp.where(low_half, o_lo, o_hi).astype(o_ref.dtype)
        for h in range(H_G):
            lse_all = jnp.where(lane == h, lse[h], lse_all)
        lse_ref[rows, :] = lse_all


ATTN_BLOCKS = 4


def _attn_prompt(qkv, bias):
    nsub, u_len, _ = qkv.shape
    n_blk = math.gcd(ATTN_BLOCKS, u_len // Q_BLOCK)
    rows = n_blk * Q_BLOCK
    nb = u_len // rows

    def spec(col_block, prev):
        if prev:
            return pl.BlockSpec((None, Q_BLOCK, A_GROUP),
                                lambda s, j: (s, jnp.maximum(j * n_blk - 1, 0), col_block))
        return pl.BlockSpec((None, rows, A_GROUP), lambda s, j: (s, j, col_block))

    return pl.pallas_call(
        _attn_kernel,
        grid=(nsub, nb),
        in_specs=[spec(0, False), spec(1, True), spec(1, False), spec(2, True), spec(2, False),
                  _resident(bias.shape)],
        out_specs=[
            pl.BlockSpec((None, rows, A_GROUP), lambda s, j: (s, j, 0)),
            pl.BlockSpec((None, rows, LANES), lambda s, j: (s, j, 0)),
        ],
        out_shape=[
            jax.ShapeDtypeStruct((nsub, u_len, A_GROUP), BF16),
            jax.ShapeDtypeStruct((nsub, u_len, LANES), F32),
        ],
        compiler_params=_params(2),
        name="attn_prompt",
    )(qkv, qkv, qkv, qkv, qkv, bias)


def _sattn_kernel(new_ref, kv0_ref, kv1_ref, kv2_ref, bt0_ref, bt1_ref, bt2_ref, b0_ref,
                  o_ref, lse_ref):
    eye = (lax.broadcasted_iota(jnp.int32, (DH_A, DH_A), 0)
           == lax.broadcasted_iota(jnp.int32, (DH_A, DH_A), 1))
    for g, (kv_ref, bt_ref) in enumerate(((kv0_ref, bt0_ref), (kv1_ref, bt1_ref),
                                          (kv2_ref, bt2_ref))):
        q = new_ref[g, 0]
        k_new = new_ref[g, 1]
        v_new = new_ref[g, 2]
        s_rows = []
        for h in range(H_G):
            q_col = jnp.sum(jnp.where(eye, q[h:h + 1, :], 0.0), axis=1, keepdims=True)
            s_rows.append(jnp.sum(kv_ref[0, h] * q_col, axis=0, keepdims=True))
        s = jnp.concatenate(s_rows, axis=0) * (DH_A ** -0.5) + bt_ref[...]
        s0 = jnp.sum(k_new * q, axis=1, keepdims=True) * (DH_A ** -0.5) + b0_ref[g]
        m = jnp.maximum(jnp.max(s, axis=1, keepdims=True), s0)
        p = jnp.exp(s - m[:, 0:1])
        p0 = jnp.exp(s0 - m)
        l = jnp.sum(p, axis=1, keepdims=True) + p0
        o_rows = []
        for h in range(H_G):
            o_col = jnp.sum(kv_ref[1, h] * p[h:h + 1, :], axis=1, keepdims=True)
            o_rows.append(jnp.sum(jnp.where(eye, o_col, 0.0), axis=0, keepdims=True))
        o_ref[g] = (jnp.concatenate(o_rows, axis=0) + p0 * v_new) / l
        lse_ref[g] = m + jnp.log(l)


def _attn_sample(new_qkv, caches, layer, bias_t, bias_0):
    nreq = new_qkv.shape[0]
    views = []
    specs = []
    for g, cache in enumerate(caches):
        n_buf = cache.shape[2]
        assert n_buf == SPAN * DILATIONS[g], "cache must hold exactly one window"
        views.append(cache.transpose(0, 1, 3, 4, 5, 2))
        specs.append(pl.BlockSpec((None, None, 2, H_G, DH_A, n_buf),
                                  lambda b, layer=layer: (layer, b, 0, 0, 0, 0)))
    out = jax.ShapeDtypeStruct((nreq, N_GROUPS, H_G, DH_A), F32)
    out_spec = pl.BlockSpec((None, N_GROUPS, H_G, DH_A), lambda b: (b, 0, 0, 0))
    return pl.pallas_call(
        _sattn_kernel,
        grid=(nreq,),
        in_specs=[pl.BlockSpec((None, N_GROUPS, 3, H_G, DH_A), lambda b: (b, 0, 0, 0, 0))] + specs
        + [_resident(t.shape) for t in bias_t] + [_resident(bias_0.shape)],
        out_specs=[out_spec, out_spec],
        out_shape=[out, out],
        compiler_params=_params(1),
        name="attn_sample",
    )(new_qkv, *views, *bias_t, bias_0)


def _mlstm_kernel(qk_ref, vo_ref, gcol_ref, grow_ref, gb_row_ref, gb_col_ref,
                  bout_ref, c_ref, n_ref, m_ref):
    chunk = qk_ref.shape[0]

    @pl.when(pl.program_id(1) == 0)
    def _():
        c_ref[...] = jnp.zeros_like(c_ref)
        n_ref[...] = jnp.zeros_like(n_ref)
        m_ref[...] = jnp.zeros_like(m_ref)

    z_col = gcol_ref[...] + gb_row_ref[...]
    z_row = grow_ref[...] + gb_col_ref[...]
    ti = lax.broadcasted_iota(jnp.int32, (chunk, chunk), 0)
    si = lax.broadcasted_iota(jnp.int32, (chunk, chunk), 1)
    causal = ti >= si
    b_col = _hdot(causal.astype(F32), _log_sigmoid(z_col))
    b_row = _hdot(_log_sigmoid(z_row), (si >= ti).astype(F32))

    for h in range(NH_B):
        hs = slice(h * DK_B, (h + 1) * DK_B)
        fh = NH_B + h
        a_row = z_row[h:h + 1, :] - b_row[fh:fh + 1, :]
        a_col = z_col[:, h:h + 1] - b_col[:, fh:fh + 1]
        bh = b_col[:, fh:fh + 1]
        m_prev = m_ref[:, h:h + 1]
        a_mat = jnp.where(causal, a_row, NEG)
        gmax = jnp.maximum(m_prev, jnp.max(a_mat, axis=1, keepdims=True))
        dw = jnp.exp(a_mat - gmax)
        iw = jnp.exp(m_prev - gmax)
        qb = qk_ref[:, hs]
        kb = qk_ref[:, B_WIDTH + h * DK_B:B_WIDTH + (h + 1) * DK_B]
        kh = kb.astype(F32)
        vh = vo_ref[:, hs]
        qk = _dot_nt(qb, kb) * dw
        c_old = c_ref[h]
        n_old = n_ref[h:h + 1, :]
        num = iw * _dot(qb, c_old.astype(BF16)) + _dot(qk.astype(BF16), vh)
        den = (iw * jnp.sum(qb.astype(F32) * n_old, axis=1, keepdims=True)
               + jnp.sum(qk, axis=1, keepdims=True))
        hid = num / jnp.maximum(jnp.abs(den), jnp.exp(-(bh + gmax)))
        o_gate = _sigmoid(vo_ref[:, B_WIDTH + h * DK_B:B_WIDTH + (h + 1) * DK_B].astype(F32))
        bout_ref[:, hs] = (o_gate * hid).astype(bout_ref.dtype)

        g_last = gmax[chunk - 1:chunk, :]
        decay = jnp.exp(m_prev - g_last)
        kw = kh * jnp.exp(a_col - g_last)
        c_ref[h] = decay * c_old + _dot_tn(kw.astype(BF16), vh)
        n_ref[h:h + 1, :] = decay * n_old + jnp.sum(kw, axis=0, keepdims=True)
        m_ref[:, h:h + 1] = bh[chunk - 1:chunk, :] + g_last


def _mlstm_prompt(qk, vo, gcol, grow, gb_row, gb_col, batch, seq):
    chunk = MLSTM_CHUNK
    nc = seq // chunk
    n = batch * seq
    row = lambda b, c: (b * nc + c, 0)
    return pl.pallas_call(
        _mlstm_kernel,
        grid=(batch, nc),
        in_specs=[
            pl.BlockSpec((chunk, COLS_QK), row),
            pl.BlockSpec((chunk, COLS_VO), row),
            pl.BlockSpec((chunk, LANES), row),
            pl.BlockSpec((SUBLANES, chunk), lambda b, c: (0, b * nc + c)),
            _resident(gb_row.shape), _resident(gb_col.shape),
        ],
        out_specs=[
            pl.BlockSpec((chunk, B_WIDTH), row),
            pl.BlockSpec((None, NH_B, DK_B, DK_B), lambda b, c: (b, 0, 0, 0)),
            pl.BlockSpec((None, NH_B, DK_B), lambda b, c: (b, 0, 0)),
            pl.BlockSpec((None, 1, LANES), lambda b, c: (b, 0, 0)),
        ],
        out_shape=[
            jax.ShapeDtypeStruct((n, B_WIDTH), BF16),
            jax.ShapeDtypeStruct((batch, NH_B, DK_B, DK_B), F32),
            jax.ShapeDtypeStruct((batch, NH_B, DK_B), F32),
            jax.ShapeDtypeStruct((batch, 1, LANES), F32),
        ],
        compiler_params=_params(2),
        name="mlstm_prompt",
    )(qk, vo, gcol, grow, gb_row, gb_col)


def _smlstm_kernel(qk_ref, vo_ref, g_ref, gb_ref, cs_ref, cw_ref, cb_ref, c0_ref, n0_ref, m0_ref,
                   bout_ref, c1_ref, n1_ref, m1_ref, ncs_ref):
    u = qk_ref[...]
    y = cb_ref[...] + u * cw_ref[CONV_B - 1:CONV_B, :]
    for i in range(CONV_B - 1):
        y = y + cs_ref[i:i + 1, :] * cw_ref[i:i + 1, :]
    ncs_ref[0:CONV_B - 2, :] = cs_ref[1:CONV_B - 1, :]
    ncs_ref[CONV_B - 2:CONV_B - 1, :] = u
    y = y * _sigmoid(y)
    qf = y[:, :B_WIDTH]
    kf = y[:, B_WIDTH:] * (DK_B ** -0.5)
    z = g_ref[...] + gb_ref[...]
    lf_all = _log_sigmoid(z)
    m0 = m0_ref[...]
    eye = (lax.broadcasted_iota(jnp.int32, (DK_B, DK_B), 0)
           == lax.broadcasted_iota(jnp.int32, (DK_B, DK_B), 1))
    for h in range(NH_B):
        hs = slice(h * DK_B, (h + 1) * DK_B)
        ig = z[:, h:h + 1]
        inter = lf_all[:, NH_B + h:NH_B + h + 1] + m0[:, h:h + 1]
        m_t = jnp.maximum(inter, ig)
        dw = jnp.exp(ig - m_t)
        iw = jnp.exp(inter - m_t)
        qh = qf[:, hs]
        kh = kf[:, hs]
        vh = vo_ref[:, hs]
        c_old = c0_ref[h]
        n_old = n0_ref[h:h + 1, :]
        qk = jnp.sum(qh * kh, axis=1, keepdims=True) * dw
        num = iw * _hdot(qh, c_old) + qk * vh
        den = iw * jnp.sum(qh * n_old, axis=1, keepdims=True) + qk
        hid = num / jnp.maximum(jnp.abs(den), jnp.exp(-m_t))
        o_gate = _sigmoid(vo_ref[:, B_WIDTH + h * DK_B:B_WIDTH + (h + 1) * DK_B])
        bout_ref[:, hs] = o_gate * hid
        k_col = jnp.sum(jnp.where(eye, kh, 0.0), axis=1, keepdims=True)
        c1_ref[h] = iw * c_old + (dw * k_col) * vh
        n1_ref[h:h + 1, :] = iw * n_old + dw * kh
        m1_ref[:, h:h + 1] = m_t


def _mlstm_sample(qk_pre, vo, gcol, gb_row, conv_state, cw, cb, c0, n0, m0, layer):
    nreq = qk_pre.shape[0]
    one = lambda b: (b, 0, 0)
    lay3 = lambda b, layer=layer: (layer, b, 0, 0)
    lay4 = lambda b, layer=layer: (layer, b, 0, 0, 0)
    return pl.pallas_call(
        _smlstm_kernel,
        grid=(nreq,),
        in_specs=[
            pl.BlockSpec((None, 1, COLS_QK), one),
            pl.BlockSpec((None, 1, COLS_VO), one),
            pl.BlockSpec((None, 1, LANES), one),
            _resident(gb_row.shape),
            pl.BlockSpec((None, None, CONV_B - 1, COLS_QK), lay3),
            _resident(cw.shape), _resident(cb.shape),
            pl.BlockSpec((None, None, NH_B, DK_B, DK_B), lay4),
            pl.BlockSpec((None, None, NH_B, DK_B), lay3),
            pl.BlockSpec((None, None, 1, LANES), lay3),
        ],
        out_specs=[
            pl.BlockSpec((None, 1, B_WIDTH), one),
            pl.BlockSpec((None, NH_B, DK_B, DK_B), lambda b: (b, 0, 0, 0)),
            pl.BlockSpec((None, NH_B, DK_B), one),
            pl.BlockSpec((None, 1, LANES), one),
            pl.BlockSpec((None, CONV_B - 1, COLS_QK), one),
        ],
        out_shape=[
            jax.ShapeDtypeStruct((nreq, 1, B_WIDTH), F32),
            jax.ShapeDtypeStruct((nreq, NH_B, DK_B, DK_B), F32),
            jax.ShapeDtypeStruct((nreq, NH_B, DK_B), F32),
            jax.ShapeDtypeStruct((nreq, 1, LANES), F32),
            jax.ShapeDtypeStruct((nreq, CONV_B - 1, COLS_QK), F32),
        ],
        compiler_params=_params(1),
        name="mlstm_sample",
    )(qk_pre.reshape(nreq, 1, COLS_QK), vo.reshape(nreq, 1, COLS_VO),
      gcol.reshape(nreq, 1, LANES), gb_row, conv_state, cw, cb, c0, n0, m0)


def _merge_kernel(o0_ref, o1_ref, o2_ref, l0_ref, l1_ref, l2_ref, bo_ref, gab_ref, x_ref,
                  wpa_ref, wpb_ref, wo_ref, g2_ref, e_ref, x2_ref, h2_ref, *scratch):
    tm = x_ref.shape[0]

    def natural(ref, buf):
        dil, rows, _ = ref.shape
        n_slab = buf.shape[0]
        for r in range(dil):
            val = ref[r].astype(F32)
            for k in range(n_slab):
                buf[k, pl.ds(r, rows, stride=dil), :] = val[:, k * LANES:(k + 1) * LANES]
        return jnp.concatenate([buf[k] for k in range(n_slab)], axis=1)

    if scratch:
        os_ = [o0_ref[...].astype(F32), natural(o1_ref, scratch[0]), natural(o2_ref, scratch[1])]
        lses = [l0_ref[...], natural(l1_ref, scratch[2]), natural(l2_ref, scratch[3])]
    else:
        os_ = [r[...].astype(F32) for r in (o0_ref, o1_ref, o2_ref)]
        lses = [r[...] for r in (l0_ref, l1_ref, l2_ref)]
    top = jnp.maximum(jnp.maximum(lses[0], lses[1]), lses[2])
    ws = [jnp.exp(l - top) for l in lses]
    inv = 1.0 / (ws[0] + ws[1] + ws[2])
    e = e_ref[...]
    a = None
    for w, o in zip(ws, os_):
        alpha = w * inv
        hi = alpha.astype(BF16)
        lo = (alpha - hi.astype(F32)).astype(BF16)
        term = (_dot(hi, e) + _dot(lo, e)) * o
        a = term if a is None else a + term
    pa = _dot(a.astype(BF16), wpa_ref[...])
    pb = _dot(bo_ref[...].astype(BF16), wpb_ref[...])
    merged = (_sigmoid(gab_ref[:, :D_MODEL].astype(F32)) * pa
              + _sigmoid(gab_ref[:, D_MODEL:].astype(F32)) * pb)
    x2 = x_ref[...] + _dot(merged.astype(BF16), wo_ref[...])
    x2_ref[...] = x2
    h2_ref[...] = _rms(x2, g2_ref[...]).astype(BF16)


def _merge(os_, lses, bout, gab, x, wpa, wpb, wo, g2, e_bf, *, tm, seq=None):
    n = x.shape[0]
    row = lambda i: (i, 0)

    def group_specs(cols):
        specs = [pl.BlockSpec((tm, cols), row)]
        for g in (1, 2):
            if seq is None:
                specs.append(pl.BlockSpec((tm, cols), row))
            else:
                d = DILATIONS[g]
                tpb = seq // tm
                specs.append(pl.BlockSpec((None, d, tm // d, cols),
                                          lambda i, tpb=tpb: (i // tpb, 0, i % tpb, 0)))
        return specs

    scratch = []
    if seq is not None:
        scratch = ([pltpu.VMEM((A_GROUP // LANES, tm, LANES), F32)] * 2
                   + [pltpu.VMEM((1, tm, LANES), F32)] * 2)
    return pl.pallas_call(
        _merge_kernel,
        grid=(n // tm,),
        scratch_shapes=scratch,
        in_specs=group_specs(A_GROUP) + group_specs(LANES) + [
            pl.BlockSpec((tm, B_WIDTH), row),
            pl.BlockSpec((tm, COLS_GAB), row),
            pl.BlockSpec((tm, D_MODEL), row),
            _resident(wpa.shape), _resident(wpb.shape), _resident(wo.shape),
            _resident(g2.shape), _resident(e_bf.shape),
        ],
        out_specs=[pl.BlockSpec((tm, D_MODEL), row), pl.BlockSpec((tm, D_MODEL), row)],
        out_shape=[jax.ShapeDtypeStruct((n, D_MODEL), F32), jax.ShapeDtypeStruct((n, D_MODEL), BF16)],
        compiler_params=_params(1),
        name="merge",
    )(*os_, *lses, bout, gab, x, wpa, wpb, wo, g2, e_bf)


def _ffn_kernel(*refs, tiles_per_batch, from_state, final_norm):
    h2_ref, x2_ref, wup_ref, wdn_ref, cw_ref, cb_ref = refs[:6]
    pos = 6
    if from_state:
        prev_refs = refs[pos:pos + CONV_F - 1]
        pos += CONV_F - 1
    if final_norm:
        fg_ref = refs[pos]
        pos += 1
    x3_ref, u_ref = refs[pos:pos + 2]
    tm = h2_ref.shape[0]
    pad = SUBLANES
    if not from_state:
        ext_s = refs[pos + 2]

        @pl.when(pl.program_id(0) % tiles_per_batch == 0)
        def _():
            ext_s[:, 0:pad, :] = jnp.zeros((ext_s.shape[0], pad, LANES), F32)

    h2 = h2_ref[...]
    acc = x2_ref[...]
    for c in range(D_FF // FF_CHUNK):
        ys = []
        for off in (c * FF_CHUNK, D_FF + c * FF_CHUNK):
            cs = slice(off, off + FF_CHUNK)
            u = _dot(h2, wup_ref[:, cs])
            y = cb_ref[:, cs] + u * cw_ref[CONV_F - 1:CONV_F, cs]
            if from_state:
                u_ref[:, cs] = u
                for i in range(CONV_F - 1):
                    y = y + prev_refs[i][:, cs] * cw_ref[i:i + 1, cs]
            else:
                taps = []
                for k in range(FF_CHUNK // LANES):
                    slab = off // LANES + k
                    ks = slice(off + k * LANES, off + (k + 1) * LANES)
                    ext_s[slab, pad:pad + tm, :] = u[:, k * LANES:(k + 1) * LANES]
                    tap = None
                    for i in range(CONV_F - 1):
                        lo = pad - (CONV_F - 1) + i
                        term = ext_s[slab, lo:lo + tm, :] * cw_ref[i:i + 1, ks]
                        tap = term if tap is None else tap + term
                    taps.append(tap)
                    tail = ext_s[slab, tm:tm + pad, :]
                    u_ref[:, ks] = tail
                    ext_s[slab, 0:pad, :] = tail
                y = y + jnp.concatenate(taps, axis=1)
            ys.append(y)
        act = _gelu_tanh(ys[0]) * ys[1]
        acc = acc + _dot(act.astype(BF16), wdn_ref[c * FF_CHUNK:(c + 1) * FF_CHUNK, :])
    if final_norm:
        acc = _rms(acc, fg_ref[...])
    x3_ref[...] = acc


def _ffn(h2, x2, wup, wdn, cw, cb, *, tm, tiles_per_batch=None, prev_rows=None, final_g=None):
    n = h2.shape[0]
    from_state = prev_rows is not None
    row = lambda i: (i, 0)
    in_specs = [
        pl.BlockSpec((tm, D_MODEL), row), pl.BlockSpec((tm, D_MODEL), row),
        _resident(wup.shape), _resident(wdn.shape), _resident(cw.shape), _resident(cb.shape),
    ]
    args = [h2, x2, wup, wdn, cw, cb]
    scratch = []
    if from_state:
        in_specs += [pl.BlockSpec((tm, 2 * D_FF), row)] * (CONV_F - 1)
        args += list(prev_rows)
        u_shape = jax.ShapeDtypeStruct((n, 2 * D_FF), F32)
        u_spec = pl.BlockSpec((tm, 2 * D_FF), row)
    else:
        n_batch = n // (tm * tiles_per_batch)
        u_shape = jax.ShapeDtypeStruct((n_batch, SUBLANES, 2 * D_FF), F32)
        u_spec = pl.BlockSpec((None, SUBLANES, 2 * D_FF), lambda i: (i // tiles_per_batch, 0, 0))
        scratch = [pltpu.VMEM((2 * D_FF // LANES, tm + SUBLANES, LANES), F32)]
    if final_g is not None:
        in_specs.append(_resident(final_g.shape))
        args.append(final_g)
    return pl.pallas_call(
        functools.partial(_ffn_kernel, tiles_per_batch=tiles_per_batch, from_state=from_state,
                          final_norm=final_g is not None),
        grid=(n // tm,),
        in_specs=in_specs,
        out_specs=[pl.BlockSpec((tm, D_MODEL), row), u_spec],
        out_shape=[jax.ShapeDtypeStruct((n, D_MODEL), F32), u_shape],
        scratch_shapes=scratch,
        compiler_params=_params(1),
        name="ffn",
    )(*args)


def _t5_bucket(dist):
    max_exact = NUM_BUCKETS // 2
    df = jnp.maximum(dist, 1).astype(F32)
    large = max_exact + (jnp.log(df / max_exact) / math.log(MAX_DISTANCE / max_exact)
                         * (NUM_BUCKETS - max_exact)).astype(jnp.int32)
    large = jnp.minimum(large, NUM_BUCKETS - 1)
    return jnp.where(dist < max_exact, dist, large)


def _bias_table(rel_bias, g, dist):
    bucket = _t5_bucket(jnp.asarray(dist, jnp.int32))
    table = rel_bias[:, g * H_G:(g + 1) * H_G].reshape((NUM_BUCKETS, H_G) + (1,) * bucket.ndim)
    ids = jnp.arange(NUM_BUCKETS).reshape((NUM_BUCKETS, 1) + (1,) * bucket.ndim)
    return jnp.sum(jnp.where(bucket[None, None] == ids, table, 0.0), axis=0)


def _prompt_bias(rel_bias, g):
    qi = np.arange(Q_BLOCK)[:, None]
    ki = np.arange(2 * Q_BLOCK)[None, :]
    rel = qi + Q_BLOCK - ki
    band = (rel >= 0) & (rel <= SPAN)
    bias = _bias_table(rel_bias, g, np.maximum(rel, 0) * DILATIONS[g])
    return jnp.where(band[None], bias, NEG)


def _sample_bias(rel_bias):
    tables, news = [], []
    for g in range(N_GROUPS):
        dil = DILATIONS[g]
        n_buf = SPAN * dil
        pos = np.arange(n_buf)
        bias = _bias_table(rel_bias, g, n_buf - pos)
        tables.append(jnp.where((pos % dil == 0)[None], bias, NEG))
        news.append(jnp.broadcast_to(_bias_table(rel_bias, g, np.zeros((1,), np.int32)), (H_G, DH_A)))
    return tables, jnp.stack(news)


def _head_indicator():
    e = np.zeros((LANES, A_GROUP), np.float32)
    for h in range(H_G):
        e[h, h * DH_A:(h + 1) * DH_A] = 1.0
    return e


def kernel(x_prompt, x_sample, cache_kv_w128, cache_kv_w512, cache_kv_w2048, state_mlstm_conv,
           state_mlstm_C, state_mlstm_n, state_mlstm_m, state_ffn_conv, rel_bias, norm1_g, w_in,
           mconv_w, mconv_b, mgate_b, w_pa, w_pb, w_o, norm2_g, w_up, fconv_w, fconv_b, w_down,
           final_norm_g):
    batch, seq, _ = x_prompt.shape
    nreq = x_sample.shape[0]
    depth = w_in.shape[0]
    n_p = batch * seq
    caches = (cache_kv_w128, cache_kv_w512, cache_kv_w2048)

    assert seq % (Q_BLOCK * DILATIONS[-1]) == 0 and seq >= WINDOWS[-1]
    e_bf = jnp.asarray(_head_indicator(), BF16)
    prompt_bias = [_prompt_bias(rel_bias, g) for g in range(N_GROUPS)]
    sbias_m, sbias_0 = _sample_bias(rel_bias)
    m0_all = jnp.pad(state_mlstm_m, ((0, 0), (0, 0), (0, LANES - NH_B)))[:, :, None, :]
    fg = final_norm_g.reshape(1, D_MODEL)

    xp = x_prompt.reshape(n_p, D_MODEL)
    xs = x_sample.reshape(nreq, D_MODEL)
    p_st = [[] for _ in range(8)]
    s_st = [[] for _ in range(8)]

    for l in range(depth):
        last = l == depth - 1
        w_l = w_in[l]
        w_qkv = w_l[:, :COLS_QKV].reshape(D_MODEL, 3, N_GROUPS, A_GROUP).transpose(0, 2, 1, 3)
        w_main = jnp.concatenate([w_qkv.reshape(D_MODEL, COLS_QKV), w_l[:, COLS_QKV:GATE_COL0],
                                  w_l[:, GATE_COL0 + 2 * NH_B:]], axis=1).astype(BF16)
        w_gate = jnp.pad(w_l[:, GATE_COL0:GATE_COL0 + 2 * NH_B], ((0, 0), (0, LANES - 2 * NH_B))).astype(BF16)
        g1 = norm1_g[l].reshape(1, D_MODEL)
        g2 = norm2_g[l].reshape(1, D_MODEL)
        gate_bias = mgate_b[l].reshape(1, 2 * NH_B)
        gb_row = jnp.pad(gate_bias, ((0, 0), (0, LANES - 2 * NH_B)))
        gb_col = gate_bias.reshape(2 * NH_B, 1)
        mcw, mcb = mconv_w[l], mconv_b[l].reshape(1, COLS_QK)
        fcw, fcb = fconv_w[l], fconv_b[l].reshape(1, 2 * D_FF)
        wpa, wpb, wo = w_pa[l].astype(BF16), w_pb[l].astype(BF16), w_o[l].astype(BF16)
        wup, wdn = w_up[l].astype(BF16), w_down[l].astype(BF16)

        *qkvs, qk, vo, gab, gcol, grow, conv_tail = _inproj(xp, g1, w_main, w_gate, tm=256,
                                                            act_dtype=BF16, seq=seq, conv=(mcw, mcb))
        os_, lses = [], []
        for g in range(N_GROUPS):
            d = DILATIONS[g]
            o_g, lse_g = _attn_prompt(qkvs[g].reshape(batch * d, seq // d, COLS_QKV_G), prompt_bias[g])
            shape = (n_p,) if g == 0 else (batch, d, seq // d)
            os_.append(o_g.reshape(shape + (A_GROUP,)))
            lses.append(lse_g.reshape(shape + (LANES,)))
        bout, c_p, n_p_state, m_p = _mlstm_prompt(qk, vo, gcol, grow, gb_row, gb_col, batch, seq)
        x2, h2 = _merge(os_, lses, bout, gab, xp, wpa, wpb, wo, g2, e_bf, tm=512, seq=seq)
        xp, u_tail = _ffn(h2, x2, wup, wdn, fcw, fcb, tm=256, tiles_per_batch=seq // 256,
                          final_g=fg if last else None)

        for g in range(N_GROUPS):
            d = DILATIONS[g]
            tail = qkvs[g].reshape(batch, d, seq // d, COLS_QKV_G)[:, :, seq // d - SPAN:, A_GROUP:]
            tail = tail.transpose(0, 2, 1, 3)
            p_st[g].append(tail.reshape(batch, SPAN * d, 2, H_G, DH_A).astype(F32))
        p_st[3].append(conv_tail[:, SUBLANES - (CONV_B - 1):])
        p_st[4].append(c_p)
        p_st[5].append(n_p_state)
        p_st[6].append(m_p[:, 0, :NH_B])
        p_st[7].append(u_tail[:, SUBLANES - (CONV_F - 1):])

        *qkvs_s, qk_s, vo_s, gab_s, gcol_s = _inproj(xs, g1, w_main, w_gate, tm=nreq, act_dtype=F32)
        new_qkv = jnp.stack(qkvs_s, axis=1).reshape(nreq, N_GROUPS, 3, H_G, DH_A)
        o_s, lse_s = _attn_sample(new_qkv, caches, l, sbias_m, sbias_0)
        bout_s, c_s, n_s, m_s, conv_s = _mlstm_sample(qk_s, vo_s, gcol_s, gb_row, state_mlstm_conv,
                                                      mcw, mcb, state_mlstm_C, state_mlstm_n, m0_all, l)
        lse_pad = jnp.pad(lse_s[..., 0], ((0, 0), (0, 0), (0, LANES - H_G)))
        x2_s, h2_s = _merge([o_s[:, g].reshape(nreq, A_GROUP) for g in range(N_GROUPS)],
                            [lse_pad[:, g] for g in range(N_GROUPS)],
                            bout_s.reshape(nreq, B_WIDTH), gab_s, xs, wpa, wpb, wo, g2, e_bf, tm=nreq)
        fbuf = state_ffn_conv[l]
        xs, u_s = _ffn(h2_s, x2_s, wup, wdn, fcw, fcb, tm=nreq,
                       prev_rows=[fbuf[:, i] for i in range(CONV_F - 1)],
                       final_g=fg if last else None)

        for g in range(N_GROUPS):
            s_st[g].append(new_qkv[:, g, 1:][:, None])
        s_st[3].append(conv_s)
        s_st[4].append(c_s)
        s_st[5].append(n_s)
        s_st[6].append(m_s[:, 0, :NH_B])
        s_st[7].append(jnp.concatenate([fbuf[:, 1:], u_s[:, None, :]], axis=1))

    outs = [xp.reshape(batch, seq, D_MODEL), xs.reshape(nreq, 1, D_MODEL)]
    for i in range(8):
        outs.append(jnp.stack(p_st[i], 0))
        outs.append(jnp.stack(s_st[i], 0))
    return tuple(outs)
```

```python
import functools
import math

import numpy as np
import jax
import jax.numpy as jnp
from jax import lax
from jax.experimental import pallas as pl
from jax.experimental.pallas import tpu as pltpu

F32 = jnp.float32
BF16 = jnp.bfloat16
HIGHEST = lax.Precision.HIGHEST

D_MODEL = 1024
WINDOWS = (128, 512, 2048)
DILATIONS = (1, 4, 16)
N_GROUPS = 3
H_G = 8
DH_A = 64
A_GROUP = H_G * DH_A
A_QKV = N_GROUPS * A_GROUP
Q_BLOCK = 128
SPAN = 128
NH_B = 4
DK_B = 256
B_WIDTH = NH_B * DK_B
CONV_B = 4
D_FF = 2816
CONV_F = 3
NUM_BUCKETS = 32
MAX_DISTANCE = 2048
RMS_EPS = 1e-6
NEG = -1e30

LANES = 128
SUBLANES = 8
FF_CHUNK = 256
FF_AHEAD = 3
MLSTM_CHUNK = 256
MLSTM_SEQS = 2
VMEM_LIMIT = 56 * 1024 * 1024

COLS_QKV_G = 3 * A_GROUP
COLS_QKV = N_GROUPS * COLS_QKV_G
COLS_QK = 2 * B_WIDTH
COLS_VO = 2 * B_WIDTH
COLS_GAB = 2 * D_MODEL
GATE_COL0 = COLS_QKV + COLS_QK + COLS_VO


def _dot(a, b):
    return jnp.dot(a, b, preferred_element_type=F32)


def _hdot(a, b):
    return jnp.dot(a, b, precision=HIGHEST, preferred_element_type=F32)


def _dot_nt(a, b):
    return lax.dot_general(a, b, (((1,), (1,)), ((), ())), preferred_element_type=F32)


def _dot_tn(a, b):
    return lax.dot_general(a, b, (((0,), (0,)), ((), ())), preferred_element_type=F32)


def _split3(x):
    hi = x.astype(BF16)
    rest = x - hi.astype(F32)
    mid = rest.astype(BF16)
    return hi, mid, (rest - mid.astype(F32)).astype(BF16)


def _sigmoid(x):
    return 1.0 / (1.0 + jnp.exp(-x))


def _log_sigmoid(x):
    return jnp.minimum(x, 0.0) - jnp.log1p(jnp.exp(-jnp.abs(x)))


def _gelu_tanh(x):
    return 0.5 * x * (1.0 + jnp.tanh(math.sqrt(2.0 / math.pi) * (x + 0.044715 * (x * x * x))))


def _rms(x, g):
    return x * lax.rsqrt(jnp.mean(x * x, axis=-1, keepdims=True) + RMS_EPS) * g


def _resident(shape):
    nd = len(shape)
    return pl.BlockSpec(shape, lambda *_: (0,) * nd, pipeline_mode=pl.Buffered(1))


def _params(n_grid):
    return pltpu.CompilerParams(dimension_semantics=("arbitrary",) * n_grid,
                                vmem_limit_bytes=VMEM_LIMIT)


def _inproj_kernel(*refs, prompt, tiles_per_batch):
    if prompt:
        (x_ref, g_ref, w_ref, wg_ref, cw_ref, cb_ref, qkv0_ref, qkv1_ref, qkv2_ref, qk_ref, vo_ref,
         gab_ref, gcol_ref, grow_ref, ctail_ref, kt_ref, hs_ref, ext_ref) = refs
    else:
        (x_ref, g_ref, w_ref, wg_ref, qkv0_ref, qkv1_ref, qkv2_ref, qk_ref, vo_ref, gab_ref,
         gcol_ref) = refs
    if prompt:
        @pl.when(pl.program_id(0) % tiles_per_batch == 0)
        def _():
            ext_ref[:, 0:SUBLANES, :] = jnp.zeros((ext_ref.shape[0], SUBLANES, LANES), F32)

    hf = _rms(x_ref[...], g_ref[...])
    h = hf.astype(BF16)
    tm = hf.shape[0]

    def project(lhs, col0, width, store):
        for c in range(0, width, 512):
            store(c, _dot(lhs, w_ref[:, col0 + c:col0 + c + 512]))

    def store_rows(ref):
        def store(c, res):
            ref[:, c:c + 512] = res.astype(ref.dtype)
        return store

    if prompt:
        pad = SUBLANES

        def store_slabs(c, res):
            for k in range(512 // LANES):
                ext_ref[c // LANES + k, pad:pad + tm, :] = res[:, k * LANES:(k + 1) * LANES]
        project(h, COLS_QKV, COLS_QK, store_slabs)
    project(h, 0, COLS_QKV_G, store_rows(qkv0_ref))
    if prompt:
        n_slab = hs_ref.shape[0]
        for k in range(n_slab):
            hs_ref[k] = hf[:, k * LANES:(k + 1) * LANES]
        for g, ref in ((1, qkv1_ref), (2, qkv2_ref)):
            d = DILATIONS[g]
            rows = tm // d
            hp = jnp.concatenate(
                [jnp.concatenate([hs_ref[k, pl.ds(r, rows, stride=d), :] for r in range(d)], axis=0)
                 for k in range(n_slab)], axis=1).astype(BF16)

            def store(c, res, ref=ref, d=d, rows=rows):
                res = res.astype(ref.dtype)
                for r in range(d):
                    ref[r, :, c:c + 512] = res[r * rows:(r + 1) * rows]
            project(hp, g * COLS_QKV_G, COLS_QKV_G, store)
    else:
        project(h, COLS_QKV_G, COLS_QKV_G, store_rows(qkv1_ref))
        project(h, 2 * COLS_QKV_G, COLS_QKV_G, store_rows(qkv2_ref))
    if prompt:
        for k in range(COLS_QK // LANES):
            ks = slice(k * LANES, (k + 1) * LANES)
            y = cb_ref[:, ks] + ext_ref[k, pad:pad + tm, :] * cw_ref[CONV_B - 1:CONV_B, ks]
            for i in range(CONV_B - 1):
                lo = pad - (CONV_B - 1) + i
                y = y + ext_ref[k, lo:lo + tm, :] * cw_ref[i:i + 1, ks]
            y = y * _sigmoid(y)
            if k * LANES < B_WIDTH:
                qk_ref[:, ks] = y.astype(qk_ref.dtype)
            else:
                kt_ref[k * LANES - B_WIDTH:(k + 1) * LANES - B_WIDTH, :] = (
                    (y * (DK_B ** -0.5)).T.astype(kt_ref.dtype))
            tail = ext_ref[k, tm:tm + pad, :]
            ctail_ref[:, ks] = tail
            ext_ref[k, 0:pad, :] = tail
    else:
        project(h, COLS_QKV, COLS_QK, store_rows(qk_ref))
    col = COLS_QKV + COLS_QK
    for ref in (vo_ref, gab_ref):
        project(h, col, ref.shape[-1], store_rows(ref))
        col += ref.shape[-1]
    gates = _dot(h, wg_ref[...])
    gcol_ref[...] = gates
    if prompt:
        grow_ref[...] = gates.T[:SUBLANES, :]


def _inproj(x, gain, w_main, w_gate, *, tm, act_dtype, seq=None, conv=None):
    n = x.shape[0]
    dilate = seq is not None
    row = lambda i: (i, 0)
    out_shape = [jax.ShapeDtypeStruct((n, COLS_QKV_G), act_dtype)]
    out_specs = [pl.BlockSpec((tm, COLS_QKV_G), row)]
    for g in (1, 2):
        if dilate:
            d = DILATIONS[g]
            tpb = seq // tm
            out_shape.append(jax.ShapeDtypeStruct((n // seq, d, seq // d, COLS_QKV_G), act_dtype))
            out_specs.append(pl.BlockSpec((None, d, tm // d, COLS_QKV_G),
                                          lambda i, tpb=tpb: (i // tpb, 0, i % tpb, 0)))
        else:
            out_shape.append(jax.ShapeDtypeStruct((n, COLS_QKV_G), act_dtype))
            out_specs.append(pl.BlockSpec((tm, COLS_QKV_G), row))
    for cols, dt in ((B_WIDTH, act_dtype) if dilate else (COLS_QK, F32), (COLS_VO, act_dtype),
                     (COLS_GAB, act_dtype), (LANES, F32)):
        out_shape.append(jax.ShapeDtypeStruct((n, cols), dt))
        out_specs.append(pl.BlockSpec((tm, cols), row))
    in_specs = [
        pl.BlockSpec((tm, D_MODEL), row),
        _resident((1, D_MODEL)),
        _resident(w_main.shape),
        _resident(w_gate.shape),
    ]
    args = [x, gain, w_main, w_gate]
    scratch = []
    tpb = None
    if dilate:
        tpb = seq // tm
        in_specs += [_resident(conv[0].shape), _resident(conv[1].shape)]
        args += list(conv)
        out_shape.append(jax.ShapeDtypeStruct((SUBLANES, n), F32))
        out_specs.append(pl.BlockSpec((SUBLANES, tm), lambda i: (0, i)))
        out_shape.append(jax.ShapeDtypeStruct((n // seq, SUBLANES, COLS_QK), F32))
        out_specs.append(pl.BlockSpec((None, SUBLANES, COLS_QK), lambda i: (i // tpb, 0, 0)))
        out_shape.append(jax.ShapeDtypeStruct((B_WIDTH, n), act_dtype))
        out_specs.append(pl.BlockSpec((B_WIDTH, tm), lambda i: (0, i)))
        scratch =[pltpu.VMEM((D_MODEL // LANES, tm, LANES), F32),
                   pltpu.VMEM((COLS_QK // LANES, tm + SUBLANES, LANES), F32)]
    return pl.pallas_call(
        functools.partial(_inproj_kernel, prompt=dilate, tiles_per_batch=tpb),
        grid=(n // tm,),
        in_specs=in_specs,
        out_specs=out_specs,
        out_shape=out_shape,
        scratch_shapes=scratch,
        compiler_params=_params(1),
        name="inproj",
    )(*args)


def _attn_kernel(q_ref, kp_ref, kc_ref, vp_ref, vc_ref, bias_ref, o_ref, lse_ref):
    n_blk = q_ref.shape[0] // Q_BLOCK
    has_prev = pl.program_id(1) > 0
    key_lane = lax.broadcasted_iota(jnp.int32, (1, 1, 2 * Q_BLOCK), 2)
    first_mask = jnp.where((key_lane < Q_BLOCK) & jnp.logical_not(has_prev), NEG, 0.0)
    lane = lax.broadcasted_iota(jnp.int32, (Q_BLOCK, LANES), 1)
    low_half = lane < DH_A
    pair = 2 * DH_A
    for i in range(n_blk):
        rows = slice(i * Q_BLOCK, (i + 1) * Q_BLOCK)
        if i == 0:
            kprev_ref, vprev_ref, prows = kp_ref, vp_ref, slice(0, Q_BLOCK)
        else:
            kprev_ref, vprev_ref, prows = kc_ref, vc_ref, slice((i - 1) * Q_BLOCK, i * Q_BLOCK)
        scores = []
        for hp in range(H_G // 2):
            cols = slice(hp * pair, (hp + 1) * pair)
            qp = q_ref[rows, cols]
            kk = jnp.concatenate([kprev_ref[prows, cols], kc_ref[rows, cols]], axis=0)
            scores.append(_dot_nt(jnp.where(low_half, qp, jnp.zeros_like(qp)), kk))
            scores.append(_dot_nt(jnp.where(low_half, jnp.zeros_like(qp), qp), kk))
        s = jnp.stack(scores) * (DH_A ** -0.5) + bias_ref[...]
        if i == 0:
            s = s + first_mask
        m = jnp.max(s, axis=-1, keepdims=True)
        p = jnp.exp(s - m)
        l = jnp.sum(p, axis=-1, keepdims=True)
        pb = p.astype(BF16)
        inv = 1.0 / l
        lse = m + jnp.log(l)
        lse_all = jnp.zeros((Q_BLOCK, LANES), F32)
        for hp in range(H_G // 2):
            cols = slice(hp * pair, (hp + 1) * pair)
            vv = jnp.concatenate([vprev_ref[prows, cols], vc_ref[rows, cols]], axis=0)
            o_lo = _dot(pb[2 * hp], vv) * inv[2 * hp]
            o_hi = _dot(pb[2 * hp + 1], vv) * inv[2 * hp + 1]
            o_ref[rows, cols] = jnp.where(low_half, o_lo, o_hi).astype(o_ref.dtype)
        for h in range(H_G):
            lse_all = jnp.where(lane == h, lse[h], lse_all)
        lse_ref[rows, :] = lse_all


ATTN_BLOCKS = 4


def _attn_prompt(qkv, bias):
    nsub, u_len, _ = qkv.shape
    n_blk = math.gcd(ATTN_BLOCKS, u_len // Q_BLOCK)
    rows = n_blk * Q_BLOCK
    nb = u_len // rows

    def spec(col_block, prev):
        if prev:
            return pl.BlockSpec((None, Q_BLOCK, A_GROUP),
                                lambda s, j: (s, jnp.maximum(j * n_blk - 1, 0), col_block))
        return pl.BlockSpec((None, rows, A_GROUP), lambda s, j: (s, j, col_block))

    return pl.pallas_call(
        _attn_kernel,
        grid=(nsub, nb),
        in_specs=[spec(0, False), spec(1, True), spec(1, False), spec(2, True), spec(2, False),
                  _resident(bias.shape)],
        out_specs=[
            pl.BlockSpec((None, rows, A_GROUP), lambda s, j: (s, j, 0)),
            pl.BlockSpec((None, rows, LANES), lambda s, j: (s, j, 0)),
        ],
        out_shape=[
            jax.ShapeDtypeStruct((nsub, u_len, A_GROUP), BF16),
            jax.ShapeDtypeStruct((nsub, u_len, LANES), F32),
        ],
        compiler_params=_params(2),
        name="attn_prompt",
    )(qkv, qkv, qkv, qkv, qkv, bias)


def _sattn_kernel(new_ref, kv0_ref, kv1_ref, kv2_ref, bt0_ref, bt1_ref, bt2_ref, b0_ref,
                  o_ref, lse_ref):
    eye = (lax.broadcasted_iota(jnp.int32, (DH_A, DH_A), 0)
           == lax.broadcasted_iota(jnp.int32, (DH_A, DH_A), 1))
    for g, (kv_ref, bt_ref) in enumerate(((kv0_ref, bt0_ref), (kv1_ref, bt1_ref),
                                          (kv2_ref, bt2_ref))):
        q = new_ref[g, 0]
        k_new = new_ref[g, 1]
        v_new = new_ref[g, 2]
        s_rows = []
        for h in range(H_G):
            q_col = jnp.sum(jnp.where(eye, q[h:h + 1, :], 0.0), axis=1, keepdims=True)
            s_rows.append(jnp.sum(kv_ref[0, h] * q_col, axis=0, keepdims=True))
        s = jnp.concatenate(s_rows, axis=0) * (DH_A ** -0.5) + bt_ref[...]
        s0 = jnp.sum(k_new * q, axis=1, keepdims=True) * (DH_A ** -0.5) + b0_ref[g]
        m = jnp.maximum(jnp.max(s, axis=1, keepdims=True), s0)
        p = jnp.exp(s - m[:, 0:1])
        p0 = jnp.exp(s0 - m)
        l = jnp.sum(p, axis=1, keepdims=True) + p0
        o_rows = []
        for h in range(H_G):
            o_col = jnp.sum(kv_ref[1, h] * p[h:h + 1, :], axis=1, keepdims=True)
            o_rows.append(jnp.sum(jnp.where(eye, o_col, 0.0), axis=0, keepdims=True))
        o_ref[g] = (jnp.concatenate(o_rows, axis=0) + p0 * v_new) / l
        lse_ref[g] = m + jnp.log(l)


def _attn_sample(new_qkv, caches, layer, bias_t, bias_0):
    nreq = new_qkv.shape[0]
    views = []
    specs = []
    for g, cache in enumerate(caches):
        n_buf = cache.shape[2]
        assert n_buf == SPAN * DILATIONS[g], "cache must hold exactly one window"
        views.append(cache.transpose(0, 1, 3, 4, 5, 2))
        specs.append(pl.BlockSpec((None, None, 2, H_G, DH_A, n_buf),
                                  lambda b, layer=layer: (layer, b, 0, 0, 0, 0)))
    out = jax.ShapeDtypeStruct((nreq, N_GROUPS, H_G, DH_A), F32)
    out_spec = pl.BlockSpec((None, N_GROUPS, H_G, DH_A), lambda b: (b, 0, 0, 0))
    return pl.pallas_call(
        _sattn_kernel,
        grid=(nreq,),
        in_specs=[pl.BlockSpec((None, N_GROUPS, 3, H_G, DH_A), lambda b: (b, 0, 0, 0, 0))] + specs
        + [_resident(t.shape) for t in bias_t] + [_resident(bias_0.shape)],
        out_specs=[out_spec, out_spec],
        out_shape=[out, out],
        compiler_params=_params(1),
        name="attn_sample",
    )(new_qkv, *views, *bias_t, bias_0)


def _mlstm_kernel(*refs, n_seq):
    q_ref = refs[0]
    kt_refs = refs[1:1 + n_seq]
    vo_ref, gcol_ref = refs[1 + n_seq:3 + n_seq]
    grow_refs = refs[3 + n_seq:3 + 2 * n_seq]
    gb_row_ref, gb_col_ref, bout_ref, c_ref, n_ref, m_ref, nrep_s = refs[3 + 2 * n_seq:]
    chunk = q_ref.shape[1]

    @pl.when(pl.program_id(1) == 0)
    def _():
        c_ref[...] = jnp.zeros_like(c_ref)
        nrep_s[...] = jnp.zeros_like(nrep_s)
        m_ref[...] = jnp.zeros_like(m_ref)

    ti = lax.broadcasted_iota(jnp.int32, (chunk, chunk), 0)
    si = lax.broadcasted_iota(jnp.int32, (chunk, chunk), 1)
    causal = ti >= si
    lower = causal.astype(BF16)
    upper = (si >= ti).astype(BF16)
    items = [(j, h) for j in range(n_seq) for h in range(NH_B)]
    hcol = lambda h: slice(h * DK_B, (h + 1) * DK_B)
    a_rows, b_ts, b_lasts = [], [], []
    for j in range(n_seq):
        z_col = gcol_ref[j] + gb_row_ref[...]
        z_row = grow_refs[j][...] + gb_col_ref[...]
        b_col = sum(_dot(lower, part) for part in _split3(_log_sigmoid(z_col)))
        b_row = sum(_dot(part, upper) for part in _split3(_log_sigmoid(z_row)))
        for h in range(NH_B):
            a_rows.append(z_row[h:h + 1, :] - b_row[NH_B + h:NH_B + h + 1, :])
            b_ts.append(b_col[:, NH_B + h:NH_B + h + 1])
            b_lasts.append(b_row[NH_B + h:NH_B + h + 1, chunk - 1:chunk])
    a_row = jnp.stack(a_rows)
    b_t = jnp.stack(b_ts)
    m_prev = jnp.stack([m_ref[j, :, h:h + 1] for j, h in items])
    a_mat = jnp.where(causal, a_row, NEG)
    gmax = jnp.maximum(m_prev, jnp.max(a_mat, axis=-1, keepdims=True))
    dw = jnp.exp(a_mat - gmax)
    iw = jnp.exp(m_prev - gmax)
    g_last = gmax[:, chunk - 1:chunk, :]
    decay = jnp.exp(m_prev - g_last)
    w_state = jnp.exp(a_row - g_last)

    qb = [q_ref[j, :, hcol(h)] for j, h in items]
    kt = [kt_refs[j][hcol(h), :] for j, h in items]
    vb = [vo_ref[j, :, hcol(h)] for j, h in items]
    c_old = [c_ref[j, h] for j, h in items]
    n_old = [nrep_s[j, h] for j, h in items]
    idx = range(len(items))
    qk = jnp.stack([_dot(qb[i], kt[i]) for i in idx]) * dw
    qkb = qk.astype(BF16)
    q_c = jnp.stack([_dot(qb[i], c_old[i].astype(BF16)) for i in idx])
    qk_v = jnp.stack([_dot(qkb[i], vb[i]) for i in idx])
    num = iw * q_c + qk_v
    q_n = jnp.stack([_dot(qb[i], n_old[i].astype(BF16))[:, 0:1] for i in idx])
    den = iw * q_n + jnp.sum(qk, axis=-1, keepdims=True)
    hid = num / jnp.maximum(jnp.abs(den), jnp.exp(-(b_t + gmax)))
    for i, (j, h) in enumerate(items):
        o_gate = _sigmoid(vo_ref[j, :, B_WIDTH + h * DK_B:B_WIDTH + (h + 1) * DK_B].astype(F32))
        bout_ref[j, :, hcol(h)] = (o_gate * hid[i]).astype(bout_ref.dtype)

    kwt = (jnp.stack(kt).astype(F32) * w_state).astype(BF16)
    ones = jnp.ones((chunk, LANES), BF16)
    for i, (j, h) in enumerate(items):
        c_ref[j, h] = decay[i] * c_old[i] + _dot(kwt[i], vb[i])
        nrep_s[j, h] = decay[i] * n_old[i] + _dot(kwt[i], ones)
        m_ref[j, :, h:h + 1] = b_lasts[i] + g_last[i]

    @pl.when(pl.program_id(1) == pl.num_programs(1) - 1)
    def _():
        for j, h in items:
            n_ref[j, h:h + 1, :] = nrep_s[j, h].T[0:1, :]


def _mlstm_prompt(q, kt, vo, gcol, grow, gb_row, gb_col, batch, seq):
    chunk = MLSTM_CHUNK
    n_seq = math.gcd(MLSTM_SEQS, batch)
    nc = seq // chunk
    seq3 = lambda b, c: (b, c, 0)
    lanes = [lambda b, c, j=j: (0, (b * n_seq + j) * nc + c) for j in range(n_seq)]
    state = lambda b, c: (b, 0, 0)
    return pl.pallas_call(
        functools.partial(_mlstm_kernel, n_seq=n_seq),
        grid=(batch // n_seq, nc),
        in_specs=[pl.BlockSpec((n_seq, chunk, B_WIDTH), seq3)]
        + [pl.BlockSpec((B_WIDTH, chunk), lanes[j]) for j in range(n_seq)]
        + [pl.BlockSpec((n_seq, chunk, COLS_VO), seq3), pl.BlockSpec((n_seq, chunk, LANES), seq3)]
        + [pl.BlockSpec((SUBLANES, chunk), lanes[j]) for j in range(n_seq)]
        + [_resident(gb_row.shape), _resident(gb_col.shape)],
        out_specs=[
            pl.BlockSpec((n_seq, chunk, B_WIDTH), seq3),
            pl.BlockSpec((n_seq, NH_B, DK_B, DK_B), lambda b, c: (b, 0, 0, 0)),
            pl.BlockSpec((n_seq, NH_B, DK_B), state),
            pl.BlockSpec((n_seq, 1, LANES), state),
        ],
        out_shape=[
            jax.ShapeDtypeStruct((batch, seq, B_WIDTH), BF16),
            jax.ShapeDtypeStruct((batch, NH_B, DK_B, DK_B), F32),
            jax.ShapeDtypeStruct((batch, NH_B, DK_B), F32),
            jax.ShapeDtypeStruct((batch, 1, LANES), F32),
        ],
        scratch_shapes=[pltpu.VMEM((n_seq, NH_B, DK_B, LANES), F32)],
        compiler_params=_params(2),
        name="mlstm_prompt",
    )(q.reshape(batch, seq, B_WIDTH), *([kt] * n_seq), vo.reshape(batch, seq, COLS_VO),
      gcol.reshape(batch, seq, LANES), *([grow] * n_seq), gb_row, gb_col)


def _smlstm_kernel(qk_ref, vo_ref, g_ref, gb_ref, cs_ref, cw_ref, cb_ref, c0_ref, n0_ref, m0_ref,
                   bout_ref, c1_ref, n1_ref, m1_ref, ncs_ref):
    u = qk_ref[...]
    y = cb_ref[...] + u * cw_ref[CONV_B - 1:CONV_B, :]
    for i in range(CONV_B - 1):
        y = y + cs_ref[i:i + 1, :] * cw_ref[i:i + 1, :]
    ncs_ref[0:CONV_B - 2, :] = cs_ref[1:CONV_B - 1, :]
    ncs_ref[CONV_B - 2:CONV_B - 1, :] = u
    y = y * _sigmoid(y)
    qf = y[:, :B_WIDTH]
    kf = y[:, B_WIDTH:] * (DK_B ** -0.5)
    z = g_ref[...] + gb_ref[...]
    lf_all = _log_sigmoid(z)
    m0 = m0_ref[...]
    eye = (lax.broadcasted_iota(jnp.int32, (DK_B, DK_B), 0)
           == lax.broadcasted_iota(jnp.int32, (DK_B, DK_B), 1))
    for h in range(NH_B):
        hs = slice(h * DK_B, (h + 1) * DK_B)
        ig = z[:, h:h + 1]
        inter = lf_all[:, NH_B + h:NH_B + h + 1] + m0[:, h:h + 1]
        m_t = jnp.maximum(inter, ig)
        dw = jnp.exp(ig - m_t)
        iw = jnp.exp(inter - m_t)
        qh = qf[:, hs]
        kh = kf[:, hs]
        vh = vo_ref[:, hs]
        c_old = c0_ref[h]
        n_old = n0_ref[h:h + 1, :]
        qk = jnp.sum(qh * kh, axis=1, keepdims=True) * dw
        num = iw * _hdot(qh, c_old) + qk * vh
        den = iw * jnp.sum(qh * n_old, axis=1, keepdims=True) + qk
        hid = num / jnp.maximum(jnp.abs(den), jnp.exp(-m_t))
        o_gate = _sigmoid(vo_ref[:, B_WIDTH + h * DK_B:B_WIDTH + (h + 1) * DK_B])
        bout_ref[:, hs] = o_gate * hid
        k_col = jnp.sum(jnp.where(eye, kh, 0.0), axis=1, keepdims=True)
        c1_ref[h] = iw * c_old + (dw * k_col) * vh
        n1_ref[h:h + 1, :] = iw * n_old + dw * kh
        m1_ref[:, h:h + 1] = m_t


def _mlstm_sample(qk_pre, vo, gcol, gb_row, conv_state, cw, cb, c0, n0, m0, layer):
    nreq = qk_pre.shape[0]
    one = lambda b: (b, 0, 0)
    lay3 = lambda b, layer=layer: (layer, b, 0, 0)
    lay4 = lambda b, layer=layer: (layer, b, 0, 0, 0)
    return pl.pallas_call(
        _smlstm_kernel,
        grid=(nreq,),
        in_specs=[
            pl.BlockSpec((None, 1, COLS_QK), one),
            pl.BlockSpec((None, 1, COLS_VO), one),
            pl.BlockSpec((None, 1, LANES), one),
            _resident(gb_row.shape),
            pl.BlockSpec((None, None, CONV_B - 1, COLS_QK), lay3),
            _resident(cw.shape), _resident(cb.shape),
            pl.BlockSpec((None, None, NH_B, DK_B, DK_B), lay4),
            pl.BlockSpec((None, None, NH_B, DK_B), lay3),
            pl.BlockSpec((None, None, 1, LANES), lay3),
        ],
        out_specs=[
            pl.BlockSpec((None, 1, B_WIDTH), one),
            pl.BlockSpec((None, NH_B, DK_B, DK_B), lambda b: (b, 0, 0, 0)),
            pl.BlockSpec((None, NH_B, DK_B), one),
            pl.BlockSpec((None, 1, LANES), one),
            pl.BlockSpec((None, CONV_B - 1, COLS_QK), one),
        ],
        out_shape=[
            jax.ShapeDtypeStruct((nreq, 1, B_WIDTH), F32),
            jax.ShapeDtypeStruct((nreq, NH_B, DK_B, DK_B), F32),
            jax.ShapeDtypeStruct((nreq, NH_B, DK_B), F32),
            jax.ShapeDtypeStruct((nreq, 1, LANES), F32),
            jax.ShapeDtypeStruct((nreq, CONV_B - 1, COLS_QK), F32),
        ],
        compiler_params=_params(1),
        name="mlstm_sample",
    )(qk_pre.reshape(nreq, 1, COLS_QK), vo.reshape(nreq, 1, COLS_VO),
      gcol.reshape(nreq, 1, LANES), gb_row, conv_state, cw, cb, c0, n0, m0)


def _merge_kernel(o0_ref, o1_ref, o2_ref, l0_ref, l1_ref, l2_ref, bo_ref, gab_ref, x_ref,
                  wpa_ref, wpb_ref, wo_ref, g2_ref, e_ref, x2_ref, h2_ref, *scratch):
    tm = x_ref.shape[0]

    def natural(ref, buf):
        dil, rows, _ = ref.shape
        n_slab = buf.shape[0]
        for r in range(dil):
            val = ref[r].astype(F32)
            for k in range(n_slab):
                buf[k, pl.ds(r, rows, stride=dil), :] = val[:, k * LANES:(k + 1) * LANES]
        return jnp.concatenate([buf[k] for k in range(n_slab)], axis=1)

    if scratch:
        os_ = [o0_ref[...].astype(F32), natural(o1_ref, scratch[0]), natural(o2_ref, scratch[1])]
        lses = [l0_ref[...], natural(l1_ref, scratch[2]), natural(l2_ref, scratch[3])]
    else:
        os_ = [r[...].astype(F32) for r in (o0_ref, o1_ref, o2_ref)]
        lses = [r[...] for r in (l0_ref, l1_ref, l2_ref)]
    top = jnp.maximum(jnp.maximum(lses[0], lses[1]), lses[2])
    ws = [jnp.exp(l - top) for l in lses]
    inv = 1.0 / (ws[0] + ws[1] + ws[2])
    e = e_ref[...]
    a = None
    for w, o in zip(ws, os_):
        alpha = w * inv
        hi = alpha.astype(BF16)
        lo = (alpha - hi.astype(F32)).astype(BF16)
        term = (_dot(hi, e) + _dot(lo, e)) * o
        a = term if a is None else a + term
    pa = _dot(a.astype(BF16), wpa_ref[...])
    pb = _dot(bo_ref[...].astype(BF16), wpb_ref[...])
    merged = (_sigmoid(gab_ref[:, :D_MODEL].astype(F32)) * pa
              + _sigmoid(gab_ref[:, D_MODEL:].astype(F32)) * pb)
    x2 = x_ref[...] + _dot(merged.astype(BF16), wo_ref[...])
    x2_ref[...] = x2
    h2_ref[...] = _rms(x2, g2_ref[...]).astype(BF16)


def _merge(os_, lses, bout, gab, x, wpa, wpb, wo, g2, e_bf, *, tm, seq=None):
    n = x.shape[0]
    row = lambda i: (i, 0)

    def group_specs(cols):
        specs = [pl.BlockSpec((tm, cols), row)]
        for g in (1, 2):
            if seq is None:
                specs.append(pl.BlockSpec((tm, cols), row))
            else:
                d = DILATIONS[g]
                tpb = seq // tm
                specs.append(pl.BlockSpec((None, d, tm // d, cols),
                                          lambda i, tpb=tpb: (i // tpb, 0, i % tpb, 0)))
        return specs

    scratch = []
    if seq is not None:
        scratch = ([pltpu.VMEM((A_GROUP // LANES, tm, LANES), F32)] * 2
                   + [pltpu.VMEM((1, tm, LANES), F32)] * 2)
    return pl.pallas_call(
        _merge_kernel,
        grid=(n // tm,),
        scratch_shapes=scratch,
        in_specs=group_specs(A_GROUP) + group_specs(LANES) + [
            pl.BlockSpec((tm, B_WIDTH), row),
            pl.BlockSpec((tm, COLS_GAB), row),
            pl.BlockSpec((tm, D_MODEL), row),
            _resident(wpa.shape), _resident(wpb.shape), _resident(wo.shape),
            _resident(g2.shape), _resident(e_bf.shape),
        ],
        out_specs=[pl.BlockSpec((tm, D_MODEL), row), pl.BlockSpec((tm, D_MODEL), row)],
        out_shape=[jax.ShapeDtypeStruct((n, D_MODEL), F32), jax.ShapeDtypeStruct((n, D_MODEL), BF16)],
        compiler_params=_params(1),
        name="merge",
    )(*os_, *lses, bout, gab, x, wpa, wpb, wo, g2, e_bf)


def _ffn_kernel(*refs, tiles_per_batch, from_state, final_norm):
    h2_ref, x2_ref, wup_ref, wdn_ref, cw_ref, cb_ref = refs[:6]
    pos = 6
    if from_state:
        prev_refs = refs[pos:pos + CONV_F - 1]
        pos += CONV_F - 1
    if final_norm:
        fg_ref = refs[pos]
        pos += 1
    x3_ref, u_ref = refs[pos:pos + 2]
    tm = h2_ref.shape[0]
    pad = SUBLANES
    if not from_state:
        ext_s = refs[pos + 2]

        @pl.when(pl.program_id(0) % tiles_per_batch == 0)
        def _():
            ext_s[:, 0:pad, :] = jnp.zeros((ext_s.shape[0], pad, LANES), F32)

    h2 = h2_ref[...]
    acc = x2_ref[...]
    n_chunk = D_FF // FF_CHUNK

    def halves(c):
        return (c * FF_CHUNK, D_FF + c * FF_CHUNK)

    def up(c):
        us = []
        for off in halves(c):
            u = _dot(h2, wup_ref[:, off:off + FF_CHUNK])
            if from_state:
                u_ref[:, off:off + FF_CHUNK] = u
                us.append(u)
            else:
                for k in range(FF_CHUNK // LANES):
                    ext_s[off // LANES + k, pad:pad + tm, :] = u[:, k * LANES:(k + 1) * LANES]
        return us

    def conv(c, us):
        ys = []
        for j, off in enumerate(halves(c)):
            cs = slice(off, off + FF_CHUNK)
            if from_state:
                y = cb_ref[:, cs] + us[j] * cw_ref[CONV_F - 1:CONV_F, cs]
                for i in range(CONV_F - 1):
                    y = y + prev_refs[i][:, cs] * cw_ref[i:i + 1, cs]
            else:
                parts = []
                for k in range(FF_CHUNK // LANES):
                    slab = off // LANES + k
                    ks = slice(off + k * LANES, off + (k + 1) * LANES)
                    part = cb_ref[:, ks]
                    for i in range(CONV_F):
                        lo = pad - (CONV_F - 1) + i
                        part = part + ext_s[slab, lo:lo + tm, :] * cw_ref[i:i + 1, ks]
                    parts.append(part)
                    tail = ext_s[slab, tm:tm + pad, :]
                    u_ref[:, ks] = tail
                    ext_s[slab, 0:pad, :] = tail
                y = jnp.concatenate(parts, axis=1)
            ys.append(y)
        return ys

    ahead = 1 if from_state else FF_AHEAD
    pending = [up(c) for c in range(min(ahead, n_chunk))]
    for c in range(n_chunk):
        us = pending.pop(0)
        if c + ahead < n_chunk:
            pending.append(up(c + ahead))
        y_act, y_gate = conv(c, us)
        act = _gelu_tanh(y_act) * y_gate
        acc = acc + _dot(act.astype(BF16), wdn_ref[c * FF_CHUNK:(c + 1) * FF_CHUNK, :])
    if final_norm:
        acc = _rms(acc, fg_ref[...])
    x3_ref[...] = acc


def _ffn(h2, x2, wup, wdn, cw, cb, *, tm, tiles_per_batch=None, prev_rows=None, final_g=None):
    n = h2.shape[0]
    from_state = prev_rows is not None
    row = lambda i: (i, 0)
    in_specs = [
        pl.BlockSpec((tm, D_MODEL), row), pl.BlockSpec((tm, D_MODEL), row),
        _resident(wup.shape), _resident(wdn.shape), _resident(cw.shape), _resident(cb.shape),
    ]
    args = [h2, x2, wup, wdn, cw, cb]
    scratch = []
    if from_state:
        in_specs += [pl.BlockSpec((tm, 2 * D_FF), row)] * (CONV_F - 1)
        args += list(prev_rows)
        u_shape = jax.ShapeDtypeStruct((n, 2 * D_FF), F32)
        u_spec = pl.BlockSpec((tm, 2 * D_FF), row)
    else:
        n_batch = n // (tm * tiles_per_batch)
        u_shape = jax.ShapeDtypeStruct((n_batch, SUBLANES, 2 * D_FF), F32)
        u_spec = pl.BlockSpec((None, SUBLANES, 2 * D_FF), lambda i: (i // tiles_per_batch, 0, 0))
        scratch = [pltpu.VMEM((2 * D_FF // LANES, tm + SUBLANES, LANES), F32)]
    if final_g is not None:
        in_specs.append(_resident(final_g.shape))
        args.append(final_g)
    return pl.pallas_call(
        functools.partial(_ffn_kernel, tiles_per_batch=tiles_per_batch, from_state=from_state,
                          final_norm=final_g is not None),
        grid=(n // tm,),
        in_specs=in_specs,
        out_specs=[pl.BlockSpec((tm, D_MODEL), row), u_spec],
        out_shape=[jax.ShapeDtypeStruct((n, D_MODEL), F32), u_shape],
        scratch_shapes=scratch,
        compiler_params=_params(1),
        name="ffn",
    )(*args)


def _t5_bucket(dist):
    max_exact = NUM_BUCKETS // 2
    df = jnp.maximum(dist, 1).astype(F32)
    large = max_exact + (jnp.log(df / max_exact) / math.log(MAX_DISTANCE / max_exact)
                         * (NUM_BUCKETS - max_exact)).astype(jnp.int32)
    large = jnp.minimum(large, NUM_BUCKETS - 1)
    return jnp.where(dist < max_exact, dist, large)


def _bias_table(rel_bias, g, dist):
    bucket = _t5_bucket(jnp.asarray(dist, jnp.int32))
    table = rel_bias[:, g * H_G:(g + 1) * H_G].reshape((NUM_BUCKETS, H_G) + (1,) * bucket.ndim)
    ids = jnp.arange(NUM_BUCKETS).reshape((NUM_BUCKETS, 1) + (1,) * bucket.ndim)
    return jnp.sum(jnp.where(bucket[None, None] == ids, table, 0.0), axis=0)


def _prompt_bias(rel_bias, g):
    qi = np.arange(Q_BLOCK)[:, None]
    ki = np.arange(2 * Q_BLOCK)[None, :]
    rel = qi + Q_BLOCK - ki
    band = (rel >= 0) & (rel <= SPAN)
    bias = _bias_table(rel_bias, g, np.maximum(rel, 0) * DILATIONS[g])
    return jnp.where(band[None], bias, NEG)


def _sample_bias(rel_bias):
    tables, news = [], []
    for g in range(N_GROUPS):
        dil = DILATIONS[g]
        n_buf = SPAN * dil
        pos = np.arange(n_buf)
        bias = _bias_table(rel_bias, g, n_buf - pos)
        tables.append(jnp.where((pos % dil == 0)[None], bias, NEG))
        news.append(jnp.broadcast_to(_bias_table(rel_bias, g, np.zeros((1,), np.int32)), (H_G, DH_A)))
    return tables, jnp.stack(news)


def _head_indicator():
    e = np.zeros((LANES, A_GROUP), np.float32)
    for h in range(H_G):
        e[h, h * DH_A:(h + 1) * DH_A] = 1.0
    return e


def kernel(x_prompt, x_sample, cache_kv_w128, cache_kv_w512, cache_kv_w2048, state_mlstm_conv,
           state_mlstm_C, state_mlstm_n, state_mlstm_m, state_ffn_conv, rel_bias, norm1_g, w_in,
           mconv_w, mconv_b, mgate_b, w_pa, w_pb, w_o, norm2_g, w_up, fconv_w, fconv_b, w_down,
           final_norm_g):
    batch, seq, _ = x_prompt.shape
    nreq = x_sample.shape[0]
    depth = w_in.shape[0]
    n_p = batch * seq
    caches = (cache_kv_w128, cache_kv_w512, cache_kv_w2048)

    assert seq % (Q_BLOCK * DILATIONS[-1]) == 0 and seq >= WINDOWS[-1]
    e_bf = jnp.asarray(_head_indicator(), BF16)
    prompt_bias = [_prompt_bias(rel_bias, g) for g in range(N_GROUPS)]
    sbias_m, sbias_0 = _sample_bias(rel_bias)
    m0_all = jnp.pad(state_mlstm_m, ((0, 0), (0, 0), (0, LANES - NH_B)))[:, :, None, :]
    fg = final_norm_g.reshape(1, D_MODEL)

    xp = x_prompt.reshape(n_p, D_MODEL)
    xs = x_sample.reshape(nreq, D_MODEL)
    p_st = [[] for _ in range(8)]
    s_st = [[] for _ in range(8)]

    for l in range(depth):
        last = l == depth - 1
        w_l = w_in[l]
        w_qkv = w_l[:, :COLS_QKV].reshape(D_MODEL, 3, N_GROUPS, A_GROUP).transpose(0, 2, 1, 3)
        w_main = jnp.concatenate([w_qkv.reshape(D_MODEL, COLS_QKV), w_l[:, COLS_QKV:GATE_COL0],
                                  w_l[:, GATE_COL0 + 2 * NH_B:]], axis=1).astype(BF16)
        w_gate = jnp.pad(w_l[:, GATE_COL0:GATE_COL0 + 2 * NH_B], ((0, 0), (0, LANES - 2 * NH_B))).astype(BF16)
        g1 = norm1_g[l].reshape(1, D_MODEL)
        g2 = norm2_g[l].reshape(1, D_MODEL)
        gate_bias = mgate_b[l].reshape(1, 2 * NH_B)
        gb_row = jnp.pad(gate_bias, ((0, 0), (0, LANES - 2 * NH_B)))
        gb_col = gate_bias.reshape(2 * NH_B, 1)
        mcw, mcb = mconv_w[l], mconv_b[l].reshape(1, COLS_QK)
        fcw, fcb = fconv_w[l], fconv_b[l].reshape(1, 2 * D_FF)
        wpa, wpb, wo = w_pa[l].astype(BF16), w_pb[l].astype(BF16), w_o[l].astype(BF16)
        wup, wdn = w_up[l].astype(BF16), w_down[l].astype(BF16)

        *qkvs, q_b, vo, gab, gcol, grow, conv_tail, kt_b = _inproj(
            xp, g1, w_main, w_gate, tm=256, act_dtype=BF16, seq=seq, conv=(mcw, mcb))
        os_, lses = [], []
        for g in range(N_GROUPS):
            d = DILATIONS[g]
            o_g, lse_g = _attn_prompt(qkvs[g].reshape(batch * d, seq // d, COLS_QKV_G), prompt_bias[g])
            shape = (n_p,) if g == 0 else (batch, d, seq // d)
            os_.append(o_g.reshape(shape + (A_GROUP,)))
            lses.append(lse_g.reshape(shape + (LANES,)))
        bout, c_p, n_p_state, m_p = _mlstm_prompt(q_b, kt_b, vo, gcol, grow, gb_row, gb_col,
                                                  batch, seq)
        x2, h2 = _merge(os_, lses, bout.reshape(n_p, B_WIDTH), gab, xp, wpa, wpb, wo, g2, e_bf,
                        tm=512, seq=seq)
        xp, u_tail = _ffn(h2, x2, wup, wdn, fcw, fcb, tm=256, tiles_per_batch=seq // 256,
                          final_g=fg if last else None)

        for g in range(N_GROUPS):
            d = DILATIONS[g]
            tail = qkvs[g].reshape(batch, d, seq // d, COLS_QKV_G)[:, :, seq // d - SPAN:, A_GROUP:]
            tail = tail.transpose(0, 2, 1, 3)
            p_st[g].append(tail.reshape(batch, SPAN * d, 2, H_G, DH_A).astype(F32))
        p_st[3].append(conv_tail[:, SUBLANES - (CONV_B - 1):])
        p_st[4].append(c_p)
        p_st[5].append(n_p_state)
        p_st[6].append(m_p[:, 0, :NH_B])
        p_st[7].append(u_tail[:, SUBLANES - (CONV_F - 1):])

        *qkvs_s, qk_s, vo_s, gab_s, gcol_s = _inproj(xs, g1, w_main, w_gate, tm=nreq, act_dtype=F32)
        new_qkv = jnp.stack(qkvs_s, axis=1).reshape(nreq, N_GROUPS, 3, H_G, DH_A)
        o_s, lse_s = _attn_sample(new_qkv, caches, l, sbias_m, sbias_0)
        bout_s, c_s, n_s, m_s, conv_s = _mlstm_sample(qk_s, vo_s, gcol_s, gb_row, state_mlstm_conv,
                                                      mcw, mcb, state_mlstm_C, state_mlstm_n, m0_all, l)
        lse_pad = jnp.pad(lse_s[..., 0], ((0, 0), (0, 0), (0, LANES - H_G)))
        x2_s, h2_s = _merge([o_s[:, g].reshape(nreq, A_GROUP) for g in range(N_GROUPS)],
                            [lse_pad[:, g] for g in range(N_GROUPS)],
                            bout_s.reshape(nreq, B_WIDTH), gab_s, xs, wpa, wpb, wo, g2, e_bf, tm=nreq)
        fbuf = state_ffn_conv[l]
        xs, u_s = _ffn(h2_s, x2_s, wup, wdn, fcw, fcb, tm=nreq,
                       prev_rows=[fbuf[:, i] for i in range(CONV_F - 1)],
                       final_g=fg if last else None)

        for g in range(N_GROUPS):
            s_st[g].append(new_qkv[:, g, 1:][:, None])
        s_st[3].append(conv_s)
        s_st[4].append(c_s)
        s_st[5].append(n_s)
        s_st[6].append(m_s[:, 0, :NH_B])
        s_st[7].append(jnp.concatenate([fbuf[:, 1:], u_s[:, None, :]], axis=1))

    outs = [xp.reshape(batch, seq, D_MODEL), xs.reshape(nreq, 1, D_MODEL)]
    for i in range(8):
        outs.append(jnp.stack(p_st[i], 0))
        outs.append(jnp.stack(s_st[i], 0))
    return tuple(outs)
```

```python
import functools
import math

import numpy as np
import jax
import jax.numpy as jnp
from jax import lax
from jax.experimental import pallas as pl
from jax.experimental.pallas import tpu as pltpu

F32 = jnp.float32
BF16 = jnp.bfloat16
HIGHEST = lax.Precision.HIGHEST

D_MODEL = 1024
WINDOWS = (128, 512, 2048)
DILATIONS = (1, 4, 16)
N_GROUPS = 3
H_G = 8
DH_A = 64
A_GROUP = H_G * DH_A
A_QKV = N_GROUPS * A_GROUP
Q_BLOCK = 128
SPAN = 128
NH_B = 4
DK_B = 256
B_WIDTH = NH_B * DK_B
CONV_B = 4
D_FF = 2816
CONV_F = 3
NUM_BUCKETS = 32
MAX_DISTANCE = 2048
RMS_EPS = 1e-6
NEG = -1e30

LANES = 128
SUBLANES = 8
FF_CHUNK = 256
FF_AHEAD = 3
MLSTM_CHUNK = 256
MLSTM_SEQS = 2
VMEM_LIMIT = 56 * 1024 * 1024

COLS_QKV_G = 3 * A_GROUP
COLS_QKV = N_GROUPS * COLS_QKV_G
COLS_QK = 2 * B_WIDTH
COLS_VO = 2 * B_WIDTH
COLS_GAB = 2 * D_MODEL
GATE_COL0 = COLS_QKV + COLS_QK + COLS_VO


def _dot(a, b):
    return jnp.dot(a, b, preferred_element_type=F32)


def _hdot(a, b):
    return jnp.dot(a, b, precision=HIGHEST, preferred_element_type=F32)


def _dot_nt(a, b):
    return lax.dot_general(a, b, (((1,), (1,)), ((), ())), preferred_element_type=F32)


def _dot_tn(a, b):
    return lax.dot_general(a, b, (((0,), (0,)), ((), ())), preferred_element_type=F32)


def _split3(x):
    hi = x.astype(BF16)
    rest = x - hi.astype(F32)
    mid = rest.astype(BF16)
    return hi, mid, (rest - mid.astype(F32)).astype(BF16)


def _sigmoid(x):
    return 1.0 / (1.0 + jnp.exp(-x))


def _log_sigmoid(x):
    return jnp.minimum(x, 0.0) - jnp.log1p(jnp.exp(-jnp.abs(x)))


def _gelu_tanh(x):
    return 0.5 * x * (1.0 + jnp.tanh(math.sqrt(2.0 / math.pi) * (x + 0.044715 * (x * x * x))))


def _rms(x, g):
    return x * lax.rsqrt(jnp.mean(x * x, axis=-1, keepdims=True) + RMS_EPS) * g


def _resident(shape):
    nd = len(shape)
    return pl.BlockSpec(shape, lambda *_: (0,) * nd, pipeline_mode=pl.Buffered(1))


def _params(n_grid):
    return pltpu.CompilerParams(dimension_semantics=("arbitrary",) * n_grid,
                                vmem_limit_bytes=VMEM_LIMIT)


def _inproj_kernel(*refs, prompt, tiles_per_batch):
    if prompt:
        (x_ref, g_ref, w_ref, wg_ref, cw_ref, cb_ref, qkv0_ref, qkv1_ref, qkv2_ref, qk_ref, vo_ref,
         gab_ref, gcol_ref, grow_ref, ctail_ref, kt_ref, hs_ref, ext_ref) = refs
    else:
        (x_ref, g_ref, w_ref, wg_ref, qkv0_ref, qkv1_ref, qkv2_ref, qk_ref, vo_ref, gab_ref,
         gcol_ref) = refs
    if prompt:
        @pl.when(pl.program_id(0) % tiles_per_batch == 0)
        def _():
            ext_ref[:, 0:SUBLANES, :] = jnp.zeros((ext_ref.shape[0], SUBLANES, LANES), F32)

    hf = _rms(x_ref[...], g_ref[...])
    h = hf.astype(BF16)
    tm = hf.shape[0]

    def project(lhs, col0, width, store):
        for c in range(0, width, 512):
            store(c, _dot(lhs, w_ref[:, col0 + c:col0 + c + 512]))

    def store_rows(ref):
        def store(c, res):
            ref[:, c:c + 512] = res.astype(ref.dtype)
        return store

    if prompt:
        pad = SUBLANES

        def store_slabs(c, res):
            for k in range(512 // LANES):
                ext_ref[c // LANES + k, pad:pad + tm, :] = res[:, k * LANES:(k + 1) * LANES]
        project(h, COLS_QKV, COLS_QK, store_slabs)
    project(h, 0, COLS_QKV_G, store_rows(qkv0_ref))
    if prompt:
        n_slab = hs_ref.shape[0]
        for k in range(n_slab):
            hs_ref[k] = hf[:, k * LANES:(k + 1) * LANES]
        for g, ref in ((1, qkv1_ref), (2, qkv2_ref)):
            d = DILATIONS[g]
            rows = tm // d
            hp = jnp.concatenate(
                [jnp.concatenate([hs_ref[k, pl.ds(r, rows, stride=d), :] for r in range(d)], axis=0)
                 for k in range(n_slab)], axis=1).astype(BF16)

            def store(c, res, ref=ref, d=d, rows=rows):
                res = res.astype(ref.dtype)
                for r in range(d):
                    ref[r, :, c:c + 512] = res[r * rows:(r + 1) * rows]
            project(hp, g * COLS_QKV_G, COLS_QKV_G, store)
    else:
        project(h, COLS_QKV_G, COLS_QKV_G, store_rows(qkv1_ref))
        project(h, 2 * COLS_QKV_G, COLS_QKV_G, store_rows(qkv2_ref))
    if prompt:
        for k in range(COLS_QK // LANES):
            ks = slice(k * LANES, (k + 1) * LANES)
            y = cb_ref[:, ks] + ext_ref[k, pad:pad + tm, :] * cw_ref[CONV_B - 1:CONV_B, ks]
            for i in range(CONV_B - 1):
                lo = pad - (CONV_B - 1) + i
                y = y + ext_ref[k, lo:lo + tm, :] * cw_ref[i:i + 1, ks]
            y = y * _sigmoid(y)
            if k * LANES < B_WIDTH:
                qk_ref[:, ks] = y.astype(qk_ref.dtype)
            else:
                kt_ref[k * LANES - B_WIDTH:(k + 1) * LANES - B_WIDTH, :] = (
                    (y * (DK_B ** -0.5)).T.astype(kt_ref.dtype))
            tail = ext_ref[k, tm:tm + pad, :]
            ctail_ref[:, ks] = tail
            ext_ref[k, 0:pad, :] = tail
    else:
        project(h, COLS_QKV, COLS_QK, store_rows(qk_ref))
    col = COLS_QKV + COLS_QK
    for ref in (vo_ref, gab_ref):
        project(h, col, ref.shape[-1], store_rows(ref))
        col += ref.shape[-1]
    gates = _dot(h, wg_ref[...])
    gcol_ref[...] = gates
    if prompt:
        grow_ref[...] = gates.T[:SUBLANES, :]


def _inproj(x, gain, w_main, w_gate, *, tm, act_dtype, seq=None, conv=None):
    n = x.shape[0]
    dilate = seq is not None
    row = lambda i: (i, 0)
    out_shape = [jax.ShapeDtypeStruct((n, COLS_QKV_G), act_dtype)]
    out_specs = [pl.BlockSpec((tm, COLS_QKV_G), row)]
    for g in (1, 2):
        if dilate:
            d = DILATIONS[g]
            tpb = seq // tm
            out_shape.append(jax.ShapeDtypeStruct((n // seq, d, seq // d, COLS_QKV_G), act_dtype))
            out_specs.append(pl.BlockSpec((None, d, tm // d, COLS_QKV_G),
                                          lambda i, tpb=tpb: (i // tpb, 0, i % tpb, 0)))
        else:
            out_shape.append(jax.ShapeDtypeStruct((n, COLS_QKV_G), act_dtype))
            out_specs.append(pl.BlockSpec((tm, COLS_QKV_G), row))
    for cols, dt in ((B_WIDTH, act_dtype) if dilate else (COLS_QK, F32), (COLS_VO, act_dtype),
                     (COLS_GAB, act_dtype), (LANES, F32)):
        out_shape.append(jax.ShapeDtypeStruct((n, cols), dt))
        out_specs.append(pl.BlockSpec((tm, cols), row))
    in_specs = [
        pl.BlockSpec((tm, D_MODEL), row),
        _resident((1, D_MODEL)),
        _resident(w_main.shape),
        _resident(w_gate.shape),
    ]
    args = [x, gain, w_main, w_gate]
    scratch = []
    tpb = None
    if dilate:
        tpb = seq // tm
        in_specs += [_resident(conv[0].shape), _resident(conv[1].shape)]
        args += list(conv)
        out_shape.append(jax.ShapeDtypeStruct((SUBLANES, n), F32))
        out_specs.append(pl.BlockSpec((SUBLANES, tm), lambda i: (0, i)))
        out_shape.append(jax.ShapeDtypeStruct((n // seq, SUBLANES, COLS_QK), F32))
        out_specs.append(pl.BlockSpec((None, SUBLANES, COLS_QK), lambda i: (i // tpb, 0, 0)))
        out_shape.append(jax.ShapeDtypeStruct((B_WIDTH, n), act_dtype))
        out_specs.append(pl.BlockSpec((B_WIDTH, tm), lambda i: (0, i)))
        scratch =[pltpu.VMEM((D_MODEL // LANES, tm, LANES), F32),
                   pltpu.VMEM((COLS_QK // LANES, tm + SUBLANES, LANES), F32)]
    return pl.pallas_call(
        functools.partial(_inproj_kernel, prompt=dilate, tiles_per_batch=tpb),
        grid=(n // tm,),
        in_specs=in_specs,
        out_specs=out_specs,
        out_shape=out_shape,
        scratch_shapes=scratch,
        compiler_params=_params(1),
        name="inproj",
    )(*args)


def _attn_kernel(q_ref, kp_ref, kc_ref, vp_ref, vc_ref, bias_ref, o_ref, lse_ref):
    n_blk = q_ref.shape[0] // Q_BLOCK
    has_prev = pl.program_id(1) > 0
    key_lane = lax.broadcasted_iota(jnp.int32, (1, 1, 2 * Q_BLOCK), 2)
    first_mask = jnp.where((key_lane < Q_BLOCK) & jnp.logical_not(has_prev), NEG, 0.0)
    lane = lax.broadcasted_iota(jnp.int32, (Q_BLOCK, LANES), 1)
    low_half = lane < DH_A
    pair = 2 * DH_A

    def window(i, cur_ref, first_ref):
        rows = slice(i * Q_BLOCK, (i + 1) * Q_BLOCK)
        prev_ref, prows = ((first_ref, slice(0, Q_BLOCK)) if i == 0
                           else (cur_ref, slice((i - 1) * Q_BLOCK, i * Q_BLOCK)))
        return [jnp.concatenate([prev_ref[prows, hp * pair:(hp + 1) * pair],
                                 cur_ref[rows, hp * pair:(hp + 1) * pair]], axis=0)
                for hp in range(H_G // 2)]

    def qk_scores(i):
        rows = slice(i * Q_BLOCK, (i + 1) * Q_BLOCK)
        scores = []
        for hp, kk in enumerate(window(i, kc_ref, kp_ref)):
            qp = q_ref[rows, hp * pair:(hp + 1) * pair]
            scores.append(_dot_nt(jnp.where(low_half, qp, jnp.zeros_like(qp)), kk))
            scores.append(_dot_nt(jnp.where(low_half, jnp.zeros_like(qp), qp), kk))
        return jnp.stack(scores)

    for i in range(n_blk):
        rows = slice(i * Q_BLOCK, (i + 1) * Q_BLOCK)
        s = qk_scores(i) + bias_ref[...]
        if i == 0:
            s = s + first_mask
        m = jnp.max(s, axis=-1, keepdims=True)
        p = jnp.exp(s - m)
        l = jnp.sum(p, axis=-1, keepdims=True)
        pb = p.astype(BF16)
        inv = 1.0 / l
        lse = m + jnp.log(l)
        lse_all = jnp.zeros((Q_BLOCK, LANES), F32)
        for hp, vv in enumerate(window(i, vc_ref, vp_ref)):
            cols = slice(hp * pair, (hp + 1) * pair)
            o_lo = _dot(pb[2 * hp], vv) * inv[2 * hp]
            o_hi = _dot(pb[2 * hp + 1], vv) * inv[2 * hp + 1]
            o_ref[rows, cols] = jnp.where(low_half, o_lo, o_hi).astype(o_ref.dtype)
        for h in range(H_G):
            lse_all = jnp.where(lane == h, lse[h], lse_all)
        lse_ref[rows, :] = lse_all


ATTN_BLOCKS = 8


def _attn_prompt(qkv, bias):
    nsub, u_len, _ = qkv.shape
    n_blk = math.gcd(ATTN_BLOCKS, u_len // Q_BLOCK)
    rows = n_blk * Q_BLOCK
    nb = u_len // rows

    def spec(col_block, prev):
        if prev:
            return pl.BlockSpec((None, Q_BLOCK, A_GROUP),
                                lambda s, j: (s, jnp.maximum(j * n_blk - 1, 0), col_block))
        return pl.BlockSpec((None, rows, A_GROUP), lambda s, j: (s, j, col_block))

    return pl.pallas_call(
        _attn_kernel,
        grid=(nsub, nb),
        in_specs=[spec(0, False), spec(1, True), spec(1, False), spec(2, True), spec(2, False),
                  _resident(bias.shape)],
        out_specs=[
            pl.BlockSpec((None, rows, A_GROUP), lambda s, j: (s, j, 0)),
            pl.BlockSpec((None, rows, LANES), lambda s, j: (s, j, 0)),
        ],
        out_shape=[
            jax.ShapeDtypeStruct((nsub, u_len, A_GROUP), BF16),
            jax.ShapeDtypeStruct((nsub, u_len, LANES), F32),
        ],
        compiler_params=_params(2),
        name="attn_prompt",
    )(qkv, qkv, qkv, qkv, qkv, bias)


def _kvtail_kernel(*refs):
    out_ref, nat_s = refs[-2:]
    srcs = refs[:-2]
    n_slab = nat_s.shape[0]
    for layer in range(len(srcs) // 2):
        @pl.when(pl.program_id(0) == layer)
        def _():
            for j, ref in enumerate(srcs[2 * layer:2 * layer + 2]):
                dil = ref.shape[0]
                for r in range(dil):
                    val = ref[r].astype(F32)
                    for s in range(n_slab):
                        nat_s[s, pl.ds(r, SPAN, stride=dil), :] = val[:, s * LANES:(s + 1) * LANES]
                for s in range(n_slab):
                    out_ref[j, s * LANES:(s + 1) * LANES, :] = nat_s[s].T


def _kv_tail(qkv_layers, g):
    depth = len(qkv_layers)
    batch, dil, u_len, _ = qkv_layers[0].shape
    keep = SPAN * dil
    last_blk = u_len // SPAN - 1
    specs, args = [], []
    for arr in qkv_layers:
        for col in (1, 2):
            specs.append(pl.BlockSpec((None, dil, SPAN, A_GROUP),
                                      lambda l, b, col=col: (b, 0, last_blk, col)))
            args.append(arr)
    return pl.pallas_call(
        _kvtail_kernel,
        grid=(depth, batch),
        in_specs=specs,
        out_specs=pl.BlockSpec((None, None, 2, A_GROUP, keep), lambda l, b: (l, b, 0, 0, 0)),
        out_shape=jax.ShapeDtypeStruct((depth, batch, 2, A_GROUP, keep), F32),
        scratch_shapes=[pltpu.VMEM((A_GROUP // LANES, keep, LANES), F32)],
        compiler_params=_params(2),
        name=f"kv_tail_g{g}",
    )(*args)


def _sattn_kernel(new_ref, kv0_ref, kv1_ref, kv2_ref, bt0_ref, bt1_ref, bt2_ref, b0_ref,
                  o_ref, lse_ref):
    eye = (lax.broadcasted_iota(jnp.int32, (DH_A, DH_A), 0)
           == lax.broadcasted_iota(jnp.int32, (DH_A, DH_A), 1))
    for g, (kv_ref, bt_ref) in enumerate(((kv0_ref, bt0_ref), (kv1_ref, bt1_ref),
                                          (kv2_ref, bt2_ref))):
        q = new_ref[g, 0]
        k_new = new_ref[g, 1]
        v_new = new_ref[g, 2]
        s_rows = []
        for h in range(H_G):
            q_col = jnp.sum(jnp.where(eye, q[h:h + 1, :], 0.0), axis=1, keepdims=True)
            s_rows.append(jnp.sum(kv_ref[0, h] * q_col, axis=0, keepdims=True))
        s = jnp.concatenate(s_rows, axis=0) + bt_ref[...]
        s0 = jnp.sum(k_new * q, axis=1, keepdims=True) + b0_ref[g]
        m = jnp.maximum(jnp.max(s, axis=1, keepdims=True), s0)
        p = jnp.exp(s - m[:, 0:1])
        p0 = jnp.exp(s0 - m)
        l = jnp.sum(p, axis=1, keepdims=True) + p0
        o_rows = []
        for h in range(H_G):
            o_col = jnp.sum(kv_ref[1, h] * p[h:h + 1, :], axis=1, keepdims=True)
            o_rows.append(jnp.sum(jnp.where(eye, o_col, 0.0), axis=0, keepdims=True))
        o_ref[g] = (jnp.concatenate(o_rows, axis=0) + p0 * v_new) / l
        lse_ref[g] = m + jnp.log(l)


def _attn_sample(new_qkv, caches, layer, bias_t, bias_0):
    nreq = new_qkv.shape[0]
    views = []
    specs = []
    for g, cache in enumerate(caches):
        n_buf = cache.shape[2]
        assert n_buf == SPAN * DILATIONS[g], "cache must hold exactly one window"
        views.append(cache.transpose(0, 1, 3, 4, 5, 2))
        specs.append(pl.BlockSpec((None, None, 2, H_G, DH_A, n_buf),
                                  lambda b, layer=layer: (layer, b, 0, 0, 0, 0)))
    out = jax.ShapeDtypeStruct((nreq, N_GROUPS, H_G, DH_A), F32)
    out_spec = pl.BlockSpec((None, N_GROUPS, H_G, DH_A), lambda b: (b, 0, 0, 0))
    return pl.pallas_call(
        _sattn_kernel,
        grid=(nreq,),
        in_specs=[pl.BlockSpec((None, N_GROUPS, 3, H_G, DH_A), lambda b: (b, 0, 0, 0, 0))] + specs
        + [_resident(t.shape) for t in bias_t] + [_resident(bias_0.shape)],
        out_specs=[out_spec, out_spec],
        out_shape=[out, out],
        compiler_params=_params(1),
        name="attn_sample",
    )(new_qkv, *views, *bias_t, bias_0)


def _mlstm_kernel(*refs, n_seq):
    q_ref = refs[0]
    kt_refs = refs[1:1 + n_seq]
    vo_ref, gcol_ref = refs[1 + n_seq:3 + n_seq]
    grow_refs = refs[3 + n_seq:3 + 2 * n_seq]
    gb_row_ref, gb_col_ref, bout_ref, c_ref, n_ref, m_ref, nrep_s = refs[3 + 2 * n_seq:]
    chunk = q_ref.shape[1]

    @pl.when(pl.program_id(1) == 0)
    def _():
        c_ref[...] = jnp.zeros_like(c_ref)
        nrep_s[...] = jnp.zeros_like(nrep_s)
        m_ref[...] = jnp.zeros_like(m_ref)

    ti = lax.broadcasted_iota(jnp.int32, (chunk, chunk), 0)
    si = lax.broadcasted_iota(jnp.int32, (chunk, chunk), 1)
    causal = ti >= si
    lower = causal.astype(BF16)
    upper = (si >= ti).astype(BF16)
    items = [(j, h) for j in range(n_seq) for h in range(NH_B)]
    hcol = lambda h: slice(h * DK_B, (h + 1) * DK_B)
    a_rows, b_ts, b_lasts = [], [], []
    for j in range(n_seq):
        z_col = gcol_ref[j] + gb_row_ref[...]
        z_row = grow_refs[j][...] + gb_col_ref[...]
        b_col = sum(_dot(lower, part) for part in _split3(_log_sigmoid(z_col)))
        b_row = sum(_dot(part, upper) for part in _split3(_log_sigmoid(z_row)))
        for h in range(NH_B):
            a_rows.append(z_row[h:h + 1, :] - b_row[NH_B + h:NH_B + h + 1, :])
            b_ts.append(b_col[:, NH_B + h:NH_B + h + 1])
            b_lasts.append(b_row[NH_B + h:NH_B + h + 1, chunk - 1:chunk])
    a_row = jnp.stack(a_rows)
    b_t = jnp.stack(b_ts)
    m_prev = jnp.stack([m_ref[j, :, h:h + 1] for j, h in items])
    a_mat = jnp.where(causal, a_row, NEG)
    gmax = jnp.maximum(m_prev, jnp.max(a_mat, axis=-1, keepdims=True))
    dw = jnp.exp(a_mat - gmax)
    iw = jnp.exp(m_prev - gmax)
    g_last = gmax[:, chunk - 1:chunk, :]
    decay = jnp.exp(m_prev - g_last)
    w_state = jnp.exp(a_row - g_last)

    qb = [q_ref[j, :, hcol(h)] for j, h in items]
    kt = [kt_refs[j][hcol(h), :] for j, h in items]
    vb = [vo_ref[j, :, hcol(h)] for j, h in items]
    c_old = [c_ref[j, h] for j, h in items]
    n_old = [nrep_s[j, h] for j, h in items]
    idx = range(len(items))
    qk = jnp.stack([_dot(qb[i], kt[i]) for i in idx]) * dw
    qkb = qk.astype(BF16)
    q_c = jnp.stack([_dot(qb[i], c_old[i].astype(BF16)) for i in idx])
    qk_v = jnp.stack([_dot(qkb[i], vb[i]) for i in idx])
    num = iw * q_c + qk_v
    q_n = jnp.stack([_dot(qb[i], n_old[i].astype(BF16))[:, 0:1] for i in idx])
    den = iw * q_n + jnp.sum(qk, axis=-1, keepdims=True)
    hid = num / jnp.maximum(jnp.abs(den), jnp.exp(-(b_t + gmax)))
    for i, (j, h) in enumerate(items):
        o_gate = _sigmoid(vo_ref[j, :, B_WIDTH + h * DK_B:B_WIDTH + (h + 1) * DK_B].astype(F32))
        bout_ref[j, :, hcol(h)] = (o_gate * hid[i]).astype(bout_ref.dtype)

    kwt = (jnp.stack(kt).astype(F32) * w_state).astype(BF16)
    ones = jnp.ones((chunk, LANES), BF16)
    for i, (j, h) in enumerate(items):
        c_ref[j, h] = decay[i] * c_old[i] + _dot(kwt[i], vb[i])
        nrep_s[j, h] = decay[i] * n_old[i] + _dot(kwt[i], ones)
        m_ref[j, :, h:h + 1] = b_lasts[i] + g_last[i]

    @pl.when(pl.program_id(1) == pl.num_programs(1) - 1)
    def _():
        for j, h in items:
            n_ref[j, h:h + 1, :] = nrep_s[j, h].T[0:1, :]


def _mlstm_prompt(q, kt, vo, gcol, grow, gb_row, gb_col, batch, seq):
    chunk = MLSTM_CHUNK
    n_seq = math.gcd(MLSTM_SEQS, batch)
    nc = seq // chunk
    seq3 = lambda b, c: (b, c, 0)
    lanes = [lambda b, c, j=j: (0, (b * n_seq + j) * nc + c) for j in range(n_seq)]
    state = lambda b, c: (b, 0, 0)
    return pl.pallas_call(
        functools.partial(_mlstm_kernel, n_seq=n_seq),
        grid=(batch // n_seq, nc),
        in_specs=[pl.BlockSpec((n_seq, chunk, B_WIDTH), seq3)]
        + [pl.BlockSpec((B_WIDTH, chunk), lanes[j]) for j in range(n_seq)]
        + [pl.BlockSpec((n_seq, chunk, COLS_VO), seq3), pl.BlockSpec((n_seq, chunk, LANES), seq3)]
        + [pl.BlockSpec((SUBLANES, chunk), lanes[j]) for j in range(n_seq)]
        + [_resident(gb_row.shape), _resident(gb_col.shape)],
        out_specs=[
            pl.BlockSpec((n_seq, chunk, B_WIDTH), seq3),
            pl.BlockSpec((n_seq, NH_B, DK_B, DK_B), lambda b, c: (b, 0, 0, 0)),
            pl.BlockSpec((n_seq, NH_B, DK_B), state),
            pl.BlockSpec((n_seq, 1, LANES), state),
        ],
        out_shape=[
            jax.ShapeDtypeStruct((batch, seq, B_WIDTH), BF16),
            jax.ShapeDtypeStruct((batch, NH_B, DK_B, DK_B), F32),
            jax.ShapeDtypeStruct((batch, NH_B, DK_B), F32),
            jax.ShapeDtypeStruct((batch, 1, LANES), F32),
        ],
        scratch_shapes=[pltpu.VMEM((n_seq, NH_B, DK_B, LANES), F32)],
        compiler_params=_params(2),
        name="mlstm_prompt",
    )(q.reshape(batch, seq, B_WIDTH), *([kt] * n_seq), vo.reshape(batch, seq, COLS_VO),
      gcol.reshape(batch, seq, LANES), *([grow] * n_seq), gb_row, gb_col)


def _smlstm_kernel(qk_ref, vo_ref, g_ref, gb_ref, cs_ref, cw_ref, cb_ref, c0_ref, n0_ref, m0_ref,
                   bout_ref, c1_ref, n1_ref, m1_ref, ncs_ref):
    u = qk_ref[...]
    y = cb_ref[...] + u * cw_ref[CONV_B - 1:CONV_B, :]
    for i in range(CONV_B - 1):
        y = y + cs_ref[i:i + 1, :] * cw_ref[i:i + 1, :]
    ncs_ref[0:CONV_B - 2, :] = cs_ref[1:CONV_B - 1, :]
    ncs_ref[CONV_B - 2:CONV_B - 1, :] = u
    y = y * _sigmoid(y)
    qf = y[:, :B_WIDTH]
    kf = y[:, B_WIDTH:] * (DK_B ** -0.5)
    z = g_ref[...] + gb_ref[...]
    lf_all = _log_sigmoid(z)
    m0 = m0_ref[...]
    eye = (lax.broadcasted_iota(jnp.int32, (DK_B, DK_B), 0)
           == lax.broadcasted_iota(jnp.int32, (DK_B, DK_B), 1))
    for h in range(NH_B):
        hs = slice(h * DK_B, (h + 1) * DK_B)
        ig = z[:, h:h + 1]
        inter = lf_all[:, NH_B + h:NH_B + h + 1] + m0[:, h:h + 1]
        m_t = jnp.maximum(inter, ig)
        dw = jnp.exp(ig - m_t)
        iw = jnp.exp(inter - m_t)
        qh = qf[:, hs]
        kh = kf[:, hs]
        vh = vo_ref[:, hs]
        c_old = c0_ref[h]
        n_old = n0_ref[h:h + 1, :]
        qk = jnp.sum(qh * kh, axis=1, keepdims=True) * dw
        num = iw * _hdot(qh, c_old) + qk * vh
        den = iw * jnp.sum(qh * n_old, axis=1, keepdims=True) + qk
        hid = num / jnp.maximum(jnp.abs(den), jnp.exp(-m_t))
        o_gate = _sigmoid(vo_ref[:, B_WIDTH + h * DK_B:B_WIDTH + (h + 1) * DK_B])
        bout_ref[:, hs] = o_gate * hid
        k_col = jnp.sum(jnp.where(eye, kh, 0.0), axis=1, keepdims=True)
        c1_ref[h] = iw * c_old + (dw * k_col) * vh
        n1_ref[h:h + 1, :] = iw * n_old + dw * kh
        m1_ref[:, h:h + 1] = m_t


def _mlstm_sample(qk_pre, vo, gcol, gb_row, conv_state, cw, cb, c0, n0, m0, layer):
    nreq = qk_pre.shape[0]
    one = lambda b: (b, 0, 0)
    lay3 = lambda b, layer=layer: (layer, b, 0, 0)
    lay4 = lambda b, layer=layer: (layer, b, 0, 0, 0)
    return pl.pallas_call(
        _smlstm_kernel,
        grid=(nreq,),
        in_specs=[
            pl.BlockSpec((None, 1, COLS_QK), one),
            pl.BlockSpec((None, 1, COLS_VO), one),
            pl.BlockSpec((None, 1, LANES), one),
            _resident(gb_row.shape),
            pl.BlockSpec((None, None, CONV_B - 1, COLS_QK), lay3),
            _resident(cw.shape), _resident(cb.shape),
            pl.BlockSpec((None, None, NH_B, DK_B, DK_B), lay4),
            pl.BlockSpec((None, None, NH_B, DK_B), lay3),
            pl.BlockSpec((None, None, 1, LANES), lay3),
        ],
        out_specs=[
            pl.BlockSpec((None, 1, B_WIDTH), one),
            pl.BlockSpec((None, NH_B, DK_B, DK_B), lambda b: (b, 0, 0, 0)),
            pl.BlockSpec((None, NH_B, DK_B), one),
            pl.BlockSpec((None, 1, LANES), one),
            pl.BlockSpec((None, CONV_B - 1, COLS_QK), one),
        ],
        out_shape=[
            jax.ShapeDtypeStruct((nreq, 1, B_WIDTH), F32),
            jax.ShapeDtypeStruct((nreq, NH_B, DK_B, DK_B), F32),
            jax.ShapeDtypeStruct((nreq, NH_B, DK_B), F32),
            jax.ShapeDtypeStruct((nreq, 1, LANES), F32),
            jax.ShapeDtypeStruct((nreq, CONV_B - 1, COLS_QK), F32),
        ],
        compiler_params=_params(1),
        name="mlstm_sample",
    )(qk_pre.reshape(nreq, 1, COLS_QK), vo.reshape(nreq, 1, COLS_VO),
      gcol.reshape(nreq, 1, LANES), gb_row, conv_state, cw, cb, c0, n0, m0)


def _merge_kernel(o0_ref, o1_ref, o2_ref, l0_ref, l1_ref, l2_ref, bo_ref, gab_ref, x_ref,
                  wpa_ref, wpb_ref, wo_ref, g2_ref, e_ref, x2_ref, h2_ref, *scratch):
    tm = x_ref.shape[0]

    def natural(ref, buf):
        dil, rows, _ = ref.shape
        n_slab = buf.shape[0]
        for r in range(dil):
            val = ref[r].astype(F32)
            for k in range(n_slab):
                buf[k, pl.ds(r, rows, stride=dil), :] = val[:, k * LANES:(k + 1) * LANES]
        return jnp.concatenate([buf[k] for k in range(n_slab)], axis=1)

    if scratch:
        os_ = [o0_ref[...].astype(F32), natural(o1_ref, scratch[0]), natural(o2_ref, scratch[1])]
        lses = [l0_ref[...], natural(l1_ref, scratch[2]), natural(l2_ref, scratch[3])]
    else:
        os_ = [r[...].astype(F32) for r in (o0_ref, o1_ref, o2_ref)]
        lses = [r[...] for r in (l0_ref, l1_ref, l2_ref)]
    top = jnp.maximum(jnp.maximum(lses[0], lses[1]), lses[2])
    ws = [jnp.exp(l - top) for l in lses]
    inv = 1.0 / (ws[0] + ws[1] + ws[2])
    e2 = e_ref[...]
    a = None
    for w, o in zip(ws, os_):
        alpha = w * inv
        hi = alpha.astype(BF16)
        lo = (alpha - hi.astype(F32)).astype(BF16)
        term = _dot(jnp.concatenate([hi, lo], axis=1), e2) * o
        a = term if a is None else a + term
    pa = _dot(a.astype(BF16), wpa_ref[...])
    pb = _dot(bo_ref[...].astype(BF16), wpb_ref[...])
    merged = (_sigmoid(gab_ref[:, :D_MODEL].astype(F32)) * pa
              + _sigmoid(gab_ref[:, D_MODEL:].astype(F32)) * pb)
    x2 = x_ref[...] + _dot(merged.astype(BF16), wo_ref[...])
    x2_ref[...] = x2
    h2_ref[...] = _rms(x2, g2_ref[...]).astype(BF16)


def _merge(os_, lses, bout, gab, x, wpa, wpb, wo, g2, e_bf, *, tm, seq=None):
    n = x.shape[0]
    row = lambda i: (i, 0)

    def group_specs(cols):
        specs = [pl.BlockSpec((tm, cols), row)]
        for g in (1, 2):
            if seq is None:
                specs.append(pl.BlockSpec((tm, cols), row))
            else:
                d = DILATIONS[g]
                tpb = seq // tm
                specs.append(pl.BlockSpec((None, d, tm // d, cols),
                                          lambda i, tpb=tpb: (i // tpb, 0, i % tpb, 0)))
        return specs

    scratch = []
    if seq is not None:
        scratch = ([pltpu.VMEM((A_GROUP // LANES, tm, LANES), F32)] * 2
                   + [pltpu.VMEM((1, tm, LANES), F32)] * 2)
    return pl.pallas_call(
        _merge_kernel,
        grid=(n // tm,),
        scratch_shapes=scratch,
        in_specs=group_specs(A_GROUP) + group_specs(LANES) + [
            pl.BlockSpec((tm, B_WIDTH), row),
            pl.BlockSpec((tm, COLS_GAB), row),
            pl.BlockSpec((tm, D_MODEL), row),
            _resident(wpa.shape), _resident(wpb.shape), _resident(wo.shape),
            _resident(g2.shape), _resident(e_bf.shape),
        ],
        out_specs=[pl.BlockSpec((tm, D_MODEL), row), pl.BlockSpec((tm, D_MODEL), row)],
        out_shape=[jax.ShapeDtypeStruct((n, D_MODEL), F32), jax.ShapeDtypeStruct((n, D_MODEL), BF16)],
        compiler_params=_params(1),
        name="merge",
    )(*os_, *lses, bout, gab, x, wpa, wpb, wo, g2, e_bf)


def _ffn_kernel(*refs, tiles_per_batch, from_state, final_norm):
    h2_ref, x2_ref, wup_ref, wdn_ref, cw_ref, cb_ref = refs[:6]
    pos = 6
    if from_state:
        prev_refs = refs[pos:pos + CONV_F - 1]
        pos += CONV_F - 1
    if final_norm:
        fg_ref = refs[pos]
        pos += 1
    x3_ref, u_ref = refs[pos:pos + 2]
    tm = h2_ref.shape[0]
    pad = SUBLANES
    if not from_state:
        ext_s = refs[pos + 2]

        @pl.when(pl.program_id(0) % tiles_per_batch == 0)
        def _():
            ext_s[:, 0:pad, :] = jnp.zeros((ext_s.shape[0], pad, LANES), F32)

    h2 = h2_ref[...]
    acc = x2_ref[...]
    n_chunk = D_FF // FF_CHUNK

    def halves(c):
        return (c * FF_CHUNK, D_FF + c * FF_CHUNK)

    def up(c):
        us = []
        for off in halves(c):
            u = _dot(h2, wup_ref[:, off:off + FF_CHUNK])
            if from_state:
                u_ref[:, off:off + FF_CHUNK] = u
                us.append(u)
            else:
                for k in range(FF_CHUNK // LANES):
                    ext_s[off // LANES + k, pad:pad + tm, :] = u[:, k * LANES:(k + 1) * LANES]
        return us

    def conv(c, us):
        ys = []
        for j, off in enumerate(halves(c)):
            cs = slice(off, off + FF_CHUNK)
            if from_state:
                y = cb_ref[:, cs] + us[j] * cw_ref[CONV_F - 1:CONV_F, cs]
                for i in range(CONV_F - 1):
                    y = y + prev_refs[i][:, cs] * cw_ref[i:i + 1, cs]
            else:
                parts = []
                for k in range(FF_CHUNK // LANES):
                    slab = off // LANES + k
                    ks = slice(off + k * LANES, off + (k + 1) * LANES)
                    part = cb_ref[:, ks]
                    for i in range(CONV_F):
                        lo = pad - (CONV_F - 1) + i
                        part = part + ext_s[slab, lo:lo + tm, :] * cw_ref[i:i + 1, ks]
                    parts.append(part)
                    tail = ext_s[slab, tm:tm + pad, :]
                    u_ref[:, ks] = tail
                    ext_s[slab, 0:pad, :] = tail
                y = jnp.concatenate(parts, axis=1)
            ys.append(y)
        return ys

    ahead = 1 if from_state else FF_AHEAD
    pending = [up(c) for c in range(min(ahead, n_chunk))]
    for c in range(n_chunk):
        us = pending.pop(0)
        if c + ahead < n_chunk:
            pending.append(up(c + ahead))
        y_act, y_gate = conv(c, us)
        act = _gelu_tanh(y_act) * y_gate
        acc = acc + _dot(act.astype(BF16), wdn_ref[c * FF_CHUNK:(c + 1) * FF_CHUNK, :])
    if final_norm:
        acc = _rms(acc, fg_ref[...])
    x3_ref[...] = acc


def _ffn(h2, x2, wup, wdn, cw, cb, *, tm, tiles_per_batch=None, prev_rows=None, final_g=None):
    n = h2.shape[0]
    from_state = prev_rows is not None
    row = lambda i: (i, 0)
    in_specs = [
        pl.BlockSpec((tm, D_MODEL), row), pl.BlockSpec((tm, D_MODEL), row),
        _resident(wup.shape), _resident(wdn.shape), _resident(cw.shape), _resident(cb.shape),
    ]
    args = [h2, x2, wup, wdn, cw, cb]
    scratch = []
    if from_state:
        in_specs += [pl.BlockSpec((tm, 2 * D_FF), row)] * (CONV_F - 1)
        args += list(prev_rows)
        u_shape = jax.ShapeDtypeStruct((n, 2 * D_FF), F32)
        u_spec = pl.BlockSpec((tm, 2 * D_FF), row)
    else:
        n_batch = n // (tm * tiles_per_batch)
        u_shape = jax.ShapeDtypeStruct((n_batch, SUBLANES, 2 * D_FF), F32)
        u_spec = pl.BlockSpec((None, SUBLANES, 2 * D_FF), lambda i: (i // tiles_per_batch, 0, 0))
        scratch = [pltpu.VMEM((2 * D_FF // LANES, tm + SUBLANES, LANES), F32)]
    if final_g is not None:
        in_specs.append(_resident(final_g.shape))
        args.append(final_g)
    return pl.pallas_call(
        functools.partial(_ffn_kernel, tiles_per_batch=tiles_per_batch, from_state=from_state,
                          final_norm=final_g is not None),
        grid=(n // tm,),
        in_specs=in_specs,
        out_specs=[pl.BlockSpec((tm, D_MODEL), row), u_spec],
        out_shape=[jax.ShapeDtypeStruct((n, D_MODEL), F32), u_shape],
        scratch_shapes=scratch,
        compiler_params=_params(1),
        name="ffn",
    )(*args)


def _t5_bucket(dist):
    max_exact = NUM_BUCKETS // 2
    df = jnp.maximum(dist, 1).astype(F32)
    large = max_exact + (jnp.log(df / max_exact) / math.log(MAX_DISTANCE / max_exact)
                         * (NUM_BUCKETS - max_exact)).astype(jnp.int32)
    large = jnp.minimum(large, NUM_BUCKETS - 1)
    return jnp.where(dist < max_exact, dist, large)


def _bias_table(rel_bias, g, dist):
    bucket = _t5_bucket(jnp.asarray(dist, jnp.int32))
    table = rel_bias[:, g * H_G:(g + 1) * H_G].reshape((NUM_BUCKETS, H_G) + (1,) * bucket.ndim)
    ids = jnp.arange(NUM_BUCKETS).reshape((NUM_BUCKETS, 1) + (1,) * bucket.ndim)
    return jnp.sum(jnp.where(bucket[None, None] == ids, table, 0.0), axis=0)


def _prompt_bias(rel_bias, g):
    qi = np.arange(Q_BLOCK)[:, None]
    ki = np.arange(2 * Q_BLOCK)[None, :]
    rel = qi + Q_BLOCK - ki
    band = (rel >= 0) & (rel <= SPAN)
    bias = _bias_table(rel_bias, g, np.maximum(rel, 0) * DILATIONS[g])
    return jnp.where(band[None], bias, NEG)


def _sample_bias(rel_bias):
    tables, news = [], []
    for g in range(N_GROUPS):
        dil = DILATIONS[g]
        n_buf = SPAN * dil
        pos = np.arange(n_buf)
        bias = _bias_table(rel_bias, g, n_buf - pos)
        tables.append(jnp.where((pos % dil == 0)[None], bias, NEG))
        news.append(jnp.broadcast_to(_bias_table(rel_bias, g, np.zeros((1,), np.int32)), (H_G, DH_A)))
    return tables, jnp.stack(news)


def _head_indicator():
    e = np.zeros((LANES, A_GROUP), np.float32)
    for h in range(H_G):
        e[h, h * DH_A:(h + 1) * DH_A] = 1.0
    return e


def kernel(x_prompt, x_sample, cache_kv_w128, cache_kv_w512, cache_kv_w2048, state_mlstm_conv,
           state_mlstm_C, state_mlstm_n, state_mlstm_m, state_ffn_conv, rel_bias, norm1_g, w_in,
           mconv_w, mconv_b, mgate_b, w_pa, w_pb, w_o, norm2_g, w_up, fconv_w, fconv_b, w_down,
           final_norm_g):
    batch, seq, _ = x_prompt.shape
    nreq = x_sample.shape[0]
    depth = w_in.shape[0]
    n_p = batch * seq
    caches = (cache_kv_w128, cache_kv_w512, cache_kv_w2048)

    assert seq % (Q_BLOCK * DILATIONS[-1]) == 0 and seq >= WINDOWS[-1]
    e_bf = jnp.asarray(np.concatenate([_head_indicator()] * 2, axis=0), BF16)
    prompt_bias = [_prompt_bias(rel_bias, g) for g in range(N_GROUPS)]
    sbias_m, sbias_0 = _sample_bias(rel_bias)
    m0_all = jnp.pad(state_mlstm_m, ((0, 0), (0, 0), (0, LANES - NH_B)))[:, :, None, :]
    fg = final_norm_g.reshape(1, D_MODEL)

    xp = x_prompt.reshape(n_p, D_MODEL)
    xs = x_sample.reshape(nreq, D_MODEL)
    p_st = [[] for _ in range(8)]
    s_st = [[] for _ in range(8)]

    for l in range(depth):
        last = l == depth - 1
        w_l = w_in[l]
        qkv_scale = jnp.asarray([DH_A ** -0.5, 1.0, 1.0], F32).reshape(1, 3, 1, 1)
        w_qkv = (w_l[:, :COLS_QKV].reshape(D_MODEL, 3, N_GROUPS, A_GROUP) * qkv_scale
                 ).transpose(0, 2, 1, 3)
        w_main = jnp.concatenate([w_qkv.reshape(D_MODEL, COLS_QKV), w_l[:, COLS_QKV:GATE_COL0],
                                  w_l[:, GATE_COL0 + 2 * NH_B:]], axis=1).astype(BF16)
        w_gate = jnp.pad(w_l[:, GATE_COL0:GATE_COL0 + 2 * NH_B], ((0, 0), (0, LANES - 2 * NH_B))).astype(BF16)
        g1 = norm1_g[l].reshape(1, D_MODEL)
        g2 = norm2_g[l].reshape(1, D_MODEL)
        gate_bias = mgate_b[l].reshape(1, 2 * NH_B)
        gb_row = jnp.pad(gate_bias, ((0, 0), (0, LANES - 2 * NH_B)))
        gb_col = gate_bias.reshape(2 * NH_B, 1)
        mcw, mcb = mconv_w[l], mconv_b[l].reshape(1, COLS_QK)
        fcw, fcb = fconv_w[l], fconv_b[l].reshape(1, 2 * D_FF)
        wpa, wpb, wo = w_pa[l].astype(BF16), w_pb[l].astype(BF16), w_o[l].astype(BF16)
        wup, wdn = w_up[l].astype(BF16), w_down[l].astype(BF16)

        *qkvs, q_b, vo, gab, gcol, grow, conv_tail, kt_b = _inproj(
            xp, g1, w_main, w_gate, tm=256, act_dtype=BF16, seq=seq, conv=(mcw, mcb))
        os_, lses = [], []
        for g in range(N_GROUPS):
            d = DILATIONS[g]
            o_g, lse_g = _attn_prompt(qkvs[g].reshape(batch * d, seq // d, COLS_QKV_G), prompt_bias[g])
            shape = (n_p,) if g == 0 else (batch, d, seq // d)
            os_.append(o_g.reshape(shape + (A_GROUP,)))
            lses.append(lse_g.reshape(shape + (LANES,)))
        bout, c_p, n_p_state, m_p = _mlstm_prompt(q_b, kt_b, vo, gcol, grow, gb_row, gb_col,
                                                  batch, seq)
        x2, h2 = _merge(os_, lses, bout.reshape(n_p, B_WIDTH), gab, xp, wpa, wpb, wo, g2, e_bf,
                        tm=512, seq=seq)
        xp, u_tail = _ffn(h2, x2, wup, wdn, fcw, fcb, tm=256, tiles_per_batch=seq // 256,
                          final_g=fg if last else None)

        for g in range(N_GROUPS):
            p_st[g].append(qkvs[g].reshape(batch, DILATIONS[g], seq // DILATIONS[g], COLS_QKV_G))
        p_st[3].append(conv_tail[:, SUBLANES - (CONV_B - 1):])
        p_st[4].append(c_p)
        p_st[5].append(n_p_state)
        p_st[6].append(m_p[:, 0, :NH_B])
        p_st[7].append(u_tail[:, SUBLANES - (CONV_F - 1):])

        *qkvs_s, qk_s, vo_s, gab_s, gcol_s = _inproj(xs, g1, w_main, w_gate, tm=nreq, act_dtype=F32)
        new_qkv = jnp.stack(qkvs_s, axis=1).reshape(nreq, N_GROUPS, 3, H_G, DH_A)
        o_s, lse_s = _attn_sample(new_qkv, caches, l, sbias_m, sbias_0)
        bout_s, c_s, n_s, m_s, conv_s = _mlstm_sample(qk_s, vo_s, gcol_s, gb_row, state_mlstm_conv,
                                                      mcw, mcb, state_mlstm_C, state_mlstm_n, m0_all, l)
        lse_pad = jnp.pad(lse_s[..., 0], ((0, 0), (0, 0), (0, LANES - H_G)))
        x2_s, h2_s = _merge([o_s[:, g].reshape(nreq, A_GROUP) for g in range(N_GROUPS)],
                            [lse_pad[:, g] for g in range(N_GROUPS)],
                            bout_s.reshape(nreq, B_WIDTH), gab_s, xs, wpa, wpb, wo, g2, e_bf, tm=nreq)
        fbuf = state_ffn_conv[l]
        xs, u_s = _ffn(h2_s, x2_s, wup, wdn, fcw, fcb, tm=nreq,
                       prev_rows=[fbuf[:, i] for i in range(CONV_F - 1)],
                       final_g=fg if last else None)

        for g in range(N_GROUPS):
            s_st[g].append(new_qkv[:, g, 1:][:, None])
        s_st[3].append(conv_s)
        s_st[4].append(c_s)
        s_st[5].append(n_s)
        s_st[6].append(m_s[:, 0, :NH_B])
        s_st[7].append(jnp.concatenate([fbuf[:, 1:], u_s[:, None, :]], axis=1))

    outs = [xp.reshape(batch, seq, D_MODEL), xs.reshape(nreq, 1, D_MODEL)]
    for i in range(8):
        if i < N_GROUPS:
            kv = _kv_tail(p_st[i], i).reshape(depth, batch, 2, H_G, DH_A, SPAN * DILATIONS[i])
            outs.append(kv.transpose(0, 1, 5, 2, 3, 4))
        else:
            outs.append(jnp.stack(p_st[i], 0))
        outs.append(jnp.stack(s_st[i], 0))
    return tuple(outs)
```

```python
import functools
import math

import numpy as np
import jax
import jax.numpy as jnp
from jax import lax
from jax.experimental import pallas as pl
from jax.experimental.pallas import tpu as pltpu

F32 = jnp.float32
BF16 = jnp.bfloat16
HIGHEST = lax.Precision.HIGHEST

D_MODEL = 1024
WINDOWS = (128, 512, 2048)
DILATIONS = (1, 4, 16)
N_GROUPS = 3
H_G = 8
DH_A = 64
A_GROUP = H_G * DH_A
A_QKV = N_GROUPS * A_GROUP
Q_BLOCK = 128
SPAN = 128
NH_B = 4
DK_B = 256
B_WIDTH = NH_B * DK_B
CONV_B = 4
D_FF = 2816
CONV_F = 3
NUM_BUCKETS = 32
MAX_DISTANCE = 2048
RMS_EPS = 1e-6
NEG = -1e30

LANES = 128
SUBLANES = 8
FF_CHUNK = 256
FF_AHEAD = 3
MLSTM_CHUNK = 256
MLSTM_SEQS = 2
VMEM_LIMIT = 56 * 1024 * 1024

COLS_QKV_G = 3 * A_GROUP
COLS_QKV = N_GROUPS * COLS_QKV_G
COLS_QK = 2 * B_WIDTH
COLS_VO = 2 * B_WIDTH
COLS_GAB = 2 * D_MODEL
GATE_COL0 = COLS_QKV + COLS_QK + COLS_VO


def _dot(a, b):
    return jnp.dot(a, b, preferred_element_type=F32)


def _hdot(a, b):
    return jnp.dot(a, b, precision=HIGHEST, preferred_element_type=F32)


def _dot_nt(a, b):
    return lax.dot_general(a, b, (((1,), (1,)), ((), ())), preferred_element_type=F32)


def _dot_tn(a, b):
    return lax.dot_general(a, b, (((0,), (0,)), ((), ())), preferred_element_type=F32)


def _split3(x):
    hi = x.astype(BF16)
    rest = x - hi.astype(F32)
    mid = rest.astype(BF16)
    return hi, mid, (rest - mid.astype(F32)).astype(BF16)


def _sigmoid(x):
    return 1.0 / (1.0 + jnp.exp(-x))


def _log_sigmoid(x):
    return jnp.minimum(x, 0.0) - jnp.log1p(jnp.exp(-jnp.abs(x)))


def _gelu_tanh(x):
    return 0.5 * x * (1.0 + jnp.tanh(math.sqrt(2.0 / math.pi) * (x + 0.044715 * (x * x * x))))


def _rms(x, g):
    return x * lax.rsqrt(jnp.mean(x * x, axis=-1, keepdims=True) + RMS_EPS) * g


def _resident(shape):
    nd = len(shape)
    return pl.BlockSpec(shape, lambda *_: (0,) * nd, pipeline_mode=pl.Buffered(1))


def _params(n_grid):
    return pltpu.CompilerParams(dimension_semantics=("arbitrary",) * n_grid,
                                vmem_limit_bytes=VMEM_LIMIT)


W_BLOCK = 512
N_QKV_BLOCKS = COLS_QKV // W_BLOCK
N_HEAD_BLOCKS = GATE_COL0 // W_BLOCK


def _regroup_kernel(head_ref, tail_ref, gate_ref, o_ref, og_ref):
    j = pl.program_id(0)

    @pl.when(j == 0)
    def _():
        lane = lax.broadcasted_iota(jnp.int32, og_ref.shape, 1)
        og_ref[...] = jnp.where(lane < 2 * NH_B, gate_ref[...].T, 0.0).astype(og_ref.dtype)

    @pl.when(j < N_HEAD_BLOCKS)
    def _():
        is_q = (j < N_QKV_BLOCKS) & (j % 3 == 0)
        scale = jnp.where(is_q, DH_A ** -0.5, 1.0)
        o_ref[...] = (head_ref[...] * scale).T.astype(o_ref.dtype)

    @pl.when(j >= N_HEAD_BLOCKS)
    def _():
        o_ref[...] = tail_ref[...].T.astype(o_ref.dtype)


def _regroup_w_in(w):
    wt = w.T
    n_blocks = N_HEAD_BLOCKS + COLS_GAB // W_BLOCK
    tail0 = GATE_COL0 + 2 * NH_B
    assert GATE_COL0 % LANES == 0 and tail0 % SUBLANES == 0

    def head_block(j):
        regrouped = (j % 3) * N_GROUPS + j // 3
        return jnp.where(j < N_QKV_BLOCKS, regrouped, jnp.minimum(j, N_HEAD_BLOCKS - 1)), 0

    return pl.pallas_call(
        _regroup_kernel,
        grid=(n_blocks,),
        in_specs=[
            pl.BlockSpec((W_BLOCK, D_MODEL), head_block),
            pl.BlockSpec((pl.Element(W_BLOCK), pl.Element(D_MODEL)),
                         lambda j: (pl.multiple_of(
                             tail0 + jnp.maximum(j - N_HEAD_BLOCKS, 0) * W_BLOCK, SUBLANES), 0)),
            pl.BlockSpec((LANES, D_MODEL), lambda j: (GATE_COL0 // LANES, 0)),
        ],
        out_specs=[pl.BlockSpec((D_MODEL, W_BLOCK), lambda j: (0, j)),
                   pl.BlockSpec((D_MODEL, LANES), lambda j: (0, 0))],
        out_shape=[jax.ShapeDtypeStruct((D_MODEL, n_blocks * W_BLOCK), BF16),
                   jax.ShapeDtypeStruct((D_MODEL, LANES), BF16)],
        compiler_params=_params(1),
        name="regroup_w_in",
    )(wt, wt, wt)


def _inproj_kernel(*refs, prompt, tiles_per_batch):
    if prompt:
        (x_ref, g_ref, w_ref, wg_ref, cw_ref, cb_ref, qkv0_ref, qkv1_ref, qkv2_ref, qk_ref, vo_ref,
         gab_ref, gcol_ref, grow_ref, ctail_ref, kt_ref, hs_ref, ext_ref) = refs
    else:
        (x_ref, g_ref, w_ref, wg_ref, qkv0_ref, qkv1_ref, qkv2_ref, qk_ref, vo_ref, gab_ref,
         gcol_ref) = refs
    if prompt:
        @pl.when(pl.program_id(0) % tiles_per_batch == 0)
        def _():
            ext_ref[:, 0:SUBLANES, :] = jnp.zeros((ext_ref.shape[0], SUBLANES, LANES), F32)

    hf = _rms(x_ref[...], g_ref[...])
    h = hf.astype(BF16)
    tm = hf.shape[0]

    def project(lhs, col0, width, store):
        for c in range(0, width, 512):
            store(c, _dot(lhs, w_ref[:, col0 + c:col0 + c + 512]))

    def store_rows(ref):
        def store(c, res):
            ref[:, c:c + 512] = res.astype(ref.dtype)
        return store

    if prompt:
        pad = SUBLANES

        def store_slabs(c, res):
            for k in range(512 // LANES):
                ext_ref[c // LANES + k, pad:pad + tm, :] = res[:, k * LANES:(k + 1) * LANES]
        project(h, COLS_QKV, COLS_QK, store_slabs)
    project(h, 0, COLS_QKV_G, store_rows(qkv0_ref))
    if prompt:
        n_slab = hs_ref.shape[0]
        for k in range(n_slab):
            hs_ref[k] = hf[:, k * LANES:(k + 1) * LANES]
        for g, ref in ((1, qkv1_ref), (2, qkv2_ref)):
            d = DILATIONS[g]
            rows = tm // d
            hp = jnp.concatenate(
                [jnp.concatenate([hs_ref[k, pl.ds(r, rows, stride=d), :] for r in range(d)], axis=0)
                 for k in range(n_slab)], axis=1).astype(BF16)

            def store(c, res, ref=ref, d=d, rows=rows):
                res = res.astype(ref.dtype)
                for r in range(d):
                    ref[r, :, c:c + 512] = res[r * rows:(r + 1) * rows]
            project(hp, g * COLS_QKV_G, COLS_QKV_G, store)
    else:
        project(h, COLS_QKV_G, COLS_QKV_G, store_rows(qkv1_ref))
        project(h, 2 * COLS_QKV_G, COLS_QKV_G, store_rows(qkv2_ref))
    if prompt:
        for k in range(COLS_QK // LANES):
            ks = slice(k * LANES, (k + 1) * LANES)
            y = cb_ref[:, ks] + ext_ref[k, pad:pad + tm, :] * cw_ref[CONV_B - 1:CONV_B, ks]
            for i in range(CONV_B - 1):
                lo = pad - (CONV_B - 1) + i
                y = y + ext_ref[k, lo:lo + tm, :] * cw_ref[i:i + 1, ks]
            y = y * _sigmoid(y)
            if k * LANES < B_WIDTH:
                qk_ref[:, ks] = y.astype(qk_ref.dtype)
            else:
                kt_ref[k * LANES - B_WIDTH:(k + 1) * LANES - B_WIDTH, :] = (
                    (y * (DK_B ** -0.5)).T.astype(kt_ref.dtype))
            tail = ext_ref[k, tm:tm + pad, :]
            ctail_ref[:, ks] = tail
            ext_ref[k, 0:pad, :] = tail
    else:
        project(h, COLS_QKV, COLS_QK, store_rows(qk_ref))
    col = COLS_QKV + COLS_QK
    for ref in (vo_ref, gab_ref):
        project(h, col, ref.shape[-1], store_rows(ref))
        col += ref.shape[-1]
    gates = _dot(h, wg_ref[...])
    gcol_ref[...] = gates
    if prompt:
        grow_ref[...] = gates.T[:SUBLANES, :]


def _inproj(x, gain, w_main, w_gate, *, tm, act_dtype, seq=None, conv=None):
    n = x.shape[0]
    dilate = seq is not None
    row = lambda i: (i, 0)
    out_shape = [jax.ShapeDtypeStruct((n, COLS_QKV_G), act_dtype)]
    out_specs = [pl.BlockSpec((tm, COLS_QKV_G), row)]
    for g in (1, 2):
        if dilate:
            d = DILATIONS[g]
            tpb = seq // tm
            out_shape.append(jax.ShapeDtypeStruct((n // seq, d, seq // d, COLS_QKV_G), act_dtype))
            out_specs.append(pl.BlockSpec((None, d, tm // d, COLS_QKV_G),
                                          lambda i, tpb=tpb: (i // tpb, 0, i % tpb, 0)))
        else:
            out_shape.append(jax.ShapeDtypeStruct((n, COLS_QKV_G), act_dtype))
            out_specs.append(pl.BlockSpec((tm, COLS_QKV_G), row))
    for cols, dt in ((B_WIDTH, act_dtype) if dilate else (COLS_QK, F32), (COLS_VO, act_dtype),
                     (COLS_GAB, act_dtype), (LANES, F32)):
        out_shape.append(jax.ShapeDtypeStruct((n, cols), dt))
        out_specs.append(pl.BlockSpec((tm, cols), row))
    in_specs = [
        pl.BlockSpec((tm, D_MODEL), row),
        _resident((1, D_MODEL)),
        _resident(w_main.shape),
        _resident(w_gate.shape),
    ]
    args = [x, gain, w_main, w_gate]
    scratch = []
    tpb = None
    if dilate:
        tpb = seq // tm
        in_specs += [_resident(conv[0].shape), _resident(conv[1].shape)]
        args += list(conv)
        out_shape.append(jax.ShapeDtypeStruct((SUBLANES, n), F32))
        out_specs.append(pl.BlockSpec((SUBLANES, tm), lambda i: (0, i)))
        out_shape.append(jax.ShapeDtypeStruct((n // seq, SUBLANES, COLS_QK), F32))
        out_specs.append(pl.BlockSpec((None, SUBLANES, COLS_QK), lambda i: (i // tpb, 0, 0)))
        out_shape.append(jax.ShapeDtypeStruct((B_WIDTH, n), act_dtype))
        out_specs.append(pl.BlockSpec((B_WIDTH, tm), lambda i: (0, i)))
        scratch =[pltpu.VMEM((D_MODEL // LANES, tm, LANES), F32),
                   pltpu.VMEM((COLS_QK // LANES, tm + SUBLANES, LANES), F32)]
    return pl.pallas_call(
        functools.partial(_inproj_kernel, prompt=dilate, tiles_per_batch=tpb),
        grid=(n // tm,),
        in_specs=in_specs,
        out_specs=out_specs,
        out_shape=out_shape,
        scratch_shapes=scratch,
        compiler_params=_params(1),
        name="inproj",
    )(*args)


def _attn_kernel(q_ref, kp_ref, kc_ref, vp_ref, vc_ref, bias_ref, o_ref, lse_ref):
    n_blk = q_ref.shape[0] // Q_BLOCK
    has_prev = pl.program_id(1) > 0
    key_lane = lax.broadcasted_iota(jnp.int32, (1, 1, 2 * Q_BLOCK), 2)
    first_mask = jnp.where((key_lane < Q_BLOCK) & jnp.logical_not(has_prev), NEG, 0.0)
    lane = lax.broadcasted_iota(jnp.int32, (Q_BLOCK, LANES), 1)
    low_half = lane < DH_A
    pair = 2 * DH_A

    def window(i, cur_ref, first_ref):
        rows = slice(i * Q_BLOCK, (i + 1) * Q_BLOCK)
        prev_ref, prows = ((first_ref, slice(0, Q_BLOCK)) if i == 0
                           else (cur_ref, slice((i - 1) * Q_BLOCK, i * Q_BLOCK)))
        return [jnp.concatenate([prev_ref[prows, hp * pair:(hp + 1) * pair],
                                 cur_ref[rows, hp * pair:(hp + 1) * pair]], axis=0)
                for hp in range(H_G // 2)]

    def qk_scores(i):
        rows = slice(i * Q_BLOCK, (i + 1) * Q_BLOCK)
        scores = []
        for hp, kk in enumerate(window(i, kc_ref, kp_ref)):
            qp = q_ref[rows, hp * pair:(hp + 1) * pair]
            scores.append(_dot_nt(jnp.where(low_half, qp, jnp.zeros_like(qp)), kk))
            scores.append(_dot_nt(jnp.where(low_half, jnp.zeros_like(qp), qp), kk))
        return jnp.stack(scores)

    for i in range(n_blk):
        rows = slice(i * Q_BLOCK, (i + 1) * Q_BLOCK)
        s = qk_scores(i) + bias_ref[...]
        if i == 0:
            s = s + first_mask
        m = jnp.max(s, axis=-1, keepdims=True)
        p = jnp.exp(s - m)
        l = jnp.sum(p, axis=-1, keepdims=True)
        pb = p.astype(BF16)
        inv = 1.0 / l
        lse = m + jnp.log(l)
        lse_all = jnp.zeros((Q_BLOCK, LANES), F32)
        for hp, vv in enumerate(window(i, vc_ref, vp_ref)):
            cols = slice(hp * pair, (hp + 1) * pair)
            o_lo = _dot(pb[2 * hp], vv) * inv[2 * hp]
            o_hi = _dot(pb[2 * hp + 1], vv) * inv[2 * hp + 1]
            o_ref[rows, cols] = jnp.where(low_half, o_lo, o_hi).astype(o_ref.dtype)
        for h in range(H_G):
            lse_all = jnp.where(lane == h, lse[h], lse_all)
        lse_ref[rows, :] = lse_all


ATTN_BLOCKS = 8


def _attn_prompt(qkv, bias):
    nsub, u_len, _ = qkv.shape
    n_blk = math.gcd(ATTN_BLOCKS, u_len // Q_BLOCK)
    rows = n_blk * Q_BLOCK
    nb = u_len // rows

    def spec(col_block, prev):
        if prev:
            return pl.BlockSpec((None, Q_BLOCK, A_GROUP),
                                lambda s, j: (s, jnp.maximum(j * n_blk - 1, 0), col_block))
        return pl.BlockSpec((None, rows, A_GROUP), lambda s, j: (s, j, col_block))

    return pl.pallas_call(
        _attn_kernel,
        grid=(nsub, nb),
        in_specs=[spec(0, False), spec(1, True), spec(1, False), spec(2, True), spec(2, False),
                  _resident(bias.shape)],
        out_specs=[
            pl.BlockSpec((None, rows, A_GROUP), lambda s, j: (s, j, 0)),
            pl.BlockSpec((None, rows, LANES), lambda s, j: (s, j, 0)),
        ],
        out_shape=[
            jax.ShapeDtypeStruct((nsub, u_len, A_GROUP), BF16),
            jax.ShapeDtypeStruct((nsub, u_len, LANES), F32),
        ],
        compiler_params=_params(2),
        name="attn_prompt",
    )(qkv, qkv, qkv, qkv, qkv, bias)


def _kvtail_kernel(*refs):
    out_ref, nat_s = refs[-2:]
    srcs = refs[:-2]
    n_slab = nat_s.shape[0]
    for layer in range(len(srcs) // 2):
        @pl.when(pl.program_id(0) == layer)
        def _():
            for j, ref in enumerate(srcs[2 * layer:2 * layer + 2]):
                dil = ref.shape[0]
                for r in range(dil):
                    val = ref[r].astype(F32)
                    for s in range(n_slab):
                        nat_s[s, pl.ds(r, SPAN, stride=dil), :] = val[:, s * LANES:(s + 1) * LANES]
                for s in range(n_slab):
                    out_ref[j, s * LANES:(s + 1) * LANES, :] = nat_s[s].T


def _kv_tail(qkv_layers, g):
    depth = len(qkv_layers)
    batch, dil, u_len, _ = qkv_layers[0].shape
    keep = SPAN * dil
    last_blk = u_len // SPAN - 1
    specs, args = [], []
    for arr in qkv_layers:
        for col in (1, 2):
            specs.append(pl.BlockSpec((None, dil, SPAN, A_GROUP),
                                      lambda l, b, col=col: (b, 0, last_blk, col)))
            args.append(arr)
    return pl.pallas_call(
        _kvtail_kernel,
        grid=(depth, batch),
        in_specs=specs,
        out_specs=pl.BlockSpec((None, None, 2, A_GROUP, keep), lambda l, b: (l, b, 0, 0, 0)),
        out_shape=jax.ShapeDtypeStruct((depth, batch, 2, A_GROUP, keep), F32),
        scratch_shapes=[pltpu.VMEM((A_GROUP // LANES, keep, LANES), F32)],
        compiler_params=_params(2),
        name=f"kv_tail_g{g}",
    )(*args)


def _sattn_kernel(new_ref, kv0_ref, kv1_ref, kv2_ref, bt0_ref, bt1_ref, bt2_ref, b0_ref,
                  o_ref, lse_ref):
    eye = (lax.broadcasted_iota(jnp.int32, (DH_A, DH_A), 0)
           == lax.broadcasted_iota(jnp.int32, (DH_A, DH_A), 1))
    for g, (kv_ref, bt_ref) in enumerate(((kv0_ref, bt0_ref), (kv1_ref, bt1_ref),
                                          (kv2_ref, bt2_ref))):
        q = new_ref[g, 0]
        k_new = new_ref[g, 1]
        v_new = new_ref[g, 2]
        s_rows = []
        for h in range(H_G):
            q_col = jnp.sum(jnp.where(eye, q[h:h + 1, :], 0.0), axis=1, keepdims=True)
            s_rows.append(jnp.sum(kv_ref[0, h] * q_col, axis=0, keepdims=True))
        s = jnp.concatenate(s_rows, axis=0) + bt_ref[...]
        s0 = jnp.sum(k_new * q, axis=1, keepdims=True) + b0_ref[g]
        m = jnp.maximum(jnp.max(s, axis=1, keepdims=True), s0)
        p = jnp.exp(s - m[:, 0:1])
        p0 = jnp.exp(s0 - m)
        l = jnp.sum(p, axis=1, keepdims=True) + p0
        o_rows = []
        for h in range(H_G):
            o_col = jnp.sum(kv_ref[1, h] * p[h:h + 1, :], axis=1, keepdims=True)
            o_rows.append(jnp.sum(jnp.where(eye, o_col, 0.0), axis=0, keepdims=True))
        o_ref[g] = (jnp.concatenate(o_rows, axis=0) + p0 * v_new) / l
        lse_ref[g] = m + jnp.log(l)


def _attn_sample(new_qkv, caches, layer, bias_t, bias_0):
    nreq = new_qkv.shape[0]
    views = []
    specs = []
    for g, cache in enumerate(caches):
        n_buf = cache.shape[2]
        assert n_buf == SPAN * DILATIONS[g], "cache must hold exactly one window"
        views.append(cache.transpose(0, 1, 3, 4, 5, 2))
        specs.append(pl.BlockSpec((None, None, 2, H_G, DH_A, n_buf),
                                  lambda b, layer=layer: (layer, b, 0, 0, 0, 0)))
    out = jax.ShapeDtypeStruct((nreq, N_GROUPS, H_G, DH_A), F32)
    out_spec = pl.BlockSpec((None, N_GROUPS, H_G, DH_A), lambda b: (b, 0, 0, 0))
    return pl.pallas_call(
        _sattn_kernel,
        grid=(nreq,),
        in_specs=[pl.BlockSpec((None, N_GROUPS, 3, H_G, DH_A), lambda b: (b, 0, 0, 0, 0))] + specs
        + [_resident(t.shape) for t in bias_t] + [_resident(bias_0.shape)],
        out_specs=[out_spec, out_spec],
        out_shape=[out, out],
        compiler_params=_params(1),
        name="attn_sample",
    )(new_qkv, *views, *bias_t, bias_0)


def _mlstm_kernel(*refs, n_seq):
    q_ref = refs[0]
    kt_refs = refs[1:1 + n_seq]
    vo_ref, gcol_ref = refs[1 + n_seq:3 + n_seq]
    grow_refs = refs[3 + n_seq:3 + 2 * n_seq]
    gb_row_ref, gb_col_ref, bout_ref, c_ref, n_ref, m_ref, nrep_s = refs[3 + 2 * n_seq:]
    chunk = q_ref.shape[1]

    @pl.when(pl.program_id(1) == 0)
    def _():
        c_ref[...] = jnp.zeros_like(c_ref)
        nrep_s[...] = jnp.zeros_like(nrep_s)
        m_ref[...] = jnp.zeros_like(m_ref)

    ti = lax.broadcasted_iota(jnp.int32, (chunk, chunk), 0)
    si = lax.broadcasted_iota(jnp.int32, (chunk, chunk), 1)
    causal = ti >= si
    lower = causal.astype(BF16)
    upper = (si >= ti).astype(BF16)
    items = [(j, h) for j in range(n_seq) for h in range(NH_B)]
    hcol = lambda h: slice(h * DK_B, (h + 1) * DK_B)
    a_rows, b_ts, b_lasts = [], [], []
    for j in range(n_seq):
        z_col = gcol_ref[j] + gb_row_ref[...]
        z_row = grow_refs[j][...] + gb_col_ref[...]
        b_col = sum(_dot(lower, part) for part in _split3(_log_sigmoid(z_col)))
        b_row = sum(_dot(part, upper) for part in _split3(_log_sigmoid(z_row)))
        for h in range(NH_B):
            a_rows.append(z_row[h:h + 1, :] - b_row[NH_B + h:NH_B + h + 1, :])
            b_ts.append(b_col[:, NH_B + h:NH_B + h + 1])
            b_lasts.append(b_row[NH_B + h:NH_B + h + 1, chunk - 1:chunk])
    a_row = jnp.stack(a_rows)
    b_t = jnp.stack(b_ts)
    m_prev = jnp.stack([m_ref[j, :, h:h + 1] for j, h in items])
    a_mat = jnp.where(causal, a_row, NEG)
    gmax = jnp.maximum(m_prev, jnp.max(a_mat, axis=-1, keepdims=True))
    dw = jnp.exp(a_mat - gmax)
    iw = jnp.exp(m_prev - gmax)
    g_last = gmax[:, chunk - 1:chunk, :]
    decay = jnp.exp(m_prev - g_last)
    w_state = jnp.exp(a_row - g_last)

    qb = [q_ref[j, :, hcol(h)] for j, h in items]
    kt = [kt_refs[j][hcol(h), :] for j, h in items]
    vb = [vo_ref[j, :, hcol(h)] for j, h in items]
    c_old = [c_ref[j, h] for j, h in items]
    n_old = [nrep_s[j, h] for j, h in items]
    idx = range(len(items))
    qk = jnp.stack([_dot(qb[i], kt[i]) for i in idx]) * dw
    qkb = qk.astype(BF16)
    q_c = jnp.stack([_dot(qb[i], c_old[i].astype(BF16)) for i in idx])
    qk_v = jnp.stack([_dot(qkb[i], vb[i]) for i in idx])
    num = iw * q_c + qk_v
    q_n = jnp.stack([_dot(qb[i], n_old[i].astype(BF16))[:, 0:1] for i in idx])
    den = iw * q_n + jnp.sum(qk, axis=-1, keepdims=True)
    hid = num / jnp.maximum(jnp.abs(den), jnp.exp(-(b_t + gmax)))
    for i, (j, h) in enumerate(items):
        o_gate = _sigmoid(vo_ref[j, :, B_WIDTH + h * DK_B:B_WIDTH + (h + 1) * DK_B].astype(F32))
        bout_ref[j, :, hcol(h)] = (o_gate * hid[i]).astype(bout_ref.dtype)

    kwt = (jnp.stack(kt).astype(F32) * w_state).astype(BF16)
    ones = jnp.ones((chunk, LANES), BF16)
    for i, (j, h) in enumerate(items):
        c_ref[j, h] = decay[i] * c_old[i] + _dot(kwt[i], vb[i])
        nrep_s[j, h] = decay[i] * n_old[i] + _dot(kwt[i], ones)
        m_ref[j, :, h:h + 1] = b_lasts[i] + g_last[i]

    @pl.when(pl.program_id(1) == pl.num_programs(1) - 1)
    def _():
        for j, h in items:
            n_ref[j, h:h + 1, :] = nrep_s[j, h].T[0:1, :]


def _mlstm_prompt(q, kt, vo, gcol, grow, gb_row, gb_col, batch, seq):
    chunk = MLSTM_CHUNK
    n_seq = math.gcd(MLSTM_SEQS, batch)
    nc = seq // chunk
    seq3 = lambda b, c: (b, c, 0)
    lanes = [lambda b, c, j=j: (0, (b * n_seq + j) * nc + c) for j in range(n_seq)]
    state = lambda b, c: (b, 0, 0)
    return pl.pallas_call(
        functools.partial(_mlstm_kernel, n_seq=n_seq),
        grid=(batch // n_seq, nc),
        in_specs=[pl.BlockSpec((n_seq, chunk, B_WIDTH), seq3)]
        + [pl.BlockSpec((B_WIDTH, chunk), lanes[j]) for j in range(n_seq)]
        + [pl.BlockSpec((n_seq, chunk, COLS_VO), seq3), pl.BlockSpec((n_seq, chunk, LANES), seq3)]
        + [pl.BlockSpec((SUBLANES, chunk), lanes[j]) for j in range(n_seq)]
        + [_resident(gb_row.shape), _resident(gb_col.shape)],
        out_specs=[
            pl.BlockSpec((n_seq, chunk, B_WIDTH), seq3),
            pl.BlockSpec((n_seq, NH_B, DK_B, DK_B), lambda b, c: (b, 0, 0, 0)),
            pl.BlockSpec((n_seq, NH_B, DK_B), state),
            pl.BlockSpec((n_seq, 1, LANES), state),
        ],
        out_shape=[
            jax.ShapeDtypeStruct((batch, seq, B_WIDTH), BF16),
            jax.ShapeDtypeStruct((batch, NH_B, DK_B, DK_B), F32),
            jax.ShapeDtypeStruct((batch, NH_B, DK_B), F32),
            jax.ShapeDtypeStruct((batch, 1, LANES), F32),
        ],
        scratch_shapes=[pltpu.VMEM((n_seq, NH_B, DK_B, LANES), F32)],
        compiler_params=_params(2),
        name="mlstm_prompt",
    )(q.reshape(batch, seq, B_WIDTH), *([kt] * n_seq), vo.reshape(batch, seq, COLS_VO),
      gcol.reshape(batch, seq, LANES), *([grow] * n_seq), gb_row, gb_col)


def _smlstm_kernel(qk_ref, vo_ref, g_ref, gb_ref, cs_ref, cw_ref, cb_ref, c0_ref, n0_ref, m0_ref,
                   bout_ref, c1_ref, n1_ref, m1_ref, ncs_ref):
    u = qk_ref[...]
    y = cb_ref[...] + u * cw_ref[CONV_B - 1:CONV_B, :]
    for i in range(CONV_B - 1):
        y = y + cs_ref[i:i + 1, :] * cw_ref[i:i + 1, :]
    ncs_ref[0:CONV_B - 2, :] = cs_ref[1:CONV_B - 1, :]
    ncs_ref[CONV_B - 2:CONV_B - 1, :] = u
    y = y * _sigmoid(y)
    qf = y[:, :B_WIDTH]
    kf = y[:, B_WIDTH:] * (DK_B ** -0.5)
    z = g_ref[...] + gb_ref[...]
    lf_all = _log_sigmoid(z)
    m0 = m0_ref[...]
    eye = (lax.broadcasted_iota(jnp.int32, (DK_B, DK_B), 0)
           == lax.broadcasted_iota(jnp.int32, (DK_B, DK_B), 1))
    for h in range(NH_B):
        hs = slice(h * DK_B, (h + 1) * DK_B)
        ig = z[:, h:h + 1]
        inter = lf_all[:, NH_B + h:NH_B + h + 1] + m0[:, h:h + 1]
        m_t = jnp.maximum(inter, ig)
        dw = jnp.exp(ig - m_t)
        iw = jnp.exp(inter - m_t)
        qh = qf[:, hs]
        kh = kf[:, hs]
        vh = vo_ref[:, hs]
        c_old = c0_ref[h]
        n_old = n0_ref[h:h + 1, :]
        qk = jnp.sum(qh * kh, axis=1, keepdims=True) * dw
        num = iw * _hdot(qh, c_old) + qk * vh
        den = iw * jnp.sum(qh * n_old, axis=1, keepdims=True) + qk
        hid = num / jnp.maximum(jnp.abs(den), jnp.exp(-m_t))
        o_gate = _sigmoid(vo_ref[:, B_WIDTH + h * DK_B:B_WIDTH + (h + 1) * DK_B])
        bout_ref[:, hs] = o_gate * hid
        k_col = jnp.sum(jnp.where(eye, kh, 0.0), axis=1, keepdims=True)
        c1_ref[h] = iw * c_old + (dw * k_col) * vh
        n1_ref[h:h + 1, :] = iw * n_old + dw * kh
        m1_ref[:, h:h + 1] = m_t


def _mlstm_sample(qk_pre, vo, gcol, gb_row, conv_state, cw, cb, c0, n0, m0, layer):
    nreq = qk_pre.shape[0]
    one = lambda b: (b, 0, 0)
    lay3 = lambda b, layer=layer: (layer, b, 0, 0)
    lay4 = lambda b, layer=layer: (layer, b, 0, 0, 0)
    return pl.pallas_call(
        _smlstm_kernel,
        grid=(nreq,),
        in_specs=[
            pl.BlockSpec((None, 1, COLS_QK), one),
            pl.BlockSpec((None, 1, COLS_VO), one),
            pl.BlockSpec((None, 1, LANES), one),
            _resident(gb_row.shape),
            pl.BlockSpec((None, None, CONV_B - 1, COLS_QK), lay3),
            _resident(cw.shape), _resident(cb.shape),
            pl.BlockSpec((None, None, NH_B, DK_B, DK_B), lay4),
            pl.BlockSpec((None, None, NH_B, DK_B), lay3),
            pl.BlockSpec((None, None, 1, LANES), lay3),
        ],
        out_specs=[
            pl.BlockSpec((None, 1, B_WIDTH), one),
            pl.BlockSpec((None, NH_B, DK_B, DK_B), lambda b: (b, 0, 0, 0)),
            pl.BlockSpec((None, NH_B, DK_B), one),
            pl.BlockSpec((None, 1, LANES), one),
            pl.BlockSpec((None, CONV_B - 1, COLS_QK), one),
        ],
        out_shape=[
            jax.ShapeDtypeStruct((nreq, 1, B_WIDTH), F32),
            jax.ShapeDtypeStruct((nreq, NH_B, DK_B, DK_B), F32),
            jax.ShapeDtypeStruct((nreq, NH_B, DK_B), F32),
            jax.ShapeDtypeStruct((nreq, 1, LANES), F32),
            jax.ShapeDtypeStruct((nreq, CONV_B - 1, COLS_QK), F32),
        ],
        compiler_params=_params(1),
        name="mlstm_sample",
    )(qk_pre.reshape(nreq, 1, COLS_QK), vo.reshape(nreq, 1, COLS_VO),
      gcol.reshape(nreq, 1, LANES), gb_row, conv_state, cw, cb, c0, n0, m0)


def _merge_kernel(o0_ref, o1_ref, o2_ref, l0_ref, l1_ref, l2_ref, bo_ref, gab_ref, x_ref,
                  wpa_ref, wpb_ref, wo_ref, g2_ref, e_ref, x2_ref, h2_ref, *scratch):
    tm = x_ref.shape[0]

    def natural(ref, buf):
        dil, rows, _ = ref.shape
        n_slab = buf.shape[0]
        for r in range(dil):
            val = ref[r].astype(F32)
            for k in range(n_slab):
                buf[k, pl.ds(r, rows, stride=dil), :] = val[:, k * LANES:(k + 1) * LANES]
        return jnp.concatenate([buf[k] for k in range(n_slab)], axis=1)

    if scratch:
        os_ = [o0_ref[...].astype(F32), natural(o1_ref, scratch[0]), natural(o2_ref, scratch[1])]
        lses = [l0_ref[...], natural(l1_ref, scratch[2]), natural(l2_ref, scratch[3])]
    else:
        os_ = [r[...].astype(F32) for r in (o0_ref, o1_ref, o2_ref)]
        lses = [r[...] for r in (l0_ref, l1_ref, l2_ref)]
    top = jnp.maximum(jnp.maximum(lses[0], lses[1]), lses[2])
    ws = [jnp.exp(l - top) for l in lses]
    inv = 1.0 / (ws[0] + ws[1] + ws[2])
    e2 = e_ref[...]
    a = None
    for w, o in zip(ws, os_):
        alpha = w * inv
        hi = alpha.astype(BF16)
        lo = (alpha - hi.astype(F32)).astype(BF16)
        term = _dot(jnp.concatenate([hi, lo], axis=1), e2) * o
        a = term if a is None else a + term
    pa = _dot(a.astype(BF16), wpa_ref[...])
    pb = _dot(bo_ref[...].astype(BF16), wpb_ref[...])
    merged = (_sigmoid(gab_ref[:, :D_MODEL].astype(F32)) * pa
              + _sigmoid(gab_ref[:, D_MODEL:].astype(F32)) * pb)
    x2 = x_ref[...] + _dot(merged.astype(BF16), wo_ref[...])
    x2_ref[...] = x2
    h2_ref[...] = _rms(x2, g2_ref[...]).astype(BF16)


def _merge(os_, lses, bout, gab, x, wpa, wpb, wo, g2, e_bf, *, tm, seq=None):
    n = x.shape[0]
    row = lambda i: (i, 0)

    def group_specs(cols):
        specs = [pl.BlockSpec((tm, cols), row)]
        for g in (1, 2):
            if seq is None:
                specs.append(pl.BlockSpec((tm, cols), row))
            else:
                d = DILATIONS[g]
                tpb = seq // tm
                specs.append(pl.BlockSpec((None, d, tm // d, cols),
                                          lambda i, tpb=tpb: (i // tpb, 0, i % tpb, 0)))
        return specs

    scratch = []
    if seq is not None:
        scratch = ([pltpu.VMEM((A_GROUP // LANES, tm, LANES), F32)] * 2
                   + [pltpu.VMEM((1, tm, LANES), F32)] * 2)
    return pl.pallas_call(
        _merge_kernel,
        grid=(n // tm,),
        scratch_shapes=scratch,
        in_specs=group_specs(A_GROUP) + group_specs(LANES) + [
            pl.BlockSpec((tm, B_WIDTH), row),
            pl.BlockSpec((tm, COLS_GAB), row),
            pl.BlockSpec((tm, D_MODEL), row),
            _resident(wpa.shape), _resident(wpb.shape), _resident(wo.shape),
            _resident(g2.shape), _resident(e_bf.shape),
        ],
        out_specs=[pl.BlockSpec((tm, D_MODEL), row), pl.BlockSpec((tm, D_MODEL), row)],
        out_shape=[jax.ShapeDtypeStruct((n, D_MODEL), F32), jax.ShapeDtypeStruct((n, D_MODEL), BF16)],
        compiler_params=_params(1),
        name="merge",
    )(*os_, *lses, bout, gab, x, wpa, wpb, wo, g2, e_bf)


def _ffn_kernel(*refs, tiles_per_batch, from_state, final_norm):
    h2_ref, x2_ref, wup_ref, wdn_ref, cw_ref, cb_ref = refs[:6]
    pos = 6
    if from_state:
        prev_refs = refs[pos:pos + CONV_F - 1]
        pos += CONV_F - 1
    if final_norm:
        fg_ref = refs[pos]
        pos += 1
    x3_ref, u_ref = refs[pos:pos + 2]
    tm = h2_ref.shape[0]
    pad = SUBLANES
    if not from_state:
        ext_s = refs[pos + 2]

        @pl.when(pl.program_id(0) % tiles_per_batch == 0)
        def _():
            ext_s[:, 0:pad, :] = jnp.zeros((ext_s.shape[0], pad, LANES), F32)

    h2 = h2_ref[...]
    acc = x2_ref[...]
    n_chunk = D_FF // FF_CHUNK

    def halves(c):
        return (c * FF_CHUNK, D_FF + c * FF_CHUNK)

    def up(c):
        us = []
        for off in halves(c):
            u = _dot(h2, wup_ref[:, off:off + FF_CHUNK])
            if from_state:
                u_ref[:, off:off + FF_CHUNK] = u
                us.append(u)
            else:
                for k in range(FF_CHUNK // LANES):
                    ext_s[off // LANES + k, pad:pad + tm, :] = u[:, k * LANES:(k + 1) * LANES]
        return us

    def conv(c, us):
        ys = []
        for j, off in enumerate(halves(c)):
            cs = slice(off, off + FF_CHUNK)
            if from_state:
                y = cb_ref[:, cs] + us[j] * cw_ref[CONV_F - 1:CONV_F, cs]
                for i in range(CONV_F - 1):
                    y = y + prev_refs[i][:, cs] * cw_ref[i:i + 1, cs]
            else:
                parts = []
                for k in range(FF_CHUNK // LANES):
                    slab = off // LANES + k
                    ks = slice(off + k * LANES, off + (k + 1) * LANES)
                    part = cb_ref[:, ks]
                    for i in range(CONV_F):
                        lo = pad - (CONV_F - 1) + i
                        part = part + ext_s[slab, lo:lo + tm, :] * cw_ref[i:i + 1, ks]
                    parts.append(part)
                    tail = ext_s[slab, tm:tm + pad, :]
                    u_ref[:, ks] = tail
                    ext_s[slab, 0:pad, :] = tail
                y = jnp.concatenate(parts, axis=1)
            ys.append(y)
        return ys

    ahead = 1 if from_state else FF_AHEAD
    pending = [up(c) for c in range(min(ahead, n_chunk))]
    for c in range(n_chunk):
        us = pending.pop(0)
        if c + ahead < n_chunk:
            pending.append(up(c + ahead))
        y_act, y_gate = conv(c, us)
        act = _gelu_tanh(y_act) * y_gate
        acc = acc + _dot(act.astype(BF16), wdn_ref[c * FF_CHUNK:(c + 1) * FF_CHUNK, :])
    if final_norm:
        acc = _rms(acc, fg_ref[...])
    x3_ref[...] = acc


def _ffn(h2, x2, wup, wdn, cw, cb, *, tm, tiles_per_batch=None, prev_rows=None, final_g=None):
    n = h2.shape[0]
    from_state = prev_rows is not None
    row = lambda i: (i, 0)
    in_specs = [
        pl.BlockSpec((tm, D_MODEL), row), pl.BlockSpec((tm, D_MODEL), row),
        _resident(wup.shape), _resident(wdn.shape), _resident(cw.shape), _resident(cb.shape),
    ]
    args = [h2, x2, wup, wdn, cw, cb]
    scratch = []
    if from_state:
        in_specs += [pl.BlockSpec((tm, 2 * D_FF), row)] * (CONV_F - 1)
        args += list(prev_rows)
        u_shape = jax.ShapeDtypeStruct((n, 2 * D_FF), F32)
        u_spec = pl.BlockSpec((tm, 2 * D_FF), row)
    else:
        n_batch = n // (tm * tiles_per_batch)
        u_shape = jax.ShapeDtypeStruct((n_batch, SUBLANES, 2 * D_FF), F32)
        u_spec = pl.BlockSpec((None, SUBLANES, 2 * D_FF), lambda i: (i // tiles_per_batch, 0, 0))
        scratch = [pltpu.VMEM((2 * D_FF // LANES, tm + SUBLANES, LANES), F32)]
    if final_g is not None:
        in_specs.append(_resident(final_g.shape))
        args.append(final_g)
    return pl.pallas_call(
        functools.partial(_ffn_kernel, tiles_per_batch=tiles_per_batch, from_state=from_state,
                          final_norm=final_g is not None),
        grid=(n // tm,),
        in_specs=in_specs,
        out_specs=[pl.BlockSpec((tm, D_MODEL), row), u_spec],
        out_shape=[jax.ShapeDtypeStruct((n, D_MODEL), F32), u_shape],
        scratch_shapes=scratch,
        compiler_params=_params(1),
        name="ffn",
    )(*args)


def _t5_bucket(dist):
    max_exact = NUM_BUCKETS // 2
    df = jnp.maximum(dist, 1).astype(F32)
    large = max_exact + (jnp.log(df / max_exact) / math.log(MAX_DISTANCE / max_exact)
                         * (NUM_BUCKETS - max_exact)).astype(jnp.int32)
    large = jnp.minimum(large, NUM_BUCKETS - 1)
    return jnp.where(dist < max_exact, dist, large)


def _bias_table(rel_bias, g, dist):
    bucket = _t5_bucket(jnp.asarray(dist, jnp.int32))
    table = rel_bias[:, g * H_G:(g + 1) * H_G].reshape((NUM_BUCKETS, H_G) + (1,) * bucket.ndim)
    ids = jnp.arange(NUM_BUCKETS).reshape((NUM_BUCKETS, 1) + (1,) * bucket.ndim)
    return jnp.sum(jnp.where(bucket[None, None] == ids, table, 0.0), axis=0)


def _prompt_bias(rel_bias, g):
    qi = np.arange(Q_BLOCK)[:, None]
    ki = np.arange(2 * Q_BLOCK)[None, :]
    rel = qi + Q_BLOCK - ki
    band = (rel >= 0) & (rel <= SPAN)
    bias = _bias_table(rel_bias, g, np.maximum(rel, 0) * DILATIONS[g])
    return jnp.where(band[None], bias, NEG)


def _sample_bias(rel_bias):
    tables, news = [], []
    for g in range(N_GROUPS):
        dil = DILATIONS[g]
        n_buf = SPAN * dil
        pos = np.arange(n_buf)
        bias = _bias_table(rel_bias, g, n_buf - pos)
        tables.append(jnp.where((pos % dil == 0)[None], bias, NEG))
        news.append(jnp.broadcast_to(_bias_table(rel_bias, g, np.zeros((1,), np.int32)), (H_G, DH_A)))
    return tables, jnp.stack(news)


def _head_indicator():
    e = np.zeros((LANES, A_GROUP), np.float32)
    for h in range(H_G):
        e[h, h * DH_A:(h + 1) * DH_A] = 1.0
    return e


def kernel(x_prompt, x_sample, cache_kv_w128, cache_kv_w512, cache_kv_w2048, state_mlstm_conv,
           state_mlstm_C, state_mlstm_n, state_mlstm_m, state_ffn_conv, rel_bias, norm1_g, w_in,
           mconv_w, mconv_b, mgate_b, w_pa, w_pb, w_o, norm2_g, w_up, fconv_w, fconv_b, w_down,
           final_norm_g):
    batch, seq, _ = x_prompt.shape
    nreq = x_sample.shape[0]
    depth = w_in.shape[0]
    n_p = batch * seq
    caches = (cache_kv_w128, cache_kv_w512, cache_kv_w2048)

    assert seq % (Q_BLOCK * DILATIONS[-1]) == 0 and seq >= WINDOWS[-1]
    e_bf = jnp.asarray(np.concatenate([_head_indicator()] * 2, axis=0), BF16)
    prompt_bias = [_prompt_bias(rel_bias, g) for g in range(N_GROUPS)]
    sbias_m, sbias_0 = _sample_bias(rel_bias)
    m0_all = jnp.pad(state_mlstm_m, ((0, 0), (0, 0), (0, LANES - NH_B)))[:, :, None, :]
    fg = final_norm_g.reshape(1, D_MODEL)

    xp = x_prompt.reshape(n_p, D_MODEL)
    xs = x_sample.reshape(nreq, D_MODEL)
    p_st = [[] for _ in range(8)]
    s_st = [[] for _ in range(8)]

    for l in range(depth):
        last = l == depth - 1
        w_main, w_gate = _regroup_w_in(w_in[l])
        g1 = norm1_g[l].reshape(1, D_MODEL)
        g2 = norm2_g[l].reshape(1, D_MODEL)
        gate_bias = mgate_b[l].reshape(1, 2 * NH_B)
        gb_row = jnp.pad(gate_bias, ((0, 0), (0, LANES - 2 * NH_B)))
        gb_col = gate_bias.reshape(2 * NH_B, 1)
        mcw, mcb = mconv_w[l], mconv_b[l].reshape(1, COLS_QK)
        fcw, fcb = fconv_w[l], fconv_b[l].reshape(1, 2 * D_FF)
        wpa, wpb, wo = w_pa[l].astype(BF16), w_pb[l].astype(BF16), w_o[l].astype(BF16)
        wup, wdn = w_up[l].astype(BF16), w_down[l].astype(BF16)

        *qkvs, q_b, vo, gab, gcol, grow, conv_tail, kt_b = _inproj(
            xp, g1, w_main, w_gate, tm=256, act_dtype=BF16, seq=seq, conv=(mcw, mcb))
        os_, lses = [], []
        for g in range(N_GROUPS):
            d = DILATIONS[g]
            o_g, lse_g = _attn_prompt(qkvs[g].reshape(batch * d, seq // d, COLS_QKV_G), prompt_bias[g])
            shape = (n_p,) if g == 0 else (batch, d, seq // d)
            os_.append(o_g.reshape(shape + (A_GROUP,)))
            lses.append(lse_g.reshape(shape + (LANES,)))
        bout, c_p, n_p_state, m_p = _mlstm_prompt(q_b, kt_b, vo, gcol, grow, gb_row, gb_col,
                                                  batch, seq)
        x2, h2 = _merge(os_, lses, bout.reshape(n_p, B_WIDTH), gab, xp, wpa, wpb, wo, g2, e_bf,
                        tm=512, seq=seq)
        xp, u_tail = _ffn(h2, x2, wup, wdn, fcw, fcb, tm=256, tiles_per_batch=seq // 256,
                          final_g=fg if last else None)

        for g in range(N_GROUPS):
            p_st[g].append(qkvs[g].reshape(batch, DILATIONS[g], seq // DILATIONS[g], COLS_QKV_G))
        p_st[3].append(conv_tail[:, SUBLANES - (CONV_B - 1):])
        p_st[4].append(c_p)
        p_st[5].append(n_p_state)
        p_st[6].append(m_p[:, 0, :NH_B])
        p_st[7].append(u_tail[:, SUBLANES - (CONV_F - 1):])

        *qkvs_s, qk_s, vo_s, gab_s, gcol_s = _inproj(xs, g1, w_main, w_gate, tm=nreq, act_dtype=F32)
        new_qkv = jnp.stack(qkvs_s, axis=1).reshape(nreq, N_GROUPS, 3, H_G, DH_A)
        o_s, lse_s = _attn_sample(new_qkv, caches, l, sbias_m, sbias_0)
        bout_s, c_s, n_s, m_s, conv_s = _mlstm_sample(qk_s, vo_s, gcol_s, gb_row, state_mlstm_conv,
                                                      mcw, mcb, state_mlstm_C, state_mlstm_n, m0_all, l)
        lse_pad = jnp.pad(lse_s[..., 0], ((0, 0), (0, 0), (0, LANES - H_G)))
        x2_s, h2_s = _merge([o_s[:, g].reshape(nreq, A_GROUP) for g in range(N_GROUPS)],
                            [lse_pad[:, g] for g in range(N_GROUPS)],
                            bout_s.reshape(nreq, B_WIDTH), gab_s, xs, wpa, wpb, wo, g2, e_bf, tm=nreq)
        fbuf = state_ffn_conv[l]
        xs, u_s = _ffn(h2_s, x2_s, wup, wdn, fcw, fcb, tm=nreq,
                       prev_rows=[fbuf[:, i] for i in range(CONV_F - 1)],
                       final_g=fg if last else None)

        for g in range(N_GROUPS):
            s_st[g].append(new_qkv[:, g, 1:][:, None])
        s_st[3].append(conv_s)
        s_st[4].append(c_s)
        s_st[5].append(n_s)
        s_st[6].append(m_s[:, 0, :NH_B])
        s_st[7].append(jnp.concatenate([fbuf[:, 1:], u_s[:, None, :]], axis=1))

    outs = [xp.reshape(batch, seq, D_MODEL), xs.reshape(nreq, 1, D_MODEL)]
    for i in range(8):
        if i < N_GROUPS:
            kv = _kv_tail(p_st[i], i).reshape(depth, batch, 2, H_G, DH_A, SPAN * DILATIONS[i])
            outs.append(kv.transpose(0, 1, 5, 2, 3, 4))
        else:
            outs.append(jnp.stack(p_st[i], 0))
        outs.append(jnp.stack(s_st[i], 0))
    return tuple(outs)
```

```python
import functools
import math

import numpy as np
import jax
import jax.numpy as jnp
from jax import lax
from jax.experimental import pallas as pl
from jax.experimental.pallas import tpu as pltpu

F32 = jnp.float32
BF16 = jnp.bfloat16
HIGHEST = lax.Precision.HIGHEST

D_MODEL = 1024
WINDOWS = (128, 512, 2048)
DILATIONS = (1, 4, 16)
N_GROUPS = 3
H_G = 8
DH_A = 64
A_GROUP = H_G * DH_A
A_QKV = N_GROUPS * A_GROUP
Q_BLOCK = 128
SPAN = 128
NH_B = 4
DK_B = 256
B_WIDTH = NH_B * DK_B
CONV_B = 4
D_FF = 2816
CONV_F = 3
NUM_BUCKETS = 32
MAX_DISTANCE = 2048
RMS_EPS = 1e-6
NEG = -1e30

LANES = 128
SUBLANES = 8
FF_CHUNK = 256
FF_AHEAD = 3
MLSTM_CHUNK = 256
MLSTM_SEQS = 2
VMEM_LIMIT = 56 * 1024 * 1024

COLS_QKV_G = 3 * A_GROUP
COLS_QKV = N_GROUPS * COLS_QKV_G
COLS_QK = 2 * B_WIDTH
COLS_VO = 2 * B_WIDTH
COLS_GAB = 2 * D_MODEL
GATE_COL0 = COLS_QKV + COLS_QK + COLS_VO


def _dot(a, b):
    return jnp.dot(a, b, preferred_element_type=F32)


def _hdot(a, b):
    return jnp.dot(a, b, precision=HIGHEST, preferred_element_type=F32)


def _dot_nt(a, b):
    return lax.dot_general(a, b, (((1,), (1,)), ((), ())), preferred_element_type=F32)


def _dot_tn(a, b):
    return lax.dot_general(a, b, (((0,), (0,)), ((), ())), preferred_element_type=F32)


def _split3(x):
    hi = x.astype(BF16)
    rest = x - hi.astype(F32)
    mid = rest.astype(BF16)
    return hi, mid, (rest - mid.astype(F32)).astype(BF16)


def _sigmoid(x):
    return 1.0 / (1.0 + jnp.exp(-x))


def _log_sigmoid(x):
    return jnp.minimum(x, 0.0) - jnp.log1p(jnp.exp(-jnp.abs(x)))


def _gelu_tanh(x):
    return 0.5 * x * (1.0 + jnp.tanh(math.sqrt(2.0 / math.pi) * (x + 0.044715 * (x * x * x))))


def _rms(x, g):
    return x * lax.rsqrt(jnp.mean(x * x, axis=-1, keepdims=True) + RMS_EPS) * g


def _resident(shape):
    nd = len(shape)
    return pl.BlockSpec(shape, lambda *_: (0,) * nd, pipeline_mode=pl.Buffered(1))


def _layer_resident(stacked_shape, layer):
    nd = len(stacked_shape)
    return pl.BlockSpec((None,) + tuple(stacked_shape[1:]), lambda *_: (layer,) + (0,) * (nd - 1),
                        pipeline_mode=pl.Buffered(1))


def _params(n_grid):
    return pltpu.CompilerParams(dimension_semantics=("arbitrary",) * n_grid,
                                vmem_limit_bytes=VMEM_LIMIT)


W_BLOCK = 512
N_QKV_BLOCKS = COLS_QKV // W_BLOCK
N_HEAD_BLOCKS = GATE_COL0 // W_BLOCK


def _regroup_kernel(head_ref, tail_ref, gate_ref, o_ref, og_ref):
    j = pl.program_id(0)

    @pl.when(j == 0)
    def _():
        lane = lax.broadcasted_iota(jnp.int32, og_ref.shape, 1)
        og_ref[...] = jnp.where(lane < 2 * NH_B, gate_ref[...].T, 0.0).astype(og_ref.dtype)

    @pl.when(j < N_HEAD_BLOCKS)
    def _():
        is_q = (j < N_QKV_BLOCKS) & (j % 3 == 0)
        scale = jnp.where(is_q, DH_A ** -0.5, 1.0)
        o_ref[...] = (head_ref[...] * scale).T.astype(o_ref.dtype)

    @pl.when(j >= N_HEAD_BLOCKS)
    def _():
        o_ref[...] = tail_ref[0].T.astype(o_ref.dtype)


def _regroup_w_in(w_stack, layer):
    wt = w_stack.transpose(0, 2, 1)
    n_blocks = N_HEAD_BLOCKS + COLS_GAB // W_BLOCK
    tail0 = GATE_COL0 + 2 * NH_B
    assert GATE_COL0 % LANES == 0 and tail0 % SUBLANES == 0

    def head_block(j):
        regrouped = (j % 3) * N_GROUPS + j // 3
        return layer, jnp.where(j < N_QKV_BLOCKS, regrouped, jnp.minimum(j, N_HEAD_BLOCKS - 1)), 0

    return pl.pallas_call(
        _regroup_kernel,
        grid=(n_blocks,),
        in_specs=[
            pl.BlockSpec((None, W_BLOCK, D_MODEL), head_block),
            pl.BlockSpec((pl.Element(1), pl.Element(W_BLOCK), pl.Element(D_MODEL)),
                         lambda j: (layer, pl.multiple_of(
                             tail0 + jnp.maximum(j - N_HEAD_BLOCKS, 0) * W_BLOCK, SUBLANES), 0)),
            pl.BlockSpec((None, LANES, D_MODEL), lambda j: (layer, GATE_COL0 // LANES, 0)),
        ],
        out_specs=[pl.BlockSpec((D_MODEL, W_BLOCK), lambda j: (0, j)),
                   pl.BlockSpec((D_MODEL, LANES), lambda j: (0, 0))],
        out_shape=[jax.ShapeDtypeStruct((D_MODEL, n_blocks * W_BLOCK), BF16),
                   jax.ShapeDtypeStruct((D_MODEL, LANES), BF16)],
        compiler_params=_params(1),
        name="regroup_w_in",
    )(wt, wt, wt)


def _inproj_kernel(*refs, prompt, tiles_per_batch):
    if prompt:
        (x_ref, g_ref, w_ref, wg_ref, cw_ref, cb_ref, qkv0_ref, qkv1_ref, qkv2_ref, qk_ref, vo_ref,
         gab_ref, gcol_ref, grow_ref, ctail_ref, kt_ref, hs_ref, ext_ref) = refs
    else:
        (x_ref, g_ref, w_ref, wg_ref, qkv0_ref, qkv1_ref, qkv2_ref, qk_ref, vo_ref, gab_ref,
         gcol_ref) = refs
    if prompt:
        @pl.when(pl.program_id(0) % tiles_per_batch == 0)
        def _():
            ext_ref[:, 0:SUBLANES, :] = jnp.zeros((ext_ref.shape[0], SUBLANES, LANES), F32)

    hf = _rms(x_ref[...], g_ref[...])
    h = hf.astype(BF16)
    tm = hf.shape[0]

    def project(lhs, col0, width, store):
        for c in range(0, width, 512):
            store(c, _dot(lhs, w_ref[:, col0 + c:col0 + c + 512]))

    def store_rows(ref):
        def store(c, res):
            ref[:, c:c + 512] = res.astype(ref.dtype)
        return store

    if prompt:
        pad = SUBLANES

        def store_slabs(c, res):
            for k in range(512 // LANES):
                ext_ref[c // LANES + k, pad:pad + tm, :] = res[:, k * LANES:(k + 1) * LANES]
        project(h, COLS_QKV, COLS_QK, store_slabs)
    project(h, 0, COLS_QKV_G, store_rows(qkv0_ref))
    if prompt:
        n_slab = hs_ref.shape[0]
        for k in range(n_slab):
            hs_ref[k] = hf[:, k * LANES:(k + 1) * LANES]
        for g, ref in ((1, qkv1_ref), (2, qkv2_ref)):
            d = DILATIONS[g]
            rows = tm // d
            hp = jnp.concatenate(
                [jnp.concatenate([hs_ref[k, pl.ds(r, rows, stride=d), :] for r in range(d)], axis=0)
                 for k in range(n_slab)], axis=1).astype(BF16)

            def store(c, res, ref=ref, d=d, rows=rows):
                res = res.astype(ref.dtype)
                for r in range(d):
                    ref[r, :, c:c + 512] = res[r * rows:(r + 1) * rows]
            project(hp, g * COLS_QKV_G, COLS_QKV_G, store)
    else:
        project(h, COLS_QKV_G, COLS_QKV_G, store_rows(qkv1_ref))
        project(h, 2 * COLS_QKV_G, COLS_QKV_G, store_rows(qkv2_ref))
    if prompt:
        for k in range(COLS_QK // LANES):
            ks = slice(k * LANES, (k + 1) * LANES)
            y = cb_ref[:, ks] + ext_ref[k, pad:pad + tm, :] * cw_ref[CONV_B - 1:CONV_B, ks]
            for i in range(CONV_B - 1):
                lo = pad - (CONV_B - 1) + i
                y = y + ext_ref[k, lo:lo + tm, :] * cw_ref[i:i + 1, ks]
            y = y * _sigmoid(y)
            if k * LANES < B_WIDTH:
                qk_ref[:, ks] = y.astype(qk_ref.dtype)
            else:
                kt_ref[k * LANES - B_WIDTH:(k + 1) * LANES - B_WIDTH, :] = (
                    (y * (DK_B ** -0.5)).T.astype(kt_ref.dtype))
            tail = ext_ref[k, tm:tm + pad, :]
            ctail_ref[:, ks] = tail
            ext_ref[k, 0:pad, :] = tail
    else:
        project(h, COLS_QKV, COLS_QK, store_rows(qk_ref))
    col = COLS_QKV + COLS_QK
    for ref in (vo_ref, gab_ref):
        project(h, col, ref.shape[-1], store_rows(ref))
        col += ref.shape[-1]
    gates = _dot(h, wg_ref[...])
    gcol_ref[...] = gates
    if prompt:
        grow_ref[...] = gates.T[:SUBLANES, :]


def _inproj(x, gain, w_main, w_gate, *, tm, act_dtype, seq=None, conv=None):
    n = x.shape[0]
    dilate = seq is not None
    row = lambda i: (i, 0)
    out_shape = [jax.ShapeDtypeStruct((n, COLS_QKV_G), act_dtype)]
    out_specs = [pl.BlockSpec((tm, COLS_QKV_G), row)]
    for g in (1, 2):
        if dilate:
            d = DILATIONS[g]
            tpb = seq // tm
            out_shape.append(jax.ShapeDtypeStruct((n // seq, d, seq // d, COLS_QKV_G), act_dtype))
            out_specs.append(pl.BlockSpec((None, d, tm // d, COLS_QKV_G),
                                          lambda i, tpb=tpb: (i // tpb, 0, i % tpb, 0)))
        else:
            out_shape.append(jax.ShapeDtypeStruct((n, COLS_QKV_G), act_dtype))
            out_specs.append(pl.BlockSpec((tm, COLS_QKV_G), row))
    for cols, dt in ((B_WIDTH, act_dtype) if dilate else (COLS_QK, F32), (COLS_VO, act_dtype),
                     (COLS_GAB, act_dtype), (LANES, F32)):
        out_shape.append(jax.ShapeDtypeStruct((n, cols), dt))
        out_specs.append(pl.BlockSpec((tm, cols), row))
    in_specs = [
        pl.BlockSpec((tm, D_MODEL), row),
        _resident((1, D_MODEL)),
        _resident(w_main.shape),
        _resident(w_gate.shape),
    ]
    args = [x, gain, w_main, w_gate]
    scratch = []
    tpb = None
    if dilate:
        tpb = seq // tm
        in_specs += [_resident(conv[0].shape), _resident(conv[1].shape)]
        args += list(conv)
        out_shape.append(jax.ShapeDtypeStruct((SUBLANES, n), F32))
        out_specs.append(pl.BlockSpec((SUBLANES, tm), lambda i: (0, i)))
        out_shape.append(jax.ShapeDtypeStruct((n // seq, SUBLANES, COLS_QK), F32))
        out_specs.append(pl.BlockSpec((None, SUBLANES, COLS_QK), lambda i: (i // tpb, 0, 0)))
        out_shape.append(jax.ShapeDtypeStruct((B_WIDTH, n), act_dtype))
        out_specs.append(pl.BlockSpec((B_WIDTH, tm), lambda i: (0, i)))
        scratch =[pltpu.VMEM((D_MODEL // LANES, tm, LANES), F32),
                   pltpu.VMEM((COLS_QK // LANES, tm + SUBLANES, LANES), F32)]
    return pl.pallas_call(
        functools.partial(_inproj_kernel, prompt=dilate, tiles_per_batch=tpb),
        grid=(n // tm,),
        in_specs=in_specs,
        out_specs=out_specs,
        out_shape=out_shape,
        scratch_shapes=scratch,
        compiler_params=_params(1),
        name="inproj",
    )(*args)


def _attn_kernel(q_ref, kp_ref, kc_ref, vp_ref, vc_ref, bias_ref, o_ref, lse_ref):
    n_blk = q_ref.shape[0] // Q_BLOCK
    has_prev = pl.program_id(1) > 0
    key_lane = lax.broadcasted_iota(jnp.int32, (1, 1, 2 * Q_BLOCK), 2)
    first_mask = jnp.where((key_lane < Q_BLOCK) & jnp.logical_not(has_prev), NEG, 0.0)
    lane = lax.broadcasted_iota(jnp.int32, (Q_BLOCK, LANES), 1)
    low_half = lane < DH_A
    pair = 2 * DH_A

    def window(i, cur_ref, first_ref):
        rows = slice(i * Q_BLOCK, (i + 1) * Q_BLOCK)
        prev_ref, prows = ((first_ref, slice(0, Q_BLOCK)) if i == 0
                           else (cur_ref, slice((i - 1) * Q_BLOCK, i * Q_BLOCK)))
        return [jnp.concatenate([prev_ref[prows, hp * pair:(hp + 1) * pair],
                                 cur_ref[rows, hp * pair:(hp + 1) * pair]], axis=0)
                for hp in range(H_G // 2)]

    def qk_scores(i):
        rows = slice(i * Q_BLOCK, (i + 1) * Q_BLOCK)
        scores = []
        for hp, kk in enumerate(window(i, kc_ref, kp_ref)):
            qp = q_ref[rows, hp * pair:(hp + 1) * pair]
            scores.append(_dot_nt(jnp.where(low_half, qp, jnp.zeros_like(qp)), kk))
            scores.append(_dot_nt(jnp.where(low_half, jnp.zeros_like(qp), qp), kk))
        return jnp.stack(scores)

    for i in range(n_blk):
        rows = slice(i * Q_BLOCK, (i + 1) * Q_BLOCK)
        s = qk_scores(i) + bias_ref[...]
        if i == 0:
            s = s + first_mask
        m = jnp.max(s, axis=-1, keepdims=True)
        p = jnp.exp(s - m)
        l = jnp.sum(p, axis=-1, keepdims=True)
        pb = p.astype(BF16)
        inv = 1.0 / l
        lse = m + jnp.log(l)
        lse_all = jnp.zeros((Q_BLOCK, LANES), F32)
        for hp, vv in enumerate(window(i, vc_ref, vp_ref)):
            cols = slice(hp * pair, (hp + 1) * pair)
            o_lo = _dot(pb[2 * hp], vv) * inv[2 * hp]
            o_hi = _dot(pb[2 * hp + 1], vv) * inv[2 * hp + 1]
            o_ref[rows, cols] = jnp.where(low_half, o_lo, o_hi).astype(o_ref.dtype)
        for h in range(H_G):
            lse_all = jnp.where(lane == h, lse[h], lse_all)
        lse_ref[rows, :] = lse_all


ATTN_BLOCKS = 8


def _attn_prompt(qkv, bias):
    nsub, u_len, _ = qkv.shape
    n_blk = math.gcd(ATTN_BLOCKS, u_len // Q_BLOCK)
    rows = n_blk * Q_BLOCK
    nb = u_len // rows

    def spec(col_block, prev):
        if prev:
            return pl.BlockSpec((None, Q_BLOCK, A_GROUP),
                                lambda s, j: (s, jnp.maximum(j * n_blk - 1, 0), col_block))
        return pl.BlockSpec((None, rows, A_GROUP), lambda s, j: (s, j, col_block))

    return pl.pallas_call(
        _attn_kernel,
        grid=(nsub, nb),
        in_specs=[spec(0, False), spec(1, True), spec(1, False), spec(2, True), spec(2, False),
                  _resident(bias.shape)],
        out_specs=[
            pl.BlockSpec((None, rows, A_GROUP), lambda s, j: (s, j, 0)),
            pl.BlockSpec((None, rows, LANES), lambda s, j: (s, j, 0)),
        ],
        out_shape=[
            jax.ShapeDtypeStruct((nsub, u_len, A_GROUP), BF16),
            jax.ShapeDtypeStruct((nsub, u_len, LANES), F32),
        ],
        compiler_params=_params(2),
        name="attn_prompt",
    )(qkv, qkv, qkv, qkv, qkv, bias)


def _kvtail_kernel(*refs):
    out_ref, nat_s = refs[-2:]
    srcs = refs[:-2]
    n_slab = nat_s.shape[0]
    for layer in range(len(srcs) // 2):
        @pl.when(pl.program_id(0) == layer)
        def _():
            for j, ref in enumerate(srcs[2 * layer:2 * layer + 2]):
                dil = ref.shape[0]
                for r in range(dil):
                    val = ref[r].astype(F32)
                    for s in range(n_slab):
                        nat_s[s, pl.ds(r, SPAN, stride=dil), :] = val[:, s * LANES:(s + 1) * LANES]
                for s in range(n_slab):
                    out_ref[j, s * LANES:(s + 1) * LANES, :] = nat_s[s].T


def _kv_tail(qkv_layers, g):
    depth = len(qkv_layers)
    batch, dil, u_len, _ = qkv_layers[0].shape
    keep = SPAN * dil
    last_blk = u_len // SPAN - 1
    specs, args = [], []
    for arr in qkv_layers:
        for col in (1, 2):
            specs.append(pl.BlockSpec((None, dil, SPAN, A_GROUP),
                                      lambda l, b, col=col: (b, 0, last_blk, col)))
            args.append(arr)
    return pl.pallas_call(
        _kvtail_kernel,
        grid=(depth, batch),
        in_specs=specs,
        out_specs=pl.BlockSpec((None, None, 2, A_GROUP, keep), lambda l, b: (l, b, 0, 0, 0)),
        out_shape=jax.ShapeDtypeStruct((depth, batch, 2, A_GROUP, keep), F32),
        scratch_shapes=[pltpu.VMEM((A_GROUP // LANES, keep, LANES), F32)],
        compiler_params=_params(2),
        name=f"kv_tail_g{g}",
    )(*args)


def _sattn_kernel(new_ref, kv0_ref, kv1_ref, kv2_ref, bt0_ref, bt1_ref, bt2_ref, b0_ref,
                  o_ref, lse_ref):
    eye = (lax.broadcasted_iota(jnp.int32, (DH_A, DH_A), 0)
           == lax.broadcasted_iota(jnp.int32, (DH_A, DH_A), 1))
    for g, (kv_ref, bt_ref) in enumerate(((kv0_ref, bt0_ref), (kv1_ref, bt1_ref),
                                          (kv2_ref, bt2_ref))):
        q = new_ref[g, 0]
        k_new = new_ref[g, 1]
        v_new = new_ref[g, 2]
        s_rows = []
        for h in range(H_G):
            q_col = jnp.sum(jnp.where(eye, q[h:h + 1, :], 0.0), axis=1, keepdims=True)
            s_rows.append(jnp.sum(kv_ref[0, h] * q_col, axis=0, keepdims=True))
        s = jnp.concatenate(s_rows, axis=0) + bt_ref[...]
        s0 = jnp.sum(k_new * q, axis=1, keepdims=True) + b0_ref[g]
        m = jnp.maximum(jnp.max(s, axis=1, keepdims=True), s0)
        p = jnp.exp(s - m[:, 0:1])
        p0 = jnp.exp(s0 - m)
        l = jnp.sum(p, axis=1, keepdims=True) + p0
        o_rows = []
        for h in range(H_G):
            o_col = jnp.sum(kv_ref[1, h] * p[h:h + 1, :], axis=1, keepdims=True)
            o_rows.append(jnp.sum(jnp.where(eye, o_col, 0.0), axis=0, keepdims=True))
        o_ref[g] = (jnp.concatenate(o_rows, axis=0) + p0 * v_new) / l
        lse_ref[g] = m + jnp.log(l)


def _attn_sample(new_qkv, caches, layer, bias_t, bias_0):
    nreq = new_qkv.shape[0]
    views = []
    specs = []
    for g, cache in enumerate(caches):
        n_buf = cache.shape[2]
        assert n_buf == SPAN * DILATIONS[g], "cache must hold exactly one window"
        views.append(cache.transpose(0, 1, 3, 4, 5, 2))
        specs.append(pl.BlockSpec((None, None, 2, H_G, DH_A, n_buf),
                                  lambda b, layer=layer: (layer, b, 0, 0, 0, 0)))
    out = jax.ShapeDtypeStruct((nreq, N_GROUPS, H_G, DH_A), F32)
    out_spec = pl.BlockSpec((None, N_GROUPS, H_G, DH_A), lambda b: (b, 0, 0, 0))
    return pl.pallas_call(
        _sattn_kernel,
        grid=(nreq,),
        in_specs=[pl.BlockSpec((None, N_GROUPS, 3, H_G, DH_A), lambda b: (b, 0, 0, 0, 0))] + specs
        + [_resident(t.shape) for t in bias_t] + [_resident(bias_0.shape)],
        out_specs=[out_spec, out_spec],
        out_shape=[out, out],
        compiler_params=_params(1),
        name="attn_sample",
    )(new_qkv, *views, *bias_t, bias_0)


def _mlstm_kernel(*refs, n_seq):
    q_ref = refs[0]
    kt_refs = refs[1:1 + n_seq]
    vo_ref, gcol_ref = refs[1 + n_seq:3 + n_seq]
    grow_refs = refs[3 + n_seq:3 + 2 * n_seq]
    gb_row_ref, gb_col_ref, bout_ref, c_ref, n_ref, m_ref, nrep_s = refs[3 + 2 * n_seq:]
    chunk = q_ref.shape[1]

    @pl.when(pl.program_id(1) == 0)
    def _():
        c_ref[...] = jnp.zeros_like(c_ref)
        nrep_s[...] = jnp.zeros_like(nrep_s)
        m_ref[...] = jnp.zeros_like(m_ref)

    ti = lax.broadcasted_iota(jnp.int32, (chunk, chunk), 0)
    si = lax.broadcasted_iota(jnp.int32, (chunk, chunk), 1)
    causal = ti >= si
    lower = causal.astype(BF16)
    upper = (si >= ti).astype(BF16)
    items = [(j, h) for j in range(n_seq) for h in range(NH_B)]
    hcol = lambda h: slice(h * DK_B, (h + 1) * DK_B)
    a_rows, b_ts, b_lasts = [], [], []
    for j in range(n_seq):
        z_col = gcol_ref[j] + gb_row_ref[...]
        z_row = grow_refs[j][...] + gb_col_ref[...]
        b_col = sum(_dot(lower, part) for part in _split3(_log_sigmoid(z_col)))
        b_row = sum(_dot(part, upper) for part in _split3(_log_sigmoid(z_row)))
        for h in range(NH_B):
            a_rows.append(z_row[h:h + 1, :] - b_row[NH_B + h:NH_B + h + 1, :])
            b_ts.append(b_col[:, NH_B + h:NH_B + h + 1])
            b_lasts.append(b_row[NH_B + h:NH_B + h + 1, chunk - 1:chunk])
    a_row = jnp.stack(a_rows)
    b_t = jnp.stack(b_ts)
    m_prev = jnp.stack([m_ref[j, :, h:h + 1] for j, h in items])
    a_mat = jnp.where(causal, a_row, NEG)
    gmax = jnp.maximum(m_prev, jnp.max(a_mat, axis=-1, keepdims=True))
    dw = jnp.exp(a_mat - gmax)
    iw = jnp.exp(m_prev - gmax)
    g_last = gmax[:, chunk - 1:chunk, :]
    decay = jnp.exp(m_prev - g_last)
    w_state = jnp.exp(a_row - g_last)

    qb = [q_ref[j, :, hcol(h)] for j, h in items]
    kt = [kt_refs[j][hcol(h), :] for j, h in items]
    vb = [vo_ref[j, :, hcol(h)] for j, h in items]
    c_old = [c_ref[j, h] for j, h in items]
    n_old = [nrep_s[j, h] for j, h in items]
    idx = range(len(items))
    qk = jnp.stack([_dot(qb[i], kt[i]) for i in idx]) * dw
    qkb = qk.astype(BF16)
    q_c = jnp.stack([_dot(qb[i], c_old[i].astype(BF16)) for i in idx])
    qk_v = jnp.stack([_dot(qkb[i], vb[i]) for i in idx])
    num = iw * q_c + qk_v
    q_n = jnp.stack([_dot(qb[i], n_old[i].astype(BF16))[:, 0:1] for i in idx])
    den = iw * q_n + jnp.sum(qk, axis=-1, keepdims=True)
    hid = num / jnp.maximum(jnp.abs(den), jnp.exp(-(b_t + gmax)))
    for i, (j, h) in enumerate(items):
        o_gate = _sigmoid(vo_ref[j, :, B_WIDTH + h * DK_B:B_WIDTH + (h + 1) * DK_B].astype(F32))
        bout_ref[j, :, hcol(h)] = (o_gate * hid[i]).astype(bout_ref.dtype)

    kwt = (jnp.stack(kt).astype(F32) * w_state).astype(BF16)
    ones = jnp.ones((chunk, LANES), BF16)
    for i, (j, h) in enumerate(items):
        c_ref[j, h] = decay[i] * c_old[i] + _dot(kwt[i], vb[i])
        nrep_s[j, h] = decay[i] * n_old[i] + _dot(kwt[i], ones)
        m_ref[j, :, h:h + 1] = b_lasts[i] + g_last[i]

    @pl.when(pl.program_id(1) == pl.num_programs(1) - 1)
    def _():
        for j, h in items:
            n_ref[j, h:h + 1, :] = nrep_s[j, h].T[0:1, :]


def _mlstm_prompt(q, kt, vo, gcol, grow, gb_row, gb_col, batch, seq):
    chunk = MLSTM_CHUNK
    n_seq = math.gcd(MLSTM_SEQS, batch)
    nc = seq // chunk
    seq3 = lambda b, c: (b, c, 0)
    lanes = [lambda b, c, j=j: (0, (b * n_seq + j) * nc + c) for j in range(n_seq)]
    state = lambda b, c: (b, 0, 0)
    return pl.pallas_call(
        functools.partial(_mlstm_kernel, n_seq=n_seq),
        grid=(batch // n_seq, nc),
        in_specs=[pl.BlockSpec((n_seq, chunk, B_WIDTH), seq3)]
        + [pl.BlockSpec((B_WIDTH, chunk), lanes[j]) for j in range(n_seq)]
        + [pl.BlockSpec((n_seq, chunk, COLS_VO), seq3), pl.BlockSpec((n_seq, chunk, LANES), seq3)]
        + [pl.BlockSpec((SUBLANES, chunk), lanes[j]) for j in range(n_seq)]
        + [_resident(gb_row.shape), _resident(gb_col.shape)],
        out_specs=[
            pl.BlockSpec((n_seq, chunk, B_WIDTH), seq3),
            pl.BlockSpec((n_seq, NH_B, DK_B, DK_B), lambda b, c: (b, 0, 0, 0)),
            pl.BlockSpec((n_seq, NH_B, DK_B), state),
            pl.BlockSpec((n_seq, 1, LANES), state),
        ],
        out_shape=[
            jax.ShapeDtypeStruct((batch, seq, B_WIDTH), BF16),
            jax.ShapeDtypeStruct((batch, NH_B, DK_B, DK_B), F32),
            jax.ShapeDtypeStruct((batch, NH_B, DK_B), F32),
            jax.ShapeDtypeStruct((batch, 1, LANES), F32),
        ],
        scratch_shapes=[pltpu.VMEM((n_seq, NH_B, DK_B, LANES), F32)],
        compiler_params=_params(2),
        name="mlstm_prompt",
    )(q.reshape(batch, seq, B_WIDTH), *([kt] * n_seq), vo.reshape(batch, seq, COLS_VO),
      gcol.reshape(batch, seq, LANES), *([grow] * n_seq), gb_row, gb_col)


def _smlstm_kernel(qk_ref, vo_ref, g_ref, gb_ref, cs_ref, cw_ref, cb_ref, c0_ref, n0_ref, m0_ref,
                   bout_ref, c1_ref, n1_ref, m1_ref, ncs_ref):
    u = qk_ref[...]
    y = cb_ref[...] + u * cw_ref[CONV_B - 1:CONV_B, :]
    for i in range(CONV_B - 1):
        y = y + cs_ref[i:i + 1, :] * cw_ref[i:i + 1, :]
    ncs_ref[0:CONV_B - 2, :] = cs_ref[1:CONV_B - 1, :]
    ncs_ref[CONV_B - 2:CONV_B - 1, :] = u
    y = y * _sigmoid(y)
    qf = y[:, :B_WIDTH]
    kf = y[:, B_WIDTH:] * (DK_B ** -0.5)
    z = g_ref[...] + gb_ref[...]
    lf_all = _log_sigmoid(z)
    m0 = m0_ref[...]
    eye = (lax.broadcasted_iota(jnp.int32, (DK_B, DK_B), 0)
           == lax.broadcasted_iota(jnp.int32, (DK_B, DK_B), 1))
    for h in range(NH_B):
        hs = slice(h * DK_B, (h + 1) * DK_B)
        ig = z[:, h:h + 1]
        inter = lf_all[:, NH_B + h:NH_B + h + 1] + m0[:, h:h + 1]
        m_t = jnp.maximum(inter, ig)
        dw = jnp.exp(ig - m_t)
        iw = jnp.exp(inter - m_t)
        qh = qf[:, hs]
        kh = kf[:, hs]
        vh = vo_ref[:, hs]
        c_old = c0_ref[h]
        n_old = n0_ref[h:h + 1, :]
        qk = jnp.sum(qh * kh, axis=1, keepdims=True) * dw
        num = iw * _hdot(qh, c_old) + qk * vh
        den = iw * jnp.sum(qh * n_old, axis=1, keepdims=True) + qk
        hid = num / jnp.maximum(jnp.abs(den), jnp.exp(-m_t))
        o_gate = _sigmoid(vo_ref[:, B_WIDTH + h * DK_B:B_WIDTH + (h + 1) * DK_B])
        bout_ref[:, hs] = o_gate * hid
        k_col = jnp.sum(jnp.where(eye, kh, 0.0), axis=1, keepdims=True)
        c1_ref[h] = iw * c_old + (dw * k_col) * vh
        n1_ref[h:h + 1, :] = iw * n_old + dw * kh
        m1_ref[:, h:h + 1] = m_t


def _mlstm_sample(qk_pre, vo, gcol, gb_row, conv_state, cw, cb, c0, n0, m0, layer):
    nreq = qk_pre.shape[0]
    one = lambda b: (b, 0, 0)
    lay3 = lambda b, layer=layer: (layer, b, 0, 0)
    lay4 = lambda b, layer=layer: (layer, b, 0, 0, 0)
    return pl.pallas_call(
        _smlstm_kernel,
        grid=(nreq,),
        in_specs=[
            pl.BlockSpec((None, 1, COLS_QK), one),
            pl.BlockSpec((None, 1, COLS_VO), one),
            pl.BlockSpec((None, 1, LANES), one),
            _resident(gb_row.shape),
            pl.BlockSpec((None, None, CONV_B - 1, COLS_QK), lay3),
            _resident(cw.shape), _resident(cb.shape),
            pl.BlockSpec((None, None, NH_B, DK_B, DK_B), lay4),
            pl.BlockSpec((None, None, NH_B, DK_B), lay3),
            pl.BlockSpec((None, None, 1, LANES), lay3),
        ],
        out_specs=[
            pl.BlockSpec((None, 1, B_WIDTH), one),
            pl.BlockSpec((None, NH_B, DK_B, DK_B), lambda b: (b, 0, 0, 0)),
            pl.BlockSpec((None, NH_B, DK_B), one),
            pl.BlockSpec((None, 1, LANES), one),
            pl.BlockSpec((None, CONV_B - 1, COLS_QK), one),
        ],
        out_shape=[
            jax.ShapeDtypeStruct((nreq, 1, B_WIDTH), F32),
            jax.ShapeDtypeStruct((nreq, NH_B, DK_B, DK_B), F32),
            jax.ShapeDtypeStruct((nreq, NH_B, DK_B), F32),
            jax.ShapeDtypeStruct((nreq, 1, LANES), F32),
            jax.ShapeDtypeStruct((nreq, CONV_B - 1, COLS_QK), F32),
        ],
        compiler_params=_params(1),
        name="mlstm_sample",
    )(qk_pre.reshape(nreq, 1, COLS_QK), vo.reshape(nreq, 1, COLS_VO),
      gcol.reshape(nreq, 1, LANES), gb_row, conv_state, cw, cb, c0, n0, m0)


def _merge_kernel(o0_ref, o1_ref, o2_ref, l0_ref, l1_ref, l2_ref, bo_ref, gab_ref, x_ref,
                  wpa_ref, wpb_ref, wo_ref, g2_ref, e_ref, x2_ref, h2_ref, *scratch):
    tm = x_ref.shape[0]

    def natural(ref, buf):
        dil, rows, _ = ref.shape
        n_slab = buf.shape[0]
        for r in range(dil):
            val = ref[r].astype(F32)
            for k in range(n_slab):
                buf[k, pl.ds(r, rows, stride=dil), :] = val[:, k * LANES:(k + 1) * LANES]
        return jnp.concatenate([buf[k] for k in range(n_slab)], axis=1)

    if scratch:
        os_ = [o0_ref[...].astype(F32), natural(o1_ref, scratch[0]), natural(o2_ref, scratch[1])]
        lses = [l0_ref[...], natural(l1_ref, scratch[2]), natural(l2_ref, scratch[3])]
    else:
        os_ = [r[...].astype(F32) for r in (o0_ref, o1_ref, o2_ref)]
        lses = [r[...] for r in (l0_ref, l1_ref, l2_ref)]
    top = jnp.maximum(jnp.maximum(lses[0], lses[1]), lses[2])
    ws = [jnp.exp(l - top) for l in lses]
    inv = 1.0 / (ws[0] + ws[1] + ws[2])
    e2 = e_ref[...]
    a = None
    for w, o in zip(ws, os_):
        alpha = w * inv
        hi = alpha.astype(BF16)
        lo = (alpha - hi.astype(F32)).astype(BF16)
        term = _dot(jnp.concatenate([hi, lo], axis=1), e2) * o
        a = term if a is None else a + term
    pa = _dot(a.astype(BF16), wpa_ref[...])
    pb = _dot(bo_ref[...].astype(BF16), wpb_ref[...])
    merged = (_sigmoid(gab_ref[:, :D_MODEL].astype(F32)) * pa
              + _sigmoid(gab_ref[:, D_MODEL:].astype(F32)) * pb)
    x2 = x_ref[...] + _dot(merged.astype(BF16), wo_ref[...])
    x2_ref[...] = x2
    h2_ref[...] = _rms(x2, g2_ref[...]).astype(BF16)


def _merge(os_, lses, bout, gab, x, wpa, wpb, wo, g2, e_bf, *, layer, tm, seq=None):
    n = x.shape[0]
    row = lambda i: (i, 0)

    def group_specs(cols):
        specs = [pl.BlockSpec((tm, cols), row)]
        for g in (1, 2):
            if seq is None:
                specs.append(pl.BlockSpec((tm, cols), row))
            else:
                d = DILATIONS[g]
                tpb = seq // tm
                specs.append(pl.BlockSpec((None, d, tm // d, cols),
                                          lambda i, tpb=tpb: (i // tpb, 0, i % tpb, 0)))
        return specs

    scratch = []
    if seq is not None:
        scratch = ([pltpu.VMEM((A_GROUP // LANES, tm, LANES), F32)] * 2
                   + [pltpu.VMEM((1, tm, LANES), F32)] * 2)
    return pl.pallas_call(
        _merge_kernel,
        grid=(n // tm,),
        scratch_shapes=scratch,
        in_specs=group_specs(A_GROUP) + group_specs(LANES) + [
            pl.BlockSpec((tm, B_WIDTH), row),
            pl.BlockSpec((tm, COLS_GAB), row),
            pl.BlockSpec((tm, D_MODEL), row),
            _layer_resident(wpa.shape, layer), _layer_resident(wpb.shape, layer),
            _layer_resident(wo.shape, layer), _resident(g2.shape), _resident(e_bf.shape),
        ],
        out_specs=[pl.BlockSpec((tm, D_MODEL), row), pl.BlockSpec((tm, D_MODEL), row)],
        out_shape=[jax.ShapeDtypeStruct((n, D_MODEL), F32), jax.ShapeDtypeStruct((n, D_MODEL), BF16)],
        compiler_params=_params(1),
        name="merge",
    )(*os_, *lses, bout, gab, x, wpa, wpb, wo, g2, e_bf)


def _ffn_kernel(*refs, tiles_per_batch, from_state, final_norm):
    h2_ref, x2_ref, wup_ref, wdn_ref, cw_ref, cb_ref = refs[:6]
    pos = 6
    if from_state:
        prev_refs = refs[pos:pos + CONV_F - 1]
        pos += CONV_F - 1
    if final_norm:
        fg_ref = refs[pos]
        pos += 1
    x3_ref, u_ref = refs[pos:pos + 2]
    tm = h2_ref.shape[0]
    pad = SUBLANES
    if not from_state:
        ext_s = refs[pos + 2]

        @pl.when(pl.program_id(0) % tiles_per_batch == 0)
        def _():
            ext_s[:, 0:pad, :] = jnp.zeros((ext_s.shape[0], pad, LANES), F32)

    h2 = h2_ref[...]
    acc = x2_ref[...]
    n_chunk = D_FF // FF_CHUNK

    def halves(c):
        return (c * FF_CHUNK, D_FF + c * FF_CHUNK)

    def up(c):
        us = []
        for off in halves(c):
            u = _dot(h2, wup_ref[:, off:off + FF_CHUNK])
            if from_state:
                u_ref[:, off:off + FF_CHUNK] = u
                us.append(u)
            else:
                for k in range(FF_CHUNK // LANES):
                    ext_s[off // LANES + k, pad:pad + tm, :] = u[:, k * LANES:(k + 1) * LANES]
        return us

    def conv(c, us):
        ys = []
        for j, off in enumerate(halves(c)):
            cs = slice(off, off + FF_CHUNK)
            if from_state:
                y = cb_ref[:, cs] + us[j] * cw_ref[CONV_F - 1:CONV_F, cs]
                for i in range(CONV_F - 1):
                    y = y + prev_refs[i][:, cs] * cw_ref[i:i + 1, cs]
            else:
                parts = []
                for k in range(FF_CHUNK // LANES):
                    slab = off // LANES + k
                    ks = slice(off + k * LANES, off + (k + 1) * LANES)
                    part = cb_ref[:, ks]
                    for i in range(CONV_F):
                        lo = pad - (CONV_F - 1) + i
                        part = part + ext_s[slab, lo:lo + tm, :] * cw_ref[i:i + 1, ks]
                    parts.append(part)
                    tail = ext_s[slab, tm:tm + pad, :]
                    u_ref[:, ks] = tail
                    ext_s[slab, 0:pad, :] = tail
                y = jnp.concatenate(parts, axis=1)
            ys.append(y)
        return ys

    ahead = 1 if from_state else FF_AHEAD
    pending = [up(c) for c in range(min(ahead, n_chunk))]
    for c in range(n_chunk):
        us = pending.pop(0)
        if c + ahead < n_chunk:
            pending.append(up(c + ahead))
        y_act, y_gate = conv(c, us)
        act = _gelu_tanh(y_act) * y_gate
        acc = acc + _dot(act.astype(BF16), wdn_ref[c * FF_CHUNK:(c + 1) * FF_CHUNK, :])
    if final_norm:
        acc = _rms(acc, fg_ref[...])
    x3_ref[...] = acc


def _ffn(h2, x2, wup, wdn, cw, cb, *, layer, tm, tiles_per_batch=None, prev_rows=None,
         final_g=None):
    n = h2.shape[0]
    from_state = prev_rows is not None
    row = lambda i: (i, 0)
    in_specs = [
        pl.BlockSpec((tm, D_MODEL), row), pl.BlockSpec((tm, D_MODEL), row),
        _layer_resident(wup.shape, layer), _layer_resident(wdn.shape, layer),
        _resident(cw.shape), _resident(cb.shape),
    ]
    args = [h2, x2, wup, wdn, cw, cb]
    scratch = []
    if from_state:
        in_specs += [pl.BlockSpec((tm, 2 * D_FF), row)] * (CONV_F - 1)
        args += list(prev_rows)
        u_shape = jax.ShapeDtypeStruct((n, 2 * D_FF), F32)
        u_spec = pl.BlockSpec((tm, 2 * D_FF), row)
    else:
        n_batch = n // (tm * tiles_per_batch)
        u_shape = jax.ShapeDtypeStruct((n_batch, SUBLANES, 2 * D_FF), F32)
        u_spec = pl.BlockSpec((None, SUBLANES, 2 * D_FF), lambda i: (i // tiles_per_batch, 0, 0))
        scratch = [pltpu.VMEM((2 * D_FF // LANES, tm + SUBLANES, LANES), F32)]
    if final_g is not None:
        in_specs.append(_resident(final_g.shape))
        args.append(final_g)
    return pl.pallas_call(
        functools.partial(_ffn_kernel, tiles_per_batch=tiles_per_batch, from_state=from_state,
                          final_norm=final_g is not None),
        grid=(n // tm,),
        in_specs=in_specs,
        out_specs=[pl.BlockSpec((tm, D_MODEL), row), u_spec],
        out_shape=[jax.ShapeDtypeStruct((n, D_MODEL), F32), u_shape],
        scratch_shapes=scratch,
        compiler_params=_params(1),
        name="ffn",
    )(*args)


def _t5_bucket(dist):
    max_exact = NUM_BUCKETS // 2
    df = jnp.maximum(dist, 1).astype(F32)
    large = max_exact + (jnp.log(df / max_exact) / math.log(MAX_DISTANCE / max_exact)
                         * (NUM_BUCKETS - max_exact)).astype(jnp.int32)
    large = jnp.minimum(large, NUM_BUCKETS - 1)
    return jnp.where(dist < max_exact, dist, large)


def _bias_table(rel_bias, g, dist):
    bucket = _t5_bucket(jnp.asarray(dist, jnp.int32))
    table = rel_bias[:, g * H_G:(g + 1) * H_G].reshape((NUM_BUCKETS, H_G) + (1,) * bucket.ndim)
    ids = jnp.arange(NUM_BUCKETS).reshape((NUM_BUCKETS, 1) + (1,) * bucket.ndim)
    return jnp.sum(jnp.where(bucket[None, None] == ids, table, 0.0), axis=0)


def _prompt_bias(rel_bias, g):
    qi = np.arange(Q_BLOCK)[:, None]
    ki = np.arange(2 * Q_BLOCK)[None, :]
    rel = qi + Q_BLOCK - ki
    band = (rel >= 0) & (rel <= SPAN)
    bias = _bias_table(rel_bias, g, np.maximum(rel, 0) * DILATIONS[g])
    return jnp.where(band[None], bias, NEG)


def _sample_bias(rel_bias):
    tables, news = [], []
    for g in range(N_GROUPS):
        dil = DILATIONS[g]
        n_buf = SPAN * dil
        pos = np.arange(n_buf)
        bias = _bias_table(rel_bias, g, n_buf - pos)
        tables.append(jnp.where((pos % dil == 0)[None], bias, NEG))
        news.append(jnp.broadcast_to(_bias_table(rel_bias, g, np.zeros((1,), np.int32)), (H_G, DH_A)))
    return tables, jnp.stack(news)


def _head_indicator():
    e = np.zeros((LANES, A_GROUP), np.float32)
    for h in range(H_G):
        e[h, h * DH_A:(h + 1) * DH_A] = 1.0
    return e


def kernel(x_prompt, x_sample, cache_kv_w128, cache_kv_w512, cache_kv_w2048, state_mlstm_conv,
           state_mlstm_C, state_mlstm_n, state_mlstm_m, state_ffn_conv, rel_bias, norm1_g, w_in,
           mconv_w, mconv_b, mgate_b, w_pa, w_pb, w_o, norm2_g, w_up, fconv_w, fconv_b, w_down,
           final_norm_g):
    batch, seq, _ = x_prompt.shape
    nreq = x_sample.shape[0]
    depth = w_in.shape[0]
    n_p = batch * seq
    caches = (cache_kv_w128, cache_kv_w512, cache_kv_w2048)

    assert seq % (Q_BLOCK * DILATIONS[-1]) == 0 and seq >= WINDOWS[-1]
    e_bf = jnp.asarray(np.concatenate([_head_indicator()] * 2, axis=0), BF16)
    prompt_bias = [_prompt_bias(rel_bias, g) for g in range(N_GROUPS)]
    sbias_m, sbias_0 = _sample_bias(rel_bias)
    m0_all = jnp.pad(state_mlstm_m, ((0, 0), (0, 0), (0, LANES - NH_B)))[:, :, None, :]
    fg = final_norm_g.reshape(1, D_MODEL)

    wpa, wpb, wo = w_pa.astype(BF16), w_pb.astype(BF16), w_o.astype(BF16)
    wup, wdn = w_up.astype(BF16), w_down.astype(BF16)

    xp = x_prompt.reshape(n_p, D_MODEL)
    xs = x_sample.reshape(nreq, D_MODEL)
    p_st = [[] for _ in range(8)]
    s_st = [[] for _ in range(8)]

    for l in range(depth):
        last = l == depth - 1
        w_main, w_gate = _regroup_w_in(w_in, l)
        g1 = norm1_g[l].reshape(1, D_MODEL)
        g2 = norm2_g[l].reshape(1, D_MODEL)
        gate_bias = mgate_b[l].reshape(1, 2 * NH_B)
        gb_row = jnp.pad(gate_bias, ((0, 0), (0, LANES - 2 * NH_B)))
        gb_col = gate_bias.reshape(2 * NH_B, 1)
        mcw, mcb = mconv_w[l], mconv_b[l].reshape(1, COLS_QK)
        fcw, fcb = fconv_w[l], fconv_b[l].reshape(1, 2 * D_FF)

        *qkvs, q_b, vo, gab, gcol, grow, conv_tail, kt_b = _inproj(
            xp, g1, w_main, w_gate, tm=512, act_dtype=BF16, seq=seq, conv=(mcw, mcb))
        os_, lses = [], []
        for g in range(N_GROUPS):
            d = DILATIONS[g]
            o_g, lse_g = _attn_prompt(qkvs[g].reshape(batch * d, seq // d, COLS_QKV_G), prompt_bias[g])
            shape = (n_p,) if g == 0 else (batch, d, seq // d)
            os_.append(o_g.reshape(shape + (A_GROUP,)))
            lses.append(lse_g.reshape(shape + (LANES,)))
        bout, c_p, n_p_state, m_p = _mlstm_prompt(q_b, kt_b, vo, gcol, grow, gb_row, gb_col,
                                                  batch, seq)
        x2, h2 = _merge(os_, lses, bout.reshape(n_p, B_WIDTH), gab, xp, wpa, wpb, wo, g2, e_bf,
                        layer=l, tm=512, seq=seq)
        xp, u_tail = _ffn(h2, x2, wup, wdn, fcw, fcb, layer=l, tm=256, tiles_per_batch=seq // 256,
                          final_g=fg if last else None)

        for g in range(N_GROUPS):
            p_st[g].append(qkvs[g].reshape(batch, DILATIONS[g], seq // DILATIONS[g], COLS_QKV_G))
        p_st[3].append(conv_tail[:, SUBLANES - (CONV_B - 1):])
        p_st[4].append(c_p)
        p_st[5].append(n_p_state)
        p_st[6].append(m_p[:, 0, :NH_B])
        p_st[7].append(u_tail[:, SUBLANES - (CONV_F - 1):])

        *qkvs_s, qk_s, vo_s, gab_s, gcol_s = _inproj(xs, g1, w_main, w_gate, tm=nreq, act_dtype=F32)
        new_qkv = jnp.stack(qkvs_s, axis=1).reshape(nreq, N_GROUPS, 3, H_G, DH_A)
        o_s, lse_s = _attn_sample(new_qkv, caches, l, sbias_m, sbias_0)
        bout_s, c_s, n_s, m_s, conv_s = _mlstm_sample(qk_s, vo_s, gcol_s, gb_row, state_mlstm_conv,
                                                      mcw, mcb, state_mlstm_C, state_mlstm_n, m0_all, l)
        lse_pad = jnp.pad(lse_s[..., 0], ((0, 0), (0, 0), (0, LANES - H_G)))
        x2_s, h2_s = _merge([o_s[:, g].reshape(nreq, A_GROUP) for g in range(N_GROUPS)],
                            [lse_pad[:, g] for g in range(N_GROUPS)],
                            bout_s.reshape(nreq, B_WIDTH), gab_s, xs, wpa, wpb, wo, g2, e_bf,
                            layer=l, tm=nreq)
        fbuf = state_ffn_conv[l]
        xs, u_s = _ffn(h2_s, x2_s, wup, wdn, fcw, fcb, layer=l, tm=nreq,
                       prev_rows=[fbuf[:, i] for i in range(CONV_F - 1)],
                       final_g=fg if last else None)

        for g in range(N_GROUPS):
            s_st[g].append(new_qkv[:, g, 1:][:, None])
        s_st[3].append(conv_s)
        s_st[4].append(c_s)
        s_st[5].append(n_s)
        s_st[6].append(m_s[:, 0, :NH_B])
        s_st[7].append(jnp.concatenate([fbuf[:, 1:], u_s[:, None, :]], axis=1))

    outs = [xp.reshape(batch, seq, D_MODEL), xs.reshape(nreq, 1, D_MODEL)]
    for i in range(8):
        if i < N_GROUPS:
            kv = _kv_tail(p_st[i], i).reshape(depth, batch, 2, H_G, DH_A, SPAN * DILATIONS[i])
            outs.append(kv.transpose(0, 1, 5, 2, 3, 4))
        else:
            outs.append(jnp.stack(p_st[i], 0))
        outs.append(jnp.stack(s_st[i], 0))
    return tuple(outs)
```

```python
import functools
import math

import numpy as np
import jax
import jax.numpy as jnp
from jax import lax
from jax.experimental import pallas as pl
from jax.experimental.pallas import tpu as pltpu

F32 = jnp.float32
BF16 = jnp.bfloat16
HIGHEST = lax.Precision.HIGHEST

D_MODEL = 1024
WINDOWS = (128, 512, 2048)
DILATIONS = (1, 4, 16)
N_GROUPS = 3
H_G = 8
DH_A = 64
A_GROUP = H_G * DH_A
Q_BLOCK = 128
SPAN = 128
NH_B = 4
DK_B = 256
B_WIDTH = NH_B * DK_B
CONV_B = 4
D_FF = 2816
CONV_F = 3
NUM_BUCKETS = 32
MAX_DISTANCE = 2048
RMS_EPS = 1e-6
NEG = -1e30

LANES = 128
SUBLANES = 8
FF_CHUNK = 256
PROJ_CHUNK = 512
FF_AHEAD = 3
MLSTM_CHUNK = 256
MLSTM_SEQS = 2
ATTN_BLOCKS = 8
VMEM_LIMIT = 56 * 1024 * 1024
INPROJ_ROWS = 512
MERGE_ROWS = 512
FFN_ROWS = 256

COLS_QKV_G = 3 * A_GROUP
COLS_QKV = N_GROUPS * COLS_QKV_G
COLS_QK = 2 * B_WIDTH
COLS_VO = 2 * B_WIDTH
COLS_GAB = 2 * D_MODEL
GATE_COL0 = COLS_QKV + COLS_QK + COLS_VO


def _dot(a, b):
    return jnp.dot(a, b, preferred_element_type=F32)


def _hdot(a, b):
    return jnp.dot(a, b, precision=HIGHEST, preferred_element_type=F32)


def _dot_nt(a, b):
    return lax.dot_general(a, b, (((1,), (1,)), ((), ())), preferred_element_type=F32)


def _split3(x):
    hi = x.astype(BF16)
    rest = x - hi.astype(F32)
    mid = rest.astype(BF16)
    return hi, mid, (rest - mid.astype(F32)).astype(BF16)


def _sigmoid(x):
    return 1.0 / (1.0 + jnp.exp(-x))


def _log_sigmoid(x):
    return jnp.minimum(x, 0.0) - jnp.log1p(jnp.exp(-jnp.abs(x)))


def _gelu_tanh(x):
    return 0.5 * x * (1.0 + jnp.tanh(math.sqrt(2.0 / math.pi) * (x + 0.044715 * (x * x * x))))


def _rms(x, g):
    return x * lax.rsqrt(jnp.mean(x * x, axis=-1, keepdims=True) + RMS_EPS) * g


def _resident(shape):
    nd = len(shape)
    return pl.BlockSpec(shape, lambda *_: (0,) * nd, pipeline_mode=pl.Buffered(1))


def _layer_resident(stacked_shape, layer):
    nd = len(stacked_shape)
    return pl.BlockSpec((None,) + tuple(stacked_shape[1:]), lambda *_: (layer,) + (0,) * (nd - 1),
                        pipeline_mode=pl.Buffered(1))


def _params(n_grid):
    return pltpu.CompilerParams(dimension_semantics=("arbitrary",) * n_grid,
                                vmem_limit_bytes=VMEM_LIMIT)


W_BLOCK = 512
N_QKV_BLOCKS = COLS_QKV // W_BLOCK
N_HEAD_BLOCKS = GATE_COL0 // W_BLOCK


def _regroup_kernel(head_ref, tail_ref, gate_ref, o_ref, og_ref):
    j = pl.program_id(0)

    @pl.when(j == 0)
    def _():
        lane = lax.broadcasted_iota(jnp.int32, og_ref.shape, 1)
        og_ref[...] = jnp.where(lane < 2 * NH_B, gate_ref[...].T, 0.0).astype(og_ref.dtype)

    @pl.when(j < N_HEAD_BLOCKS)
    def _():
        is_q = (j < N_QKV_BLOCKS) & (j % 3 == 0)
        scale = jnp.where(is_q, DH_A ** -0.5, 1.0)
        o_ref[...] = (head_ref[...] * scale).T.astype(o_ref.dtype)

    @pl.when(j >= N_HEAD_BLOCKS)
    def _():
        o_ref[...] = tail_ref[0].T.astype(o_ref.dtype)


def _regroup_w_in(w_stack, layer):
    wt = w_stack.transpose(0, 2, 1)
    n_blocks = N_HEAD_BLOCKS + COLS_GAB // W_BLOCK
    tail0 = GATE_COL0 + 2 * NH_B
    assert GATE_COL0 % LANES == 0 and tail0 % SUBLANES == 0

    def head_block(j):
        regrouped = (j % 3) * N_GROUPS + j // 3
        return layer, jnp.where(j < N_QKV_BLOCKS, regrouped, jnp.minimum(j, N_HEAD_BLOCKS - 1)), 0

    return pl.pallas_call(
        _regroup_kernel,
        grid=(n_blocks,),
        in_specs=[
            pl.BlockSpec((None, W_BLOCK, D_MODEL), head_block),
            pl.BlockSpec((pl.Element(1), pl.Element(W_BLOCK), pl.Element(D_MODEL)),
                         lambda j: (layer, pl.multiple_of(
                             tail0 + jnp.maximum(j - N_HEAD_BLOCKS, 0) * W_BLOCK, SUBLANES), 0)),
            pl.BlockSpec((None, LANES, D_MODEL), lambda j: (layer, GATE_COL0 // LANES, 0)),
        ],
        out_specs=[pl.BlockSpec((D_MODEL, W_BLOCK), lambda j: (0, j)),
                   pl.BlockSpec((D_MODEL, LANES), lambda j: (0, 0))],
        out_shape=[jax.ShapeDtypeStruct((D_MODEL, n_blocks * W_BLOCK), BF16),
                   jax.ShapeDtypeStruct((D_MODEL, LANES), BF16)],
        compiler_params=_params(1),
        name="regroup_w_in",
    )(wt, wt, wt)


def _inproj_kernel(*refs, prompt, tiles_per_batch):
    if prompt:
        (x_ref, g_ref, w_ref, wg_ref, cw_ref, cb_ref, qkv0_ref, qkv1_ref, qkv2_ref, qk_ref, vo_ref,
         gab_ref, gcol_ref, grow_ref, ctail_ref, kt_ref, hs_ref, ext_ref) = refs
    else:
        (x_ref, g_ref, w_ref, wg_ref, qkv0_ref, qkv1_ref, qkv2_ref, qk_ref, vo_ref, gab_ref,
         gcol_ref) = refs
    if prompt:
        @pl.when(pl.program_id(0) % tiles_per_batch == 0)
        def _():
            ext_ref[:, 0:SUBLANES, :] = jnp.zeros((ext_ref.shape[0], SUBLANES, LANES), F32)

    hf = _rms(x_ref[...], g_ref[...])
    h = hf.astype(BF16)
    tm = hf.shape[0]

    pw = PROJ_CHUNK

    def project(lhs, col0, width, store):
        for c in range(0, width, pw):
            store(c, _dot(lhs, w_ref[:, col0 + c:col0 + c + pw]))

    def store_rows(ref):
        def store(c, res):
            ref[:, c:c + pw] = res.astype(ref.dtype)
        return store

    if prompt:
        pad = SUBLANES

        def store_slabs(c, res):
            for k in range(pw // LANES):
                ext_ref[c // LANES + k, pad:pad + tm, :] = res[:, k * LANES:(k + 1) * LANES]
        project(h, COLS_QKV, COLS_QK, store_slabs)
    project(h, 0, COLS_QKV_G, store_rows(qkv0_ref))
    if prompt:
        n_slab = hs_ref.shape[0]
        for k in range(n_slab):
            hs_ref[k] = hf[:, k * LANES:(k + 1) * LANES]
        for g, ref in ((1, qkv1_ref), (2, qkv2_ref)):
            d = DILATIONS[g]
            rows = tm // d
            hp = jnp.concatenate(
                [jnp.concatenate([hs_ref[k, pl.ds(r, rows, stride=d), :] for r in range(d)], axis=0)
                 for k in range(n_slab)], axis=1).astype(BF16)

            def store(c, res, ref=ref, d=d, rows=rows):
                res = res.astype(ref.dtype)
                for r in range(d):
                    ref[r, :, c:c + pw] = res[r * rows:(r + 1) * rows]
            project(hp, g * COLS_QKV_G, COLS_QKV_G, store)
    else:
        project(h, COLS_QKV_G, COLS_QKV_G, store_rows(qkv1_ref))
        project(h, 2 * COLS_QKV_G, COLS_QKV_G, store_rows(qkv2_ref))
    if prompt:
        for k in range(COLS_QK // LANES):
            ks = slice(k * LANES, (k + 1) * LANES)
            y = cb_ref[:, ks] + ext_ref[k, pad:pad + tm, :] * cw_ref[CONV_B - 1:CONV_B, ks]
            for i in range(CONV_B - 1):
                lo = pad - (CONV_B - 1) + i
                y = y + ext_ref[k, lo:lo + tm, :] * cw_ref[i:i + 1, ks]
            y = y * _sigmoid(y)
            if k * LANES < B_WIDTH:
                qk_ref[:, ks] = y.astype(qk_ref.dtype)
            else:
                kt_ref[k * LANES - B_WIDTH:(k + 1) * LANES - B_WIDTH, :] = (
                    (y * (DK_B ** -0.5)).T.astype(kt_ref.dtype))
            tail = ext_ref[k, tm:tm + pad, :]
            ctail_ref[:, ks] = tail
            ext_ref[k, 0:pad, :] = tail
    else:
        project(h, COLS_QKV, COLS_QK, store_rows(qk_ref))
    col = COLS_QKV + COLS_QK
    if prompt:
        def store_vo(c, res):
            vo_ref[:, c:c + pw] = (res if c < B_WIDTH else _sigmoid(res)).astype(vo_ref.dtype)
        project(h, col, COLS_VO, store_vo)
    else:
        project(h, col, COLS_VO, store_rows(vo_ref))
    project(h, col + COLS_VO, COLS_GAB, store_rows(gab_ref))
    gates = _dot(h, wg_ref[...])
    gcol_ref[...] = gates
    if prompt:
        grow_ref[...] = gates.T[:SUBLANES, :]


def _inproj(x, gain, w_main, w_gate, *, tm, act_dtype, seq=None, conv=None):
    n = x.shape[0]
    dilate = seq is not None
    row = lambda i: (i, 0)
    out_shape = [jax.ShapeDtypeStruct((n, COLS_QKV_G), act_dtype)]
    out_specs = [pl.BlockSpec((tm, COLS_QKV_G), row)]
    for g in (1, 2):
        if dilate:
            d = DILATIONS[g]
            tpb = seq // tm
            out_shape.append(jax.ShapeDtypeStruct((n // seq, d, seq // d, COLS_QKV_G), act_dtype))
            out_specs.append(pl.BlockSpec((None, d, tm // d, COLS_QKV_G),
                                          lambda i, tpb=tpb: (i // tpb, 0, i % tpb, 0)))
        else:
            out_shape.append(jax.ShapeDtypeStruct((n, COLS_QKV_G), act_dtype))
            out_specs.append(pl.BlockSpec((tm, COLS_QKV_G), row))
    for cols, dt in ((B_WIDTH, act_dtype) if dilate else (COLS_QK, F32), (COLS_VO, act_dtype),
                     (COLS_GAB, act_dtype), (LANES, F32)):
        out_shape.append(jax.ShapeDtypeStruct((n, cols), dt))
        out_specs.append(pl.BlockSpec((tm, cols), row))
    in_specs = [
        pl.BlockSpec((tm, D_MODEL), row),
        _resident((1, D_MODEL)),
        _resident(w_main.shape),
        _resident(w_gate.shape),
    ]
    args = [x, gain, w_main, w_gate]
    scratch = []
    tpb = None
    if dilate:
        tpb = seq // tm
        in_specs += [_resident(conv[0].shape), _resident(conv[1].shape)]
        args += list(conv)
        out_shape.append(jax.ShapeDtypeStruct((SUBLANES, n), F32))
        out_specs.append(pl.BlockSpec((SUBLANES, tm), lambda i: (0, i)))
        out_shape.append(jax.ShapeDtypeStruct((n // seq, SUBLANES, COLS_QK), F32))
        out_specs.append(pl.BlockSpec((None, SUBLANES, COLS_QK), lambda i: (i // tpb, 0, 0)))
        out_shape.append(jax.ShapeDtypeStruct((B_WIDTH, n), act_dtype))
        out_specs.append(pl.BlockSpec((B_WIDTH, tm), lambda i: (0, i)))
        scratch =[pltpu.VMEM((D_MODEL // LANES, tm, LANES), F32),
                   pltpu.VMEM((COLS_QK // LANES, tm + SUBLANES, LANES), F32)]
    return pl.pallas_call(
        functools.partial(_inproj_kernel, prompt=dilate, tiles_per_batch=tpb),
        grid=(n // tm,),
        in_specs=in_specs,
        out_specs=out_specs,
        out_shape=out_shape,
        scratch_shapes=scratch,
        compiler_params=_params(1),
        name="inproj",
    )(*args)


def _attn_kernel(q_ref, kp_ref, kc_ref, vp_ref, vc_ref, bias_ref, o_ref, lse_ref):
    n_blk = q_ref.shape[0] // Q_BLOCK
    has_prev = pl.program_id(1) > 0
    key_lane = lax.broadcasted_iota(jnp.int32, (1, 1, 2 * Q_BLOCK), 2)
    first_mask = jnp.where((key_lane < Q_BLOCK) & jnp.logical_not(has_prev), NEG, 0.0)
    lane = lax.broadcasted_iota(jnp.int32, (Q_BLOCK, LANES), 1)
    low_half = lane < DH_A
    pair = 2 * DH_A

    def window(i, cur_ref, first_ref):
        rows = slice(i * Q_BLOCK, (i + 1) * Q_BLOCK)
        prev_ref, prows = ((first_ref, slice(0, Q_BLOCK)) if i == 0
                           else (cur_ref, slice((i - 1) * Q_BLOCK, i * Q_BLOCK)))
        return [jnp.concatenate([prev_ref[prows, hp * pair:(hp + 1) * pair],
                                 cur_ref[rows, hp * pair:(hp + 1) * pair]], axis=0)
                for hp in range(H_G // 2)]

    def qk_scores(i):
        rows = slice(i * Q_BLOCK, (i + 1) * Q_BLOCK)
        scores = []
        for hp, kk in enumerate(window(i, kc_ref, kp_ref)):
            qp = q_ref[rows, hp * pair:(hp + 1) * pair]
            scores.append(_dot_nt(jnp.where(low_half, qp, jnp.zeros_like(qp)), kk))
            scores.append(_dot_nt(jnp.where(low_half, jnp.zeros_like(qp), qp), kk))
        return jnp.stack(scores)

    for i in range(n_blk):
        rows = slice(i * Q_BLOCK, (i + 1) * Q_BLOCK)
        s = qk_scores(i) + bias_ref[...]
        if i == 0:
            s = s + first_mask
        m = jnp.max(s, axis=-1, keepdims=True)
        p = jnp.exp(s - m)
        l = jnp.sum(p, axis=-1, keepdims=True)
        pb = p.astype(BF16)
        inv = 1.0 / l
        lse = m + jnp.log(l)
        lse_all = jnp.zeros((Q_BLOCK, LANES), F32)
        for hp, vv in enumerate(window(i, vc_ref, vp_ref)):
            cols = slice(hp * pair, (hp + 1) * pair)
            o_lo = _dot(pb[2 * hp], vv) * inv[2 * hp]
            o_hi = _dot(pb[2 * hp + 1], vv) * inv[2 * hp + 1]
            o_ref[rows, cols] = jnp.where(low_half, o_lo, o_hi).astype(o_ref.dtype)
        for h in range(H_G):
            lse_all = jnp.where(lane == h, lse[h], lse_all)
        lse_ref[rows, :] = lse_all


def _attn_prompt(qkv, bias):
    nsub, u_len, _ = qkv.shape
    n_blk = math.gcd(ATTN_BLOCKS, u_len // Q_BLOCK)
    rows = n_blk * Q_BLOCK
    nb = u_len // rows

    def spec(col_block, prev):
        if prev:
            return pl.BlockSpec((None, Q_BLOCK, A_GROUP),
                                lambda s, j: (s, jnp.maximum(j * n_blk - 1, 0), col_block))
        return pl.BlockSpec((None, rows, A_GROUP), lambda s, j: (s, j, col_block))

    return pl.pallas_call(
        _attn_kernel,
        grid=(nsub, nb),
        in_specs=[spec(0, False), spec(1, True), spec(1, False), spec(2, True), spec(2, False),
                  _resident(bias.shape)],
        out_specs=[
            pl.BlockSpec((None, rows, A_GROUP), lambda s, j: (s, j, 0)),
            pl.BlockSpec((None, rows, LANES), lambda s, j: (s, j, 0)),
        ],
        out_shape=[
            jax.ShapeDtypeStruct((nsub, u_len, A_GROUP), BF16),
            jax.ShapeDtypeStruct((nsub, u_len, LANES), F32),
        ],
        compiler_params=_params(2),
        name="attn_prompt",
    )(qkv, qkv, qkv, qkv, qkv, bias)


def _kvtail_kernel(*refs):
    out_ref, nat_s = refs[-2:]
    srcs = refs[:-2]
    n_slab = nat_s.shape[0]
    for layer in range(len(srcs) // 2):
        @pl.when(pl.program_id(0) == layer)
        def _():
            for j, ref in enumerate(srcs[2 * layer:2 * layer + 2]):
                dil = ref.shape[0]
                for r in range(dil):
                    val = ref[r].astype(F32)
                    for s in range(n_slab):
                        nat_s[s, pl.ds(r, SPAN, stride=dil), :] = val[:, s * LANES:(s + 1) * LANES]
                for s in range(n_slab):
                    out_ref[j, s * LANES:(s + 1) * LANES, :] = nat_s[s].T


def _kv_tail(qkv_layers, g):
    depth = len(qkv_layers)
    batch, dil, u_len, _ = qkv_layers[0].shape
    keep = SPAN * dil
    last_blk = u_len // SPAN - 1
    specs, args = [], []
    for arr in qkv_layers:
        for col in (1, 2):
            specs.append(pl.BlockSpec((None, dil, SPAN, A_GROUP),
                                      lambda l, b, col=col: (b, 0, last_blk, col)))
            args.append(arr)
    return pl.pallas_call(
        _kvtail_kernel,
        grid=(depth, batch),
        in_specs=specs,
        out_specs=pl.BlockSpec((None, None, 2, A_GROUP, keep), lambda l, b: (l, b, 0, 0, 0)),
        out_shape=jax.ShapeDtypeStruct((depth, batch, 2, A_GROUP, keep), F32),
        scratch_shapes=[pltpu.VMEM((A_GROUP // LANES, keep, LANES), F32)],
        compiler_params=_params(2),
        name=f"kv_tail_g{g}",
    )(*args)


def _sattn_kernel(new_ref, kv0_ref, kv1_ref, kv2_ref, bt0_ref, bt1_ref, bt2_ref, b0_ref,
                  o_ref, lse_ref):
    eye = (lax.broadcasted_iota(jnp.int32, (DH_A, DH_A), 0)
           == lax.broadcasted_iota(jnp.int32, (DH_A, DH_A), 1))
    for g, (kv_ref, bt_ref) in enumerate(((kv0_ref, bt0_ref), (kv1_ref, bt1_ref),
                                          (kv2_ref, bt2_ref))):
        q = new_ref[g, 0]
        k_new = new_ref[g, 1]
        v_new = new_ref[g, 2]
        s_rows = []
        for h in range(H_G):
            q_col = jnp.sum(jnp.where(eye, q[h:h + 1, :], 0.0), axis=1, keepdims=True)
            s_rows.append(jnp.sum(kv_ref[0, h] * q_col, axis=0, keepdims=True))
        s = jnp.concatenate(s_rows, axis=0) + bt_ref[...]
        s0 = jnp.sum(k_new * q, axis=1, keepdims=True) + b0_ref[g]
        m = jnp.maximum(jnp.max(s, axis=1, keepdims=True), s0)
        p = jnp.exp(s - m[:, 0:1])
        p0 = jnp.exp(s0 - m)
        l = jnp.sum(p, axis=1, keepdims=True) + p0
        o_rows = []
        for h in range(H_G):
            o_col = jnp.sum(kv_ref[1, h] * p[h:h + 1, :], axis=1, keepdims=True)
            o_rows.append(jnp.sum(jnp.where(eye, o_col, 0.0), axis=0, keepdims=True))
        o_ref[g] = (jnp.concatenate(o_rows, axis=0) + p0 * v_new) / l
        lse_ref[g] = m + jnp.log(l)


def _attn_sample(new_qkv, caches, layer, bias_t, bias_0):
    nreq = new_qkv.shape[0]
    views = []
    specs = []
    for g, cache in enumerate(caches):
        n_buf = cache.shape[2]
        assert n_buf == SPAN * DILATIONS[g], "cache must hold exactly one window"
        views.append(cache.transpose(0, 1, 3, 4, 5, 2))
        specs.append(pl.BlockSpec((None, None, 2, H_G, DH_A, n_buf),
                                  lambda b, layer=layer: (layer, b, 0, 0, 0, 0)))
    out = jax.ShapeDtypeStruct((nreq, N_GROUPS, H_G, DH_A), F32)
    out_spec = pl.BlockSpec((None, N_GROUPS, H_G, DH_A), lambda b: (b, 0, 0, 0))
    return pl.pallas_call(
        _sattn_kernel,
        grid=(nreq,),
        in_specs=[pl.BlockSpec((None, N_GROUPS, 3, H_G, DH_A), lambda b: (b, 0, 0, 0, 0))] + specs
        + [_resident(t.shape) for t in bias_t] + [_resident(bias_0.shape)],
        out_specs=[out_spec, out_spec],
        out_shape=[out, out],
        compiler_params=_params(1),
        name="attn_sample",
    )(new_qkv, *views, *bias_t, bias_0)


def _mlstm_kernel(*refs, n_seq):
    q_ref = refs[0]
    kt_refs = refs[1:1 + n_seq]
    vo_ref, gcol_ref = refs[1 + n_seq:3 + n_seq]
    grow_refs = refs[3 + n_seq:3 + 2 * n_seq]
    gb_row_ref, gb_col_ref, bout_ref, c_ref, n_ref, m_ref, nrep_s = refs[3 + 2 * n_seq:]
    chunk = q_ref.shape[1]

    @pl.when(pl.program_id(1) == 0)
    def _():
        c_ref[...] = jnp.zeros_like(c_ref)
        nrep_s[...] = jnp.zeros_like(nrep_s)
        m_ref[...] = jnp.zeros_like(m_ref)

    ti = lax.broadcasted_iota(jnp.int32, (chunk, chunk), 0)
    si = lax.broadcasted_iota(jnp.int32, (chunk, chunk), 1)
    causal = ti >= si
    lower = causal.astype(BF16)
    upper = (si >= ti).astype(BF16)
    items = [(j, h) for j in range(n_seq) for h in range(NH_B)]
    hcol = lambda h: slice(h * DK_B, (h + 1) * DK_B)
    a_rows, b_ts, b_lasts = [], [], []
    for j in range(n_seq):
        z_col = gcol_ref[j] + gb_row_ref[...]
        z_row = grow_refs[j][...] + gb_col_ref[...]
        b_col = sum(_dot(lower, part) for part in _split3(_log_sigmoid(z_col)))
        b_row = sum(_dot(part, upper) for part in _split3(_log_sigmoid(z_row)))
        for h in range(NH_B):
            a_rows.append(z_row[h:h + 1, :] - b_row[NH_B + h:NH_B + h + 1, :])
            b_ts.append(b_col[:, NH_B + h:NH_B + h + 1])
            b_lasts.append(b_row[NH_B + h:NH_B + h + 1, chunk - 1:chunk])
    a_row = jnp.stack(a_rows)
    b_t = jnp.stack(b_ts)
    m_prev = jnp.stack([m_ref[j, :, h:h + 1] for j, h in items])
    a_mat = jnp.where(causal, a_row, NEG)
    gmax = jnp.maximum(m_prev, jnp.max(a_mat, axis=-1, keepdims=True))
    dw = jnp.exp(a_mat - gmax)
    iw = jnp.exp(m_prev - gmax)
    g_last = gmax[:, chunk - 1:chunk, :]
    decay = jnp.exp(m_prev - g_last)
    w_state = jnp.exp(a_row - g_last)

    qb = [q_ref[j, :, hcol(h)] for j, h in items]
    kt = [kt_refs[j][hcol(h), :] for j, h in items]
    vb = [vo_ref[j, :, hcol(h)] for j, h in items]
    c_old = [c_ref[j, h] for j, h in items]
    n_old = [nrep_s[j, h] for j, h in items]
    idx = range(len(items))
    qk = jnp.stack([_dot(qb[i], kt[i]) for i in idx]) * dw
    qkb = qk.astype(BF16)
    q_c = jnp.stack([_dot(qb[i], c_old[i].astype(BF16)) for i in idx])
    qk_v = jnp.stack([_dot(qkb[i], vb[i]) for i in idx])
    num = iw * q_c + qk_v
    q_n = jnp.stack([_dot(qb[i], n_old[i].astype(BF16))[:, 0:1] for i in idx])
    den = iw * q_n + jnp.sum(qk, axis=-1, keepdims=True)
    hid = num / jnp.maximum(jnp.abs(den), jnp.exp(-(b_t + gmax)))
    for i, (j, h) in enumerate(items):
        o_gate = vo_ref[j, :, B_WIDTH + h * DK_B:B_WIDTH + (h + 1) * DK_B].astype(F32)
        bout_ref[j, :, hcol(h)] = (o_gate * hid[i]).astype(bout_ref.dtype)

    kwt = (jnp.stack(kt).astype(F32) * w_state).astype(BF16)
    ones = jnp.ones((chunk, LANES), BF16)
    for i, (j, h) in enumerate(items):
        c_ref[j, h] = decay[i] * c_old[i] + _dot(kwt[i], vb[i])
        nrep_s[j, h] = decay[i] * n_old[i] + _dot(kwt[i], ones)
        m_ref[j, :, h:h + 1] = b_lasts[i] + g_last[i]

    @pl.when(pl.program_id(1) == pl.num_programs(1) - 1)
    def _():
        for j, h in items:
            n_ref[j, h:h + 1, :] = nrep_s[j, h].T[0:1, :]


def _mlstm_prompt(q, kt, vo, gcol, grow, gb_row, gb_col, batch, seq):
    chunk = MLSTM_CHUNK
    n_seq = math.gcd(MLSTM_SEQS, batch)
    nc = seq // chunk
    seq3 = lambda b, c: (b, c, 0)
    lanes = [lambda b, c, j=j: (0, (b * n_seq + j) * nc + c) for j in range(n_seq)]
    state = lambda b, c: (b, 0, 0)
    return pl.pallas_call(
        functools.partial(_mlstm_kernel, n_seq=n_seq),
        grid=(batch // n_seq, nc),
        in_specs=[pl.BlockSpec((n_seq, chunk, B_WIDTH), seq3)]
        + [pl.BlockSpec((B_WIDTH, chunk), lanes[j]) for j in range(n_seq)]
        + [pl.BlockSpec((n_seq, chunk, COLS_VO), seq3), pl.BlockSpec((n_seq, chunk, LANES), seq3)]
        + [pl.BlockSpec((SUBLANES, chunk), lanes[j]) for j in range(n_seq)]
        + [_resident(gb_row.shape), _resident(gb_col.shape)],
        out_specs=[
            pl.BlockSpec((n_seq, chunk, B_WIDTH), seq3),
            pl.BlockSpec((n_seq, NH_B, DK_B, DK_B), lambda b, c: (b, 0, 0, 0)),
            pl.BlockSpec((n_seq, NH_B, DK_B), state),
            pl.BlockSpec((n_seq, 1, LANES), state),
        ],
        out_shape=[
            jax.ShapeDtypeStruct((batch, seq, B_WIDTH), BF16),
            jax.ShapeDtypeStruct((batch, NH_B, DK_B, DK_B), F32),
            jax.ShapeDtypeStruct((batch, NH_B, DK_B), F32),
            jax.ShapeDtypeStruct((batch, 1, LANES), F32),
        ],
        scratch_shapes=[pltpu.VMEM((n_seq, NH_B, DK_B, LANES), F32)],
        compiler_params=_params(2),
        name="mlstm_prompt",
    )(q.reshape(batch, seq, B_WIDTH), *([kt] * n_seq), vo.reshape(batch, seq, COLS_VO),
      gcol.reshape(batch, seq, LANES), *([grow] * n_seq), gb_row, gb_col)


def _smlstm_kernel(qk_ref, vo_ref, g_ref, gb_ref, cs_ref, cw_ref, cb_ref, c0_ref, n0_ref, m0_ref,
                   bout_ref, c1_ref, n1_ref, m1_ref, ncs_ref):
    u = qk_ref[...]
    y = cb_ref[...] + u * cw_ref[CONV_B - 1:CONV_B, :]
    for i in range(CONV_B - 1):
        y = y + cs_ref[i:i + 1, :] * cw_ref[i:i + 1, :]
    ncs_ref[0:CONV_B - 2, :] = cs_ref[1:CONV_B - 1, :]
    ncs_ref[CONV_B - 2:CONV_B - 1, :] = u
    y = y * _sigmoid(y)
    qf = y[:, :B_WIDTH]
    kf = y[:, B_WIDTH:] * (DK_B ** -0.5)
    z = g_ref[...] + gb_ref[...]
    lf_all = _log_sigmoid(z)
    m0 = m0_ref[...]
    eye = (lax.broadcasted_iota(jnp.int32, (DK_B, DK_B), 0)
           == lax.broadcasted_iota(jnp.int32, (DK_B, DK_B), 1))
    for h in range(NH_B):
        hs = slice(h * DK_B, (h + 1) * DK_B)
        ig = z[:, h:h + 1]
        inter = lf_all[:, NH_B + h:NH_B + h + 1] + m0[:, h:h + 1]
        m_t = jnp.maximum(inter, ig)
        dw = jnp.exp(ig - m_t)
        iw = jnp.exp(inter - m_t)
        qh = qf[:, hs]
        kh = kf[:, hs]
        vh = vo_ref[:, hs]
        c_old = c0_ref[h]
        n_old = n0_ref[h:h + 1, :]
        qk = jnp.sum(qh * kh, axis=1, keepdims=True) * dw
        num = iw * _hdot(qh, c_old) + qk * vh
        den = iw * jnp.sum(qh * n_old, axis=1, keepdims=True) + qk
        hid = num / jnp.maximum(jnp.abs(den), jnp.exp(-m_t))
        o_gate = _sigmoid(vo_ref[:, B_WIDTH + h * DK_B:B_WIDTH + (h + 1) * DK_B])
        bout_ref[:, hs] = o_gate * hid
        k_col = jnp.sum(jnp.where(eye, kh, 0.0), axis=1, keepdims=True)
        c1_ref[h] = iw * c_old + (dw * k_col) * vh
        n1_ref[h:h + 1, :] = iw * n_old + dw * kh
        m1_ref[:, h:h + 1] = m_t


def _mlstm_sample(qk_pre, vo, gcol, gb_row, conv_state, cw, cb, c0, n0, m0, layer):
    nreq = qk_pre.shape[0]
    one = lambda b: (b, 0, 0)
    lay3 = lambda b, layer=layer: (layer, b, 0, 0)
    lay4 = lambda b, layer=layer: (layer, b, 0, 0, 0)
    return pl.pallas_call(
        _smlstm_kernel,
        grid=(nreq,),
        in_specs=[
            pl.BlockSpec((None, 1, COLS_QK), one),
            pl.BlockSpec((None, 1, COLS_VO), one),
            pl.BlockSpec((None, 1, LANES), one),
            _resident(gb_row.shape),
            pl.BlockSpec((None, None, CONV_B - 1, COLS_QK), lay3),
            _resident(cw.shape), _resident(cb.shape),
            pl.BlockSpec((None, None, NH_B, DK_B, DK_B), lay4),
            pl.BlockSpec((None, None, NH_B, DK_B), lay3),
            pl.BlockSpec((None, None, 1, LANES), lay3),
        ],
        out_specs=[
            pl.BlockSpec((None, 1, B_WIDTH), one),
            pl.BlockSpec((None, NH_B, DK_B, DK_B), lambda b: (b, 0, 0, 0)),
            pl.BlockSpec((None, NH_B, DK_B), one),
            pl.BlockSpec((None, 1, LANES), one),
            pl.BlockSpec((None, CONV_B - 1, COLS_QK), one),
        ],
        out_shape=[
            jax.ShapeDtypeStruct((nreq, 1, B_WIDTH), F32),
            jax.ShapeDtypeStruct((nreq, NH_B, DK_B, DK_B), F32),
            jax.ShapeDtypeStruct((nreq, NH_B, DK_B), F32),
            jax.ShapeDtypeStruct((nreq, 1, LANES), F32),
            jax.ShapeDtypeStruct((nreq, CONV_B - 1, COLS_QK), F32),
        ],
        compiler_params=_params(1),
        name="mlstm_sample",
    )(qk_pre.reshape(nreq, 1, COLS_QK), vo.reshape(nreq, 1, COLS_VO),
      gcol.reshape(nreq, 1, LANES), gb_row, conv_state, cw, cb, c0, n0, m0)


def _merge_kernel(o0_ref, o1_ref, o2_ref, l0_ref, l1_ref, l2_ref, bo_ref, gab_ref, x_ref,
                  wpa_ref, wpb_ref, wo_ref, g2_ref, e_ref, x2_ref, h2_ref, *scratch):
    tm = x_ref.shape[0]

    def natural(ref, buf):
        dil, rows, _ = ref.shape
        n_slab = buf.shape[0]
        for r in range(dil):
            val = ref[r].astype(F32)
            for k in range(n_slab):
                buf[k, pl.ds(r, rows, stride=dil), :] = val[:, k * LANES:(k + 1) * LANES]
        return jnp.concatenate([buf[k] for k in range(n_slab)], axis=1)

    if scratch:
        os_ = [o0_ref[...].astype(F32), natural(o1_ref, scratch[0]), natural(o2_ref, scratch[1])]
        lses = [l0_ref[...], natural(l1_ref, scratch[2]), natural(l2_ref, scratch[3])]
    else:
        os_ = [r[...].astype(F32) for r in (o0_ref, o1_ref, o2_ref)]
        lses = [r[...] for r in (l0_ref, l1_ref, l2_ref)]
    top = jnp.maximum(jnp.maximum(lses[0], lses[1]), lses[2])
    ws = [jnp.exp(l - top) for l in lses]
    inv = 1.0 / (ws[0] + ws[1] + ws[2])
    e2 = e_ref[...]
    a = None
    for w, o in zip(ws, os_):
        alpha = w * inv
        hi = alpha.astype(BF16)
        lo = (alpha - hi.astype(F32)).astype(BF16)
        term = _dot(jnp.concatenate([hi, lo], axis=1), e2) * o
        a = term if a is None else a + term
    pa = _dot(a.astype(BF16), wpa_ref[...])
    pb = _dot(bo_ref[...].astype(BF16), wpb_ref[...])
    merged = (_sigmoid(gab_ref[:, :D_MODEL].astype(F32)) * pa
              + _sigmoid(gab_ref[:, D_MODEL:].astype(F32)) * pb)
    x2 = x_ref[...] + _dot(merged.astype(BF16), wo_ref[...])
    x2_ref[...] = x2
    h2_ref[...] = _rms(x2, g2_ref[...]).astype(BF16)


def _merge(os_, lses, bout, gab, x, wpa, wpb, wo, g2, e_bf, *, layer, tm, seq=None):
    n = x.shape[0]
    row = lambda i: (i, 0)

    def group_specs(cols):
        specs = [pl.BlockSpec((tm, cols), row)]
        for g in (1, 2):
            if seq is None:
                specs.append(pl.BlockSpec((tm, cols), row))
            else:
                d = DILATIONS[g]
                tpb = seq // tm
                specs.append(pl.BlockSpec((None, d, tm // d, cols),
                                          lambda i, tpb=tpb: (i // tpb, 0, i % tpb, 0)))
        return specs

    scratch = []
    if seq is not None:
        scratch = ([pltpu.VMEM((A_GROUP // LANES, tm, LANES), F32)] * 2
                   + [pltpu.VMEM((1, tm, LANES), F32)] * 2)
    return pl.pallas_call(
        _merge_kernel,
        grid=(n // tm,),
        scratch_shapes=scratch,
        in_specs=group_specs(A_GROUP) + group_specs(LANES) + [
            pl.BlockSpec((tm, B_WIDTH), row),
            pl.BlockSpec((tm, COLS_GAB), row),
            pl.BlockSpec((tm, D_MODEL), row),
            _layer_resident(wpa.shape, layer), _layer_resident(wpb.shape, layer),
            _layer_resident(wo.shape, layer), _resident(g2.shape), _resident(e_bf.shape),
        ],
        out_specs=[pl.BlockSpec((tm, D_MODEL), row), pl.BlockSpec((tm, D_MODEL), row)],
        out_shape=[jax.ShapeDtypeStruct((n, D_MODEL), F32), jax.ShapeDtypeStruct((n, D_MODEL), BF16)],
        compiler_params=_params(1),
        name="merge",
    )(*os_, *lses, bout, gab, x, wpa, wpb, wo, g2, e_bf)


def _ffn_kernel(*refs, tiles_per_batch, from_state, final_norm):
    h2_ref, x2_ref, wup_ref, wdn_ref, cw_ref, cb_ref = refs[:6]
    pos = 6
    if from_state:
        prev_refs = refs[pos:pos + CONV_F - 1]
        pos += CONV_F - 1
    if final_norm:
        fg_ref = refs[pos]
        pos += 1
    x3_ref, u_ref = refs[pos:pos + 2]
    tm = h2_ref.shape[0]
    pad = SUBLANES
    if not from_state:
        ext_s = refs[pos + 2]

        @pl.when(pl.program_id(0) % tiles_per_batch == 0)
        def _():
            ext_s[:, 0:pad, :] = jnp.zeros((ext_s.shape[0], pad, LANES), F32)

    h2 = h2_ref[...]
    acc = x2_ref[...]
    n_chunk = D_FF // FF_CHUNK

    def halves(c):
        return (c * FF_CHUNK, D_FF + c * FF_CHUNK)

    def up(c):
        us = []
        for off in halves(c):
            u = _dot(h2, wup_ref[:, off:off + FF_CHUNK])
            if from_state:
                u_ref[:, off:off + FF_CHUNK] = u
                us.append(u)
            else:
                for k in range(FF_CHUNK // LANES):
                    ext_s[off // LANES + k, pad:pad + tm, :] = u[:, k * LANES:(k + 1) * LANES]
        return us

    def conv(c, us):
        ys = []
        for j, off in enumerate(halves(c)):
            cs = slice(off, off + FF_CHUNK)
            if from_state:
                y = cb_ref[:, cs] + us[j] * cw_ref[CONV_F - 1:CONV_F, cs]
                for i in range(CONV_F - 1):
                    y = y + prev_refs[i][:, cs] * cw_ref[i:i + 1, cs]
            else:
                parts = []
                for k in range(FF_CHUNK // LANES):
                    slab = off // LANES + k
                    ks = slice(off + k * LANES, off + (k + 1) * LANES)
                    part = cb_ref[:, ks]
                    for i in range(CONV_F):
                        lo = pad - (CONV_F - 1) + i
                        part = part + ext_s[slab, lo:lo + tm, :] * cw_ref[i:i + 1, ks]
                    parts.append(part)
                    tail = ext_s[slab, tm:tm + pad, :]
                    u_ref[:, ks] = tail
                    ext_s[slab, 0:pad, :] = tail
                y = jnp.concatenate(parts, axis=1)
            ys.append(y)
        return ys

    ahead = 1 if from_state else FF_AHEAD
    pending = [up(c) for c in range(min(ahead, n_chunk))]
    for c in range(n_chunk):
        us = pending.pop(0)
        if c + ahead < n_chunk:
            pending.append(up(c + ahead))
        y_act, y_gate = conv(c, us)
        act = _gelu_tanh(y_act) * y_gate
        acc = acc + _dot(act.astype(BF16), wdn_ref[c * FF_CHUNK:(c + 1) * FF_CHUNK, :])
    if final_norm:
        acc = _rms(acc, fg_ref[...])
    x3_ref[...] = acc


def _ffn(h2, x2, wup, wdn, cw, cb, *, layer, tm, tiles_per_batch=None, prev_rows=None,
         final_g=None):
    n = h2.shape[0]
    from_state = prev_rows is not None
    row = lambda i: (i, 0)
    in_specs = [
        pl.BlockSpec((tm, D_MODEL), row), pl.BlockSpec((tm, D_MODEL), row),
        _layer_resident(wup.shape, layer), _layer_resident(wdn.shape, layer),
        _resident(cw.shape), _resident(cb.shape),
    ]
    args = [h2, x2, wup, wdn, cw, cb]
    scratch = []
    if from_state:
        in_specs += [pl.BlockSpec((tm, 2 * D_FF), row)] * (CONV_F - 1)
        args += list(prev_rows)
        u_shape = jax.ShapeDtypeStruct((n, 2 * D_FF), F32)
        u_spec = pl.BlockSpec((tm, 2 * D_FF), row)
    else:
        n_batch = n // (tm * tiles_per_batch)
        u_shape = jax.ShapeDtypeStruct((n_batch, SUBLANES, 2 * D_FF), F32)
        u_spec = pl.BlockSpec((None, SUBLANES, 2 * D_FF), lambda i: (i // tiles_per_batch, 0, 0))
        scratch = [pltpu.VMEM((2 * D_FF // LANES, tm + SUBLANES, LANES), F32)]
    if final_g is not None:
        in_specs.append(_resident(final_g.shape))
        args.append(final_g)
    return pl.pallas_call(
        functools.partial(_ffn_kernel, tiles_per_batch=tiles_per_batch, from_state=from_state,
                          final_norm=final_g is not None),
        grid=(n // tm,),
        in_specs=in_specs,
        out_specs=[pl.BlockSpec((tm, D_MODEL), row), u_spec],
        out_shape=[jax.ShapeDtypeStruct((n, D_MODEL), F32), u_shape],
        scratch_shapes=scratch,
        compiler_params=_params(1),
        name="ffn",
    )(*args)


def _t5_bucket(dist):
    max_exact = NUM_BUCKETS // 2
    df = jnp.maximum(dist, 1).astype(F32)
    large = max_exact + (jnp.log(df / max_exact) / math.log(MAX_DISTANCE / max_exact)
                         * (NUM_BUCKETS - max_exact)).astype(jnp.int32)
    large = jnp.minimum(large, NUM_BUCKETS - 1)
    return jnp.where(dist < max_exact, dist, large)


def _bias_table(rel_bias, g, dist):
    bucket = _t5_bucket(jnp.asarray(dist, jnp.int32))
    table = rel_bias[:, g * H_G:(g + 1) * H_G].reshape((NUM_BUCKETS, H_G) + (1,) * bucket.ndim)
    ids = jnp.arange(NUM_BUCKETS).reshape((NUM_BUCKETS, 1) + (1,) * bucket.ndim)
    return jnp.sum(jnp.where(bucket[None, None] == ids, table, 0.0), axis=0)


def _prompt_bias(rel_bias, g):
    qi = np.arange(Q_BLOCK)[:, None]
    ki = np.arange(2 * Q_BLOCK)[None, :]
    rel = qi + Q_BLOCK - ki
    band = (rel >= 0) & (rel <= SPAN)
    bias = _bias_table(rel_bias, g, np.maximum(rel, 0) * DILATIONS[g])
    return jnp.where(band[None], bias, NEG)


def _sample_bias(rel_bias):
    tables, news = [], []
    for g in range(N_GROUPS):
        dil = DILATIONS[g]
        n_buf = SPAN * dil
        pos = np.arange(n_buf)
        bias = _bias_table(rel_bias, g, n_buf - pos)
        tables.append(jnp.where((pos % dil == 0)[None], bias, NEG))
        news.append(jnp.broadcast_to(_bias_table(rel_bias, g, np.zeros((1,), np.int32)), (H_G, DH_A)))
    return tables, jnp.stack(news)


def _head_indicator():
    e = np.zeros((LANES, A_GROUP), np.float32)
    for h in range(H_G):
        e[h, h * DH_A:(h + 1) * DH_A] = 1.0
    return e


def kernel(x_prompt, x_sample, cache_kv_w128, cache_kv_w512, cache_kv_w2048, state_mlstm_conv,
           state_mlstm_C, state_mlstm_n, state_mlstm_m, state_ffn_conv, rel_bias, norm1_g, w_in,
           mconv_w, mconv_b, mgate_b, w_pa, w_pb, w_o, norm2_g, w_up, fconv_w, fconv_b, w_down,
           final_norm_g):
    batch, seq, _ = x_prompt.shape
    nreq = x_sample.shape[0]
    depth = w_in.shape[0]
    n_p = batch * seq
    caches = (cache_kv_w128, cache_kv_w512, cache_kv_w2048)

    assert seq % (Q_BLOCK * DILATIONS[-1]) == 0 and seq >= WINDOWS[-1]
    e_bf = jnp.asarray(np.concatenate([_head_indicator()] * 2, axis=0), BF16)
    prompt_bias = [_prompt_bias(rel_bias, g) for g in range(N_GROUPS)]
    sbias_m, sbias_0 = _sample_bias(rel_bias)
    m0_all = jnp.pad(state_mlstm_m, ((0, 0), (0, 0), (0, LANES - NH_B)))[:, :, None, :]
    fg = final_norm_g.reshape(1, D_MODEL)

    wpa, wpb, wo = w_pa.astype(BF16), w_pb.astype(BF16), w_o.astype(BF16)
    wup, wdn = w_up.astype(BF16), w_down.astype(BF16)

    xp = x_prompt.reshape(n_p, D_MODEL)
    xs = x_sample.reshape(nreq, D_MODEL)
    p_st = [[] for _ in range(8)]
    s_st = [[] for _ in range(8)]

    for l in range(depth):
        last = l == depth - 1
        w_main, w_gate = _regroup_w_in(w_in, l)
        g1 = norm1_g[l].reshape(1, D_MODEL)
        g2 = norm2_g[l].reshape(1, D_MODEL)
        gate_bias = mgate_b[l].reshape(1, 2 * NH_B)
        gb_row = jnp.pad(gate_bias, ((0, 0), (0, LANES - 2 * NH_B)))
        gb_col = gate_bias.reshape(2 * NH_B, 1)
        mcw, mcb = mconv_w[l], mconv_b[l].reshape(1, COLS_QK)
        fcw, fcb = fconv_w[l], fconv_b[l].reshape(1, 2 * D_FF)

        *qkvs, q_b, vo, gab, gcol, grow, conv_tail, kt_b = _inproj(
            xp, g1, w_main, w_gate, tm=INPROJ_ROWS, act_dtype=BF16, seq=seq, conv=(mcw, mcb))
        os_, lses = [], []
        for g in range(N_GROUPS):
            d = DILATIONS[g]
            o_g, lse_g = _attn_prompt(qkvs[g].reshape(batch * d, seq // d, COLS_QKV_G), prompt_bias[g])
            shape = (n_p,) if g == 0 else (batch, d, seq // d)
            os_.append(o_g.reshape(shape + (A_GROUP,)))
            lses.append(lse_g.reshape(shape + (LANES,)))
        bout, c_p, n_p_state, m_p = _mlstm_prompt(q_b, kt_b, vo, gcol, grow, gb_row, gb_col,
                                                  batch, seq)
        x2, h2 = _merge(os_, lses, bout.reshape(n_p, B_WIDTH), gab, xp, wpa, wpb, wo, g2, e_bf,
                        layer=l, tm=MERGE_ROWS, seq=seq)
        xp, u_tail = _ffn(h2, x2, wup, wdn, fcw, fcb, layer=l, tm=FFN_ROWS,
                          tiles_per_batch=seq // FFN_ROWS, final_g=fg if last else None)

        for g in range(N_GROUPS):
            p_st[g].append(qkvs[g].reshape(batch, DILATIONS[g], seq // DILATIONS[g], COLS_QKV_G))
        p_st[3].append(conv_tail[:, SUBLANES - (CONV_B - 1):])
        p_st[4].append(c_p)
        p_st[5].append(n_p_state)
        p_st[6].append(m_p[:, 0, :NH_B])
        p_st[7].append(u_tail[:, SUBLANES - (CONV_F - 1):])

        *qkvs_s, qk_s, vo_s, gab_s, gcol_s = _inproj(xs, g1, w_main, w_gate, tm=nreq, act_dtype=F32)
        new_qkv = jnp.stack(qkvs_s, axis=1).reshape(nreq, N_GROUPS, 3, H_G, DH_A)
        o_s, lse_s = _attn_sample(new_qkv, caches, l, sbias_m, sbias_0)
        bout_s, c_s, n_s, m_s, conv_s = _mlstm_sample(qk_s, vo_s, gcol_s, gb_row, state_mlstm_conv,
                                                      mcw, mcb, state_mlstm_C, state_mlstm_n, m0_all, l)
        lse_pad = jnp.pad(lse_s[..., 0], ((0, 0), (0, 0), (0, LANES - H_G)))
        x2_s, h2_s = _merge([o_s[:, g].reshape(nreq, A_GROUP) for g in range(N_GROUPS)],
                            [lse_pad[:, g] for g in range(N_GROUPS)],
                            bout_s.reshape(nreq, B_WIDTH), gab_s, xs, wpa, wpb, wo, g2, e_bf,
                            layer=l, tm=nreq)
        fbuf = state_ffn_conv[l]
        xs, u_s = _ffn(h2_s, x2_s, wup, wdn, fcw, fcb, layer=l, tm=nreq,
                       prev_rows=[fbuf[:, i] for i in range(CONV_F - 1)],
                       final_g=fg if last else None)

        for g in range(N_GROUPS):
            s_st[g].append(new_qkv[:, g, 1:][:, None])
        s_st[3].append(conv_s)
        s_st[4].append(c_s)
        s_st[5].append(n_s)
        s_st[6].append(m_s[:, 0, :NH_B])
        s_st[7].append(jnp.concatenate([fbuf[:, 1:], u_s[:, None, :]], axis=1))

    outs = [xp.reshape(batch, seq, D_MODEL), xs.reshape(nreq, 1, D_MODEL)]
    for i in range(8):
        if i < N_GROUPS:
            kv = _kv_tail(p_st[i], i).reshape(depth, batch, 2, H_G, DH_A, SPAN * DILATIONS[i])
            outs.append(kv.transpose(0, 1, 5, 2, 3, 4))
        else:
            outs.append(jnp.stack(p_st[i], 0))
        outs.append(jnp.stack(s_st[i], 0))
    return tuple(outs)
```

```python
import functools
import math

import numpy as np
import jax
import jax.numpy as jnp
from jax import lax
from jax.experimental import pallas as pl
from jax.experimental.pallas import tpu as pltpu

F32 = jnp.float32
BF16 = jnp.bfloat16
HIGHEST = lax.Precision.HIGHEST

D_MODEL = 1024
WINDOWS = (128, 512, 2048)
DILATIONS = (1, 4, 16)
N_GROUPS = 3
H_G = 8
DH_A = 64
A_GROUP = H_G * DH_A
Q_BLOCK = 128
SPAN = 128
NH_B = 4
DK_B = 256
B_WIDTH = NH_B * DK_B
CONV_B = 4
D_FF = 2816
CONV_F = 3
NUM_BUCKETS = 32
MAX_DISTANCE = 2048
RMS_EPS = 1e-6
NEG = -1e30

LANES = 128
SUBLANES = 8
FF_CHUNK = 256
PROJ_CHUNK = 512
FF_AHEAD = 3
MLSTM_CHUNK = 256
MLSTM_SEQS = 2
ATTN_BLOCKS = 8
VMEM_LIMIT = 56 * 1024 * 1024
INPROJ_ROWS = 512
MERGE_ROWS = 512
FFN_ROWS = 512

COLS_QKV_G = 3 * A_GROUP
COLS_QKV = N_GROUPS * COLS_QKV_G
COLS_QK = 2 * B_WIDTH
COLS_VO = 2 * B_WIDTH
COLS_GAB = 2 * D_MODEL
GATE_COL0 = COLS_QKV + COLS_QK + COLS_VO


def _dot(a, b):
    return jnp.dot(a, b, preferred_element_type=F32)


def _hdot(a, b):
    return jnp.dot(a, b, precision=HIGHEST, preferred_element_type=F32)


def _dot_nt(a, b):
    return lax.dot_general(a, b, (((1,), (1,)), ((), ())), preferred_element_type=F32)


def _split3(x):
    hi = x.astype(BF16)
    rest = x - hi.astype(F32)
    mid = rest.astype(BF16)
    return hi, mid, (rest - mid.astype(F32)).astype(BF16)


def _sigmoid(x):
    return 1.0 / (1.0 + jnp.exp(-x))


def _log_sigmoid(x):
    return jnp.minimum(x, 0.0) - jnp.log1p(jnp.exp(-jnp.abs(x)))


def _gelu_tanh(x):
    return 0.5 * x * (1.0 + jnp.tanh(math.sqrt(2.0 / math.pi) * (x + 0.044715 * (x * x * x))))


def _rms(x, g):
    return x * lax.rsqrt(jnp.mean(x * x, axis=-1, keepdims=True) + RMS_EPS) * g


def _resident(shape):
    nd = len(shape)
    return pl.BlockSpec(shape, lambda *_: (0,) * nd, pipeline_mode=pl.Buffered(1))


def _layer_resident(stacked_shape, layer):
    nd = len(stacked_shape)
    return pl.BlockSpec((None,) + tuple(stacked_shape[1:]), lambda *_: (layer,) + (0,) * (nd - 1),
                        pipeline_mode=pl.Buffered(1))


def _params(n_grid):
    return pltpu.CompilerParams(dimension_semantics=("arbitrary",) * n_grid,
                                vmem_limit_bytes=VMEM_LIMIT)


W_BLOCK = 512
N_QKV_BLOCKS = COLS_QKV // W_BLOCK
N_HEAD_BLOCKS = GATE_COL0 // W_BLOCK


def _regroup_kernel(head_ref, tail_ref, gate_ref, o_ref, og_ref):
    j = pl.program_id(0)

    @pl.when(j == 0)
    def _():
        lane = lax.broadcasted_iota(jnp.int32, og_ref.shape, 1)
        og_ref[...] = jnp.where(lane < 2 * NH_B, gate_ref[...].T, 0.0).astype(og_ref.dtype)

    @pl.when(j < N_HEAD_BLOCKS)
    def _():
        is_q = (j < N_QKV_BLOCKS) & (j % 3 == 0)
        scale = jnp.where(is_q, DH_A ** -0.5, 1.0)
        o_ref[...] = (head_ref[...] * scale).T.astype(o_ref.dtype)

    @pl.when(j >= N_HEAD_BLOCKS)
    def _():
        o_ref[...] = tail_ref[0].T.astype(o_ref.dtype)


def _regroup_w_in(w_stack, layer):
    wt = w_stack.transpose(0, 2, 1)
    n_blocks = N_HEAD_BLOCKS + COLS_GAB // W_BLOCK
    tail0 = GATE_COL0 + 2 * NH_B
    assert GATE_COL0 % LANES == 0 and tail0 % SUBLANES == 0

    def head_block(j):
        regrouped = (j % 3) * N_GROUPS + j // 3
        return layer, jnp.where(j < N_QKV_BLOCKS, regrouped, jnp.minimum(j, N_HEAD_BLOCKS - 1)), 0

    return pl.pallas_call(
        _regroup_kernel,
        grid=(n_blocks,),
        in_specs=[
            pl.BlockSpec((None, W_BLOCK, D_MODEL), head_block),
            pl.BlockSpec((pl.Element(1), pl.Element(W_BLOCK), pl.Element(D_MODEL)),
                         lambda j: (layer, pl.multiple_of(
                             tail0 + jnp.maximum(j - N_HEAD_BLOCKS, 0) * W_BLOCK, SUBLANES), 0)),
            pl.BlockSpec((None, LANES, D_MODEL), lambda j: (layer, GATE_COL0 // LANES, 0)),
        ],
        out_specs=[pl.BlockSpec((D_MODEL, W_BLOCK), lambda j: (0, j)),
                   pl.BlockSpec((D_MODEL, LANES), lambda j: (0, 0))],
        out_shape=[jax.ShapeDtypeStruct((D_MODEL, n_blocks * W_BLOCK), BF16),
                   jax.ShapeDtypeStruct((D_MODEL, LANES), BF16)],
        compiler_params=_params(1),
        name="regroup_w_in",
    )(wt, wt, wt)


def _inproj_kernel(*refs, prompt, tiles_per_batch):
    if prompt:
        (x_ref, g_ref, w_ref, wg_ref, cw_ref, cb_ref, qkv0_ref, qkv1_ref, qkv2_ref, qk_ref, vo_ref,
         gab_ref, gcol_ref, grow_ref, ctail_ref, kt_ref, hs_ref, ext_ref) = refs
    else:
        (x_ref, g_ref, w_ref, wg_ref, qkv0_ref, qkv1_ref, qkv2_ref, qk_ref, vo_ref, gab_ref,
         gcol_ref) = refs
    if prompt:
        @pl.when(pl.program_id(0) % tiles_per_batch == 0)
        def _():
            ext_ref[:, 0:SUBLANES, :] = jnp.zeros((ext_ref.shape[0], SUBLANES, LANES), F32)

    hf = _rms(x_ref[...], g_ref[...])
    h = hf.astype(BF16)
    tm = hf.shape[0]

    pw = PROJ_CHUNK

    def project(lhs, col0, width, store):
        for c in range(0, width, pw):
            store(c, _dot(lhs, w_ref[:, col0 + c:col0 + c + pw]))

    def store_rows(ref):
        def store(c, res):
            ref[:, c:c + pw] = res.astype(ref.dtype)
        return store

    if prompt:
        pad = SUBLANES

        def store_slabs(c, res):
            for k in range(pw // LANES):
                ext_ref[c // LANES + k, pad:pad + tm, :] = res[:, k * LANES:(k + 1) * LANES]
        project(h, COLS_QKV, COLS_QK, store_slabs)
    project(h, 0, COLS_QKV_G, store_rows(qkv0_ref))
    if prompt:
        n_slab = hs_ref.shape[0]
        for k in range(n_slab):
            hs_ref[k] = hf[:, k * LANES:(k + 1) * LANES]
        for g, ref in ((1, qkv1_ref), (2, qkv2_ref)):
            d = DILATIONS[g]
            rows = tm // d
            hp = jnp.concatenate(
                [jnp.concatenate([hs_ref[k, pl.ds(r, rows, stride=d), :] for r in range(d)], axis=0)
                 for k in range(n_slab)], axis=1).astype(BF16)

            def store(c, res, ref=ref, d=d, rows=rows):
                res = res.astype(ref.dtype)
                for r in range(d):
                    ref[r, :, c:c + pw] = res[r * rows:(r + 1) * rows]
            project(hp, g * COLS_QKV_G, COLS_QKV_G, store)
    else:
        project(h, COLS_QKV_G, COLS_QKV_G, store_rows(qkv1_ref))
        project(h, 2 * COLS_QKV_G, COLS_QKV_G, store_rows(qkv2_ref))
    if prompt:
        for k in range(COLS_QK // LANES):
            ks = slice(k * LANES, (k + 1) * LANES)
            y = cb_ref[:, ks] + ext_ref[k, pad:pad + tm, :] * cw_ref[CONV_B - 1:CONV_B, ks]
            for i in range(CONV_B - 1):
                lo = pad - (CONV_B - 1) + i
                y = y + ext_ref[k, lo:lo + tm, :] * cw_ref[i:i + 1, ks]
            y = y * _sigmoid(y)
            if k * LANES < B_WIDTH:
                qk_ref[:, ks] = y.astype(qk_ref.dtype)
            else:
                kt_ref[k * LANES - B_WIDTH:(k + 1) * LANES - B_WIDTH, :] = (
                    (y * (DK_B ** -0.5)).T.astype(kt_ref.dtype))
            tail = ext_ref[k, tm:tm + pad, :]
            ctail_ref[:, ks] = tail
            ext_ref[k, 0:pad, :] = tail
    else:
        project(h, COLS_QKV, COLS_QK, store_rows(qk_ref))
    col = COLS_QKV + COLS_QK
    if prompt:
        def store_vo(c, res):
            vo_ref[:, c:c + pw] = (res if c < B_WIDTH else _sigmoid(res)).astype(vo_ref.dtype)
        project(h, col, COLS_VO, store_vo)
    else:
        project(h, col, COLS_VO, store_rows(vo_ref))
    project(h, col + COLS_VO, COLS_GAB, store_rows(gab_ref))
    gates = _dot(h, wg_ref[...])
    gcol_ref[...] = gates
    if prompt:
        grow_ref[...] = gates.T[:SUBLANES, :]


def _inproj(x, gain, w_main, w_gate, *, tm, act_dtype, seq=None, conv=None):
    n = x.shape[0]
    dilate = seq is not None
    row = lambda i: (i, 0)
    out_shape = [jax.ShapeDtypeStruct((n, COLS_QKV_G), act_dtype)]
    out_specs = [pl.BlockSpec((tm, COLS_QKV_G), row)]
    for g in (1, 2):
        if dilate:
            d = DILATIONS[g]
            tpb = seq // tm
            out_shape.append(jax.ShapeDtypeStruct((n // seq, d, seq // d, COLS_QKV_G), act_dtype))
            out_specs.append(pl.BlockSpec((None, d, tm // d, COLS_QKV_G),
                                          lambda i, tpb=tpb: (i // tpb, 0, i % tpb, 0)))
        else:
            out_shape.append(jax.ShapeDtypeStruct((n, COLS_QKV_G), act_dtype))
            out_specs.append(pl.BlockSpec((tm, COLS_QKV_G), row))
    for cols, dt in ((B_WIDTH, act_dtype) if dilate else (COLS_QK, F32), (COLS_VO, act_dtype),
                     (COLS_GAB, act_dtype), (LANES, F32)):
        out_shape.append(jax.ShapeDtypeStruct((n, cols), dt))
        out_specs.append(pl.BlockSpec((tm, cols), row))
    in_specs = [
        pl.BlockSpec((tm, D_MODEL), row),
        _resident((1, D_MODEL)),
        _resident(w_main.shape),
        _resident(w_gate.shape),
    ]
    args = [x, gain, w_main, w_gate]
    scratch = []
    tpb = None
    if dilate:
        tpb = seq // tm
        in_specs += [_resident(conv[0].shape), _resident(conv[1].shape)]
        args += list(conv)
        out_shape.append(jax.ShapeDtypeStruct((SUBLANES, n), F32))
        out_specs.append(pl.BlockSpec((SUBLANES, tm), lambda i: (0, i)))
        out_shape.append(jax.ShapeDtypeStruct((n // seq, SUBLANES, COLS_QK), F32))
        out_specs.append(pl.BlockSpec((None, SUBLANES, COLS_QK), lambda i: (i // tpb, 0, 0)))
        out_shape.append(jax.ShapeDtypeStruct((B_WIDTH, n), act_dtype))
        out_specs.append(pl.BlockSpec((B_WIDTH, tm), lambda i: (0, i)))
        scratch =[pltpu.VMEM((D_MODEL // LANES, tm, LANES), F32),
                   pltpu.VMEM((COLS_QK // LANES, tm + SUBLANES, LANES), F32)]
    return pl.pallas_call(
        functools.partial(_inproj_kernel, prompt=dilate, tiles_per_batch=tpb),
        grid=(n // tm,),
        in_specs=in_specs,
        out_specs=out_specs,
        out_shape=out_shape,
        scratch_shapes=scratch,
        compiler_params=_params(1),
        name="inproj",
    )(*args)


def _attn_kernel(q_ref, kp_ref, kc_ref, vp_ref, vc_ref, bias_ref, o_ref, lse_ref):
    n_blk = q_ref.shape[0] // Q_BLOCK
    has_prev = pl.program_id(1) > 0
    key_lane = lax.broadcasted_iota(jnp.int32, (1, 1, 2 * Q_BLOCK), 2)
    first_mask = jnp.where((key_lane < Q_BLOCK) & jnp.logical_not(has_prev), NEG, 0.0)
    lane = lax.broadcasted_iota(jnp.int32, (Q_BLOCK, LANES), 1)
    low_half = lane < DH_A
    pair = 2 * DH_A

    def window(i, cur_ref, first_ref):
        rows = slice(i * Q_BLOCK, (i + 1) * Q_BLOCK)
        prev_ref, prows = ((first_ref, slice(0, Q_BLOCK)) if i == 0
                           else (cur_ref, slice((i - 1) * Q_BLOCK, i * Q_BLOCK)))
        return [jnp.concatenate([prev_ref[prows, hp * pair:(hp + 1) * pair],
                                 cur_ref[rows, hp * pair:(hp + 1) * pair]], axis=0)
                for hp in range(H_G // 2)]

    def qk_scores(i):
        rows = slice(i * Q_BLOCK, (i + 1) * Q_BLOCK)
        scores = []
        for hp, kk in enumerate(window(i, kc_ref, kp_ref)):
            qp = q_ref[rows, hp * pair:(hp + 1) * pair]
            scores.append(_dot_nt(jnp.where(low_half, qp, jnp.zeros_like(qp)), kk))
            scores.append(_dot_nt(jnp.where(low_half, jnp.zeros_like(qp), qp), kk))
        return jnp.stack(scores)

    for i in range(n_blk):
        rows = slice(i * Q_BLOCK, (i + 1) * Q_BLOCK)
        s = qk_scores(i) + bias_ref[...]
        if i == 0:
            s = s + first_mask
        m = jnp.max(s, axis=-1, keepdims=True)
        p = jnp.exp(s - m)
        l = jnp.sum(p, axis=-1, keepdims=True)
        pb = p.astype(BF16)
        inv = 1.0 / l
        lse = m + jnp.log(l)
        lse_all = jnp.zeros((Q_BLOCK, LANES), F32)
        for hp, vv in enumerate(window(i, vc_ref, vp_ref)):
            cols = slice(hp * pair, (hp + 1) * pair)
            o_lo = _dot(pb[2 * hp], vv) * inv[2 * hp]
            o_hi = _dot(pb[2 * hp + 1], vv) * inv[2 * hp + 1]
            o_ref[rows, cols] = jnp.where(low_half, o_lo, o_hi).astype(o_ref.dtype)
        for h in range(H_G):
            lse_all = jnp.where(lane == h, lse[h], lse_all)
        lse_ref[rows, :] = lse_all


def _attn_prompt(qkv, bias):
    nsub, u_len, _ = qkv.shape
    n_blk = math.gcd(ATTN_BLOCKS, u_len // Q_BLOCK)
    rows = n_blk * Q_BLOCK
    nb = u_len // rows

    def spec(col_block, prev):
        if prev:
            return pl.BlockSpec((None, Q_BLOCK, A_GROUP),
                                lambda s, j: (s, jnp.maximum(j * n_blk - 1, 0), col_block))
        return pl.BlockSpec((None, rows, A_GROUP), lambda s, j: (s, j, col_block))

    return pl.pallas_call(
        _attn_kernel,
        grid=(nsub, nb),
        in_specs=[spec(0, False), spec(1, True), spec(1, False), spec(2, True), spec(2, False),
                  _resident(bias.shape)],
        out_specs=[
            pl.BlockSpec((None, rows, A_GROUP), lambda s, j: (s, j, 0)),
            pl.BlockSpec((None, rows, LANES), lambda s, j: (s, j, 0)),
        ],
        out_shape=[
            jax.ShapeDtypeStruct((nsub, u_len, A_GROUP), BF16),
            jax.ShapeDtypeStruct((nsub, u_len, LANES), F32),
        ],
        compiler_params=_params(2),
        name="attn_prompt",
    )(qkv, qkv, qkv, qkv, qkv, bias)


def _kvtail_kernel(*refs):
    out_ref, nat_s = refs[-2:]
    srcs = refs[:-2]
    n_slab = nat_s.shape[0]
    for layer in range(len(srcs) // 2):
        @pl.when(pl.program_id(0) == layer)
        def _():
            for j, ref in enumerate(srcs[2 * layer:2 * layer + 2]):
                dil = ref.shape[0]
                for r in range(dil):
                    val = ref[r].astype(F32)
                    for s in range(n_slab):
                        nat_s[s, pl.ds(r, SPAN, stride=dil), :] = val[:, s * LANES:(s + 1) * LANES]
                for s in range(n_slab):
                    out_ref[j, s * LANES:(s + 1) * LANES, :] = nat_s[s].T


def _kv_tail(qkv_layers, g):
    depth = len(qkv_layers)
    batch, dil, u_len, _ = qkv_layers[0].shape
    keep = SPAN * dil
    last_blk = u_len // SPAN - 1
    specs, args = [], []
    for layer, arr in enumerate(qkv_layers):
        idle_b = 0 if layer > 0 else batch - 1
        for col in (1, 2):
            specs.append(pl.BlockSpec(
                (None, dil, SPAN, A_GROUP),
                lambda l, b, col=col, layer=layer, idle_b=idle_b: (
                    jnp.where(l == layer, b, idle_b), 0, last_blk, col)))
            args.append(arr)
    return pl.pallas_call(
        _kvtail_kernel,
        grid=(depth, batch),
        in_specs=specs,
        out_specs=pl.BlockSpec((None, None, 2, A_GROUP, keep), lambda l, b: (l, b, 0, 0, 0)),
        out_shape=jax.ShapeDtypeStruct((depth, batch, 2, A_GROUP, keep), F32),
        scratch_shapes=[pltpu.VMEM((A_GROUP // LANES, keep, LANES), F32)],
        compiler_params=_params(2),
        name=f"kv_tail_g{g}",
    )(*args)


def _sattn_kernel(new_ref, kv0_ref, kv1_ref, kv2_ref, bt0_ref, bt1_ref, bt2_ref, b0_ref,
                  o_ref, lse_ref):
    eye = (lax.broadcasted_iota(jnp.int32, (DH_A, DH_A), 0)
           == lax.broadcasted_iota(jnp.int32, (DH_A, DH_A), 1))
    for g, (kv_ref, bt_ref) in enumerate(((kv0_ref, bt0_ref), (kv1_ref, bt1_ref),
                                          (kv2_ref, bt2_ref))):
        q = new_ref[g, 0]
        k_new = new_ref[g, 1]
        v_new = new_ref[g, 2]
        s_rows = []
        for h in range(H_G):
            q_col = jnp.sum(jnp.where(eye, q[h:h + 1, :], 0.0), axis=1, keepdims=True)
            s_rows.append(jnp.sum(kv_ref[0, h] * q_col, axis=0, keepdims=True))
        s = jnp.concatenate(s_rows, axis=0) + bt_ref[...]
        s0 = jnp.sum(k_new * q, axis=1, keepdims=True) + b0_ref[g]
        m = jnp.maximum(jnp.max(s, axis=1, keepdims=True), s0)
        p = jnp.exp(s - m[:, 0:1])
        p0 = jnp.exp(s0 - m)
        l = jnp.sum(p, axis=1, keepdims=True) + p0
        o_rows = []
        for h in range(H_G):
            o_col = jnp.sum(kv_ref[1, h] * p[h:h + 1, :], axis=1, keepdims=True)
            o_rows.append(jnp.sum(jnp.where(eye, o_col, 0.0), axis=0, keepdims=True))
        o_ref[g] = (jnp.concatenate(o_rows, axis=0) + p0 * v_new) / l
        lse_ref[g] = m + jnp.log(l)


def _attn_sample(new_qkv, caches, layer, bias_t, bias_0):
    nreq = new_qkv.shape[0]
    views = []
    specs = []
    for g, cache in enumerate(caches):
        n_buf = cache.shape[2]
        assert n_buf == SPAN * DILATIONS[g], "cache must hold exactly one window"
        views.append(cache.transpose(0, 1, 3, 4, 5, 2))
        specs.append(pl.BlockSpec((None, None, 2, H_G, DH_A, n_buf),
                                  lambda b, layer=layer: (layer, b, 0, 0, 0, 0)))
    out = jax.ShapeDtypeStruct((nreq, N_GROUPS, H_G, DH_A), F32)
    out_spec = pl.BlockSpec((None, N_GROUPS, H_G, DH_A), lambda b: (b, 0, 0, 0))
    return pl.pallas_call(
        _sattn_kernel,
        grid=(nreq,),
        in_specs=[pl.BlockSpec((None, N_GROUPS, 3, H_G, DH_A), lambda b: (b, 0, 0, 0, 0))] + specs
        + [_resident(t.shape) for t in bias_t] + [_resident(bias_0.shape)],
        out_specs=[out_spec, out_spec],
        out_shape=[out, out],
        compiler_params=_params(1),
        name="attn_sample",
    )(new_qkv, *views, *bias_t, bias_0)


def _mlstm_kernel(*refs, n_seq):
    q_ref = refs[0]
    kt_refs = refs[1:1 + n_seq]
    vo_ref, gcol_ref = refs[1 + n_seq:3 + n_seq]
    grow_refs = refs[3 + n_seq:3 + 2 * n_seq]
    gb_row_ref, gb_col_ref, bout_ref, c_ref, n_ref, m_ref, nrep_s = refs[3 + 2 * n_seq:]
    chunk = q_ref.shape[1]

    @pl.when(pl.program_id(1) == 0)
    def _():
        c_ref[...] = jnp.zeros_like(c_ref)
        nrep_s[...] = jnp.zeros_like(nrep_s)
        m_ref[...] = jnp.zeros_like(m_ref)

    ti = lax.broadcasted_iota(jnp.int32, (chunk, chunk), 0)
    si = lax.broadcasted_iota(jnp.int32, (chunk, chunk), 1)
    causal = ti >= si
    lower = causal.astype(BF16)
    upper = (si >= ti).astype(BF16)
    items = [(j, h) for j in range(n_seq) for h in range(NH_B)]
    hcol = lambda h: slice(h * DK_B, (h + 1) * DK_B)
    a_rows, b_ts, b_lasts = [], [], []
    for j in range(n_seq):
        z_col = gcol_ref[j] + gb_row_ref[...]
        z_row = grow_refs[j][...] + gb_col_ref[...]
        b_col = sum(_dot(lower, part) for part in _split3(_log_sigmoid(z_col)))
        b_row = sum(_dot(part, upper) for part in _split3(_log_sigmoid(z_row)))
        for h in range(NH_B):
            a_rows.append(z_row[h:h + 1, :] - b_row[NH_B + h:NH_B + h + 1, :])
            b_ts.append(b_col[:, NH_B + h:NH_B + h + 1])
            b_lasts.append(b_row[NH_B + h:NH_B + h + 1, chunk - 1:chunk])
    a_row = jnp.stack(a_rows)
    b_t = jnp.stack(b_ts)
    m_prev = jnp.stack([m_ref[j, :, h:h + 1] for j, h in items])
    a_mat = jnp.where(causal, a_row, NEG)
    gmax = jnp.maximum(m_prev, jnp.max(a_mat, axis=-1, keepdims=True))
    dw = jnp.exp(a_mat - gmax)
    iw = jnp.exp(m_prev - gmax)
    g_last = gmax[:, chunk - 1:chunk, :]
    decay = jnp.exp(m_prev - g_last)
    w_state = jnp.exp(a_row - g_last)

    qb = [q_ref[j, :, hcol(h)] for j, h in items]
    kt = [kt_refs[j][hcol(h), :] for j, h in items]
    vb = [vo_ref[j, :, hcol(h)] for j, h in items]
    c_old = [c_ref[j, h] for j, h in items]
    n_old = [nrep_s[j, h] for j, h in items]
    idx = range(len(items))
    qk = jnp.stack([_dot(qb[i], kt[i]) for i in idx]) * dw
    qkb = qk.astype(BF16)
    q_c = jnp.stack([_dot(qb[i], c_old[i].astype(BF16)) for i in idx])
    qk_v = jnp.stack([_dot(qkb[i], vb[i]) for i in idx])
    num = iw * q_c + qk_v
    q_n = jnp.stack([_dot(qb[i], n_old[i].astype(BF16))[:, 0:1] for i in idx])
    den = iw * q_n + jnp.sum(qk, axis=-1, keepdims=True)
    hid = num / jnp.maximum(jnp.abs(den), jnp.exp(-(b_t + gmax)))
    for i, (j, h) in enumerate(items):
        o_gate = vo_ref[j, :, B_WIDTH + h * DK_B:B_WIDTH + (h + 1) * DK_B].astype(F32)
        bout_ref[j, :, hcol(h)] = (o_gate * hid[i]).astype(bout_ref.dtype)

    kwt = (jnp.stack(kt).astype(F32) * w_state).astype(BF16)
    ones = jnp.ones((chunk, LANES), BF16)
    for i, (j, h) in enumerate(items):
        c_ref[j, h] = decay[i] * c_old[i] + _dot(kwt[i], vb[i])
        nrep_s[j, h] = decay[i] * n_old[i] + _dot(kwt[i], ones)
        m_ref[j, :, h:h + 1] = b_lasts[i] + g_last[i]

    @pl.when(pl.program_id(1) == pl.num_programs(1) - 1)
    def _():
        for j, h in items:
            n_ref[j, h:h + 1, :] = nrep_s[j, h].T[0:1, :]


def _mlstm_prompt(q, kt, vo, gcol, grow, gb_row, gb_col, batch, seq):
    chunk = MLSTM_CHUNK
    n_seq = math.gcd(MLSTM_SEQS, batch)
    nc = seq // chunk
    seq3 = lambda b, c: (b, c, 0)
    lanes = [lambda b, c, j=j: (0, (b * n_seq + j) * nc + c) for j in range(n_seq)]
    state = lambda b, c: (b, 0, 0)
    return pl.pallas_call(
        functools.partial(_mlstm_kernel, n_seq=n_seq),
        grid=(batch // n_seq, nc),
        in_specs=[pl.BlockSpec((n_seq, chunk, B_WIDTH), seq3)]
        + [pl.BlockSpec((B_WIDTH, chunk), lanes[j]) for j in range(n_seq)]
        + [pl.BlockSpec((n_seq, chunk, COLS_VO), seq3), pl.BlockSpec((n_seq, chunk, LANES), seq3)]
        + [pl.BlockSpec((SUBLANES, chunk), lanes[j]) for j in range(n_seq)]
        + [_resident(gb_row.shape), _resident(gb_col.shape)],
        out_specs=[
            pl.BlockSpec((n_seq, chunk, B_WIDTH), seq3),
            pl.BlockSpec((n_seq, NH_B, DK_B, DK_B), lambda b, c: (b, 0, 0, 0)),
            pl.BlockSpec((n_seq, NH_B, DK_B), state),
            pl.BlockSpec((n_seq, 1, LANES), state),
        ],
        out_shape=[
            jax.ShapeDtypeStruct((batch, seq, B_WIDTH), BF16),
            jax.ShapeDtypeStruct((batch, NH_B, DK_B, DK_B), F32),
            jax.ShapeDtypeStruct((batch, NH_B, DK_B), F32),
            jax.ShapeDtypeStruct((batch, 1, LANES), F32),
        ],
        scratch_shapes=[pltpu.VMEM((n_seq, NH_B, DK_B, LANES), F32)],
        compiler_params=_params(2),
        name="mlstm_prompt",
    )(q.reshape(batch, seq, B_WIDTH), *([kt] * n_seq), vo.reshape(batch, seq, COLS_VO),
      gcol.reshape(batch, seq, LANES), *([grow] * n_seq), gb_row, gb_col)


def _smlstm_kernel(qk_ref, vo_ref, g_ref, gb_ref, cs_ref, cw_ref, cb_ref, c0_ref, n0_ref, m0_ref,
                   bout_ref, c1_ref, n1_ref, m1_ref, ncs_ref):
    u = qk_ref[...]
    y = cb_ref[...] + u * cw_ref[CONV_B - 1:CONV_B, :]
    for i in range(CONV_B - 1):
        y = y + cs_ref[i:i + 1, :] * cw_ref[i:i + 1, :]
    ncs_ref[0:CONV_B - 2, :] = cs_ref[1:CONV_B - 1, :]
    ncs_ref[CONV_B - 2:CONV_B - 1, :] = u
    y = y * _sigmoid(y)
    qf = y[:, :B_WIDTH]
    kf = y[:, B_WIDTH:] * (DK_B ** -0.5)
    z = g_ref[...] + gb_ref[...]
    lf_all = _log_sigmoid(z)
    m0 = m0_ref[...]
    eye = (lax.broadcasted_iota(jnp.int32, (DK_B, DK_B), 0)
           == lax.broadcasted_iota(jnp.int32, (DK_B, DK_B), 1))
    for h in range(NH_B):
        hs = slice(h * DK_B, (h + 1) * DK_B)
        ig = z[:, h:h + 1]
        inter = lf_all[:, NH_B + h:NH_B + h + 1] + m0[:, h:h + 1]
        m_t = jnp.maximum(inter, ig)
        dw = jnp.exp(ig - m_t)
        iw = jnp.exp(inter - m_t)
        qh = qf[:, hs]
        kh = kf[:, hs]
        vh = vo_ref[:, hs]
        c_old = c0_ref[h]
        n_old = n0_ref[h:h + 1, :]
        qk = jnp.sum(qh * kh, axis=1, keepdims=True) * dw
        num = iw * _hdot(qh, c_old) + qk * vh
        den = iw * jnp.sum(qh * n_old, axis=1, keepdims=True) + qk
        hid = num / jnp.maximum(jnp.abs(den), jnp.exp(-m_t))
        o_gate = _sigmoid(vo_ref[:, B_WIDTH + h * DK_B:B_WIDTH + (h + 1) * DK_B])
        bout_ref[:, hs] = o_gate * hid
        k_col = jnp.sum(jnp.where(eye, kh, 0.0), axis=1, keepdims=True)
        c1_ref[h] = iw * c_old + (dw * k_col) * vh
        n1_ref[h:h + 1, :] = iw * n_old + dw * kh
        m1_ref[:, h:h + 1] = m_t


def _mlstm_sample(qk_pre, vo, gcol, gb_row, conv_state, cw, cb, c0, n0, m0, layer):
    nreq = qk_pre.shape[0]
    one = lambda b: (b, 0, 0)
    lay3 = lambda b, layer=layer: (layer, b, 0, 0)
    lay4 = lambda b, layer=layer: (layer, b, 0, 0, 0)
    return pl.pallas_call(
        _smlstm_kernel,
        grid=(nreq,),
        in_specs=[
            pl.BlockSpec((None, 1, COLS_QK), one),
            pl.BlockSpec((None, 1, COLS_VO), one),
            pl.BlockSpec((None, 1, LANES), one),
            _resident(gb_row.shape),
            pl.BlockSpec((None, None, CONV_B - 1, COLS_QK), lay3),
            _resident(cw.shape), _resident(cb.shape),
            pl.BlockSpec((None, None, NH_B, DK_B, DK_B), lay4),
            pl.BlockSpec((None, None, NH_B, DK_B), lay3),
            pl.BlockSpec((None, None, 1, LANES), lay3),
        ],
        out_specs=[
            pl.BlockSpec((None, 1, B_WIDTH), one),
            pl.BlockSpec((None, NH_B, DK_B, DK_B), lambda b: (b, 0, 0, 0)),
            pl.BlockSpec((None, NH_B, DK_B), one),
            pl.BlockSpec((None, 1, LANES), one),
            pl.BlockSpec((None, CONV_B - 1, COLS_QK), one),
        ],
        out_shape=[
            jax.ShapeDtypeStruct((nreq, 1, B_WIDTH), F32),
            jax.ShapeDtypeStruct((nreq, NH_B, DK_B, DK_B), F32),
            jax.ShapeDtypeStruct((nreq, NH_B, DK_B), F32),
            jax.ShapeDtypeStruct((nreq, 1, LANES), F32),
            jax.ShapeDtypeStruct((nreq, CONV_B - 1, COLS_QK), F32),
        ],
        compiler_params=_params(1),
        name="mlstm_sample",
    )(qk_pre.reshape(nreq, 1, COLS_QK), vo.reshape(nreq, 1, COLS_VO),
      gcol.reshape(nreq, 1, LANES), gb_row, conv_state, cw, cb, c0, n0, m0)


def _merge_kernel(o0_ref, o1_ref, o2_ref, l0_ref, l1_ref, l2_ref, bo_ref, gab_ref, x_ref,
                  wpa_ref, wpb_ref, wo_ref, g2_ref, e_ref, x2_ref, h2_ref, *scratch):
    tm = x_ref.shape[0]

    def natural(ref, buf):
        dil, rows, _ = ref.shape
        n_slab = buf.shape[0]
        for r in range(dil):
            val = ref[r].astype(F32)
            for k in range(n_slab):
                buf[k, pl.ds(r, rows, stride=dil), :] = val[:, k * LANES:(k + 1) * LANES]
        return jnp.concatenate([buf[k] for k in range(n_slab)], axis=1)

    if scratch:
        os_ = [o0_ref[...].astype(F32), natural(o1_ref, scratch[0]), natural(o2_ref, scratch[1])]
        lses = [l0_ref[...], natural(l1_ref, scratch[2]), natural(l2_ref, scratch[3])]
    else:
        os_ = [r[...].astype(F32) for r in (o0_ref, o1_ref, o2_ref)]
        lses = [r[...] for r in (l0_ref, l1_ref, l2_ref)]
    top = jnp.maximum(jnp.maximum(lses[0], lses[1]), lses[2])
    ws = [jnp.exp(l - top) for l in lses]
    inv = 1.0 / (ws[0] + ws[1] + ws[2])
    e2 = e_ref[...]
    a = None
    for w, o in zip(ws, os_):
        alpha = w * inv
        hi = alpha.astype(BF16)
        lo = (alpha - hi.astype(F32)).astype(BF16)
        term = _dot(jnp.concatenate([hi, lo], axis=1), e2) * o
        a = term if a is None else a + term
    pa = _dot(a.astype(BF16), wpa_ref[...])
    pb = _dot(bo_ref[...].astype(BF16), wpb_ref[...])
    merged = (_sigmoid(gab_ref[:, :D_MODEL].astype(F32)) * pa
              + _sigmoid(gab_ref[:, D_MODEL:].astype(F32)) * pb)
    x2 = x_ref[...] + _dot(merged.astype(BF16), wo_ref[...])
    x2_ref[...] = x2
    h2_ref[...] = _rms(x2, g2_ref[...]).astype(BF16)


def _merge(os_, lses, bout, gab, x, wpa, wpb, wo, g2, e_bf, *, layer, tm, seq=None):
    n = x.shape[0]
    row = lambda i: (i, 0)

    def group_specs(cols):
        specs = [pl.BlockSpec((tm, cols), row)]
        for g in (1, 2):
            if seq is None:
                specs.append(pl.BlockSpec((tm, cols), row))
            else:
                d = DILATIONS[g]
                tpb = seq // tm
                specs.append(pl.BlockSpec((None, d, tm // d, cols),
                                          lambda i, tpb=tpb: (i // tpb, 0, i % tpb, 0)))
        return specs

    scratch = []
    if seq is not None:
        scratch = ([pltpu.VMEM((A_GROUP // LANES, tm, LANES), F32)] * 2
                   + [pltpu.VMEM((1, tm, LANES), F32)] * 2)
    return pl.pallas_call(
        _merge_kernel,
        grid=(n // tm,),
        scratch_shapes=scratch,
        in_specs=group_specs(A_GROUP) + group_specs(LANES) + [
            pl.BlockSpec((tm, B_WIDTH), row),
            pl.BlockSpec((tm, COLS_GAB), row),
            pl.BlockSpec((tm, D_MODEL), row),
            _layer_resident(wpa.shape, layer), _layer_resident(wpb.shape, layer),
            _layer_resident(wo.shape, layer), _resident(g2.shape), _resident(e_bf.shape),
        ],
        out_specs=[pl.BlockSpec((tm, D_MODEL), row), pl.BlockSpec((tm, D_MODEL), row)],
        out_shape=[jax.ShapeDtypeStruct((n, D_MODEL), F32), jax.ShapeDtypeStruct((n, D_MODEL), BF16)],
        compiler_params=_params(1),
        name="merge",
    )(*os_, *lses, bout, gab, x, wpa, wpb, wo, g2, e_bf)


def _ffn_kernel(*refs, tiles_per_batch, from_state, final_norm):
    h2_ref, x2_ref, wup_ref, wdn_ref, cw_ref, cb_ref = refs[:6]
    pos = 6
    if from_state:
        prev_refs = refs[pos:pos + CONV_F - 1]
        pos += CONV_F - 1
    if final_norm:
        fg_ref = refs[pos]
        pos += 1
    x3_ref, u_ref = refs[pos:pos + 2]
    tm = h2_ref.shape[0]
    pad = SUBLANES
    if not from_state:
        ext_s = refs[pos + 2]

        @pl.when(pl.program_id(0) % tiles_per_batch == 0)
        def _():
            ext_s[:, 0:pad, :] = jnp.zeros((ext_s.shape[0], pad, LANES), F32)

    h2 = h2_ref[...]
    acc = x2_ref[...]
    n_chunk = D_FF // FF_CHUNK

    def halves(c):
        return (c * FF_CHUNK, D_FF + c * FF_CHUNK)

    def up(c):
        us = []
        for off in halves(c):
            u = _dot(h2, wup_ref[:, off:off + FF_CHUNK])
            if from_state:
                u_ref[:, off:off + FF_CHUNK] = u
                us.append(u)
            else:
                for k in range(FF_CHUNK // LANES):
                    ext_s[off // LANES + k, pad:pad + tm, :] = u[:, k * LANES:(k + 1) * LANES]
        return us

    def conv(c, us):
        ys = []
        for j, off in enumerate(halves(c)):
            cs = slice(off, off + FF_CHUNK)
            if from_state:
                y = cb_ref[:, cs] + us[j] * cw_ref[CONV_F - 1:CONV_F, cs]
                for i in range(CONV_F - 1):
                    y = y + prev_refs[i][:, cs] * cw_ref[i:i + 1, cs]
            else:
                parts = []
                for k in range(FF_CHUNK // LANES):
                    slab = off // LANES + k
                    ks = slice(off + k * LANES, off + (k + 1) * LANES)
                    part = cb_ref[:, ks]
                    for i in range(CONV_F):
                        lo = pad - (CONV_F - 1) + i
                        part = part + ext_s[slab, lo:lo + tm, :] * cw_ref[i:i + 1, ks]
                    parts.append(part)
                    tail = ext_s[slab, tm:tm + pad, :]
                    u_ref[:, ks] = tail
                    ext_s[slab, 0:pad, :] = tail
                y = jnp.concatenate(parts, axis=1)
            ys.append(y)
        return ys

    ahead = 1 if from_state else FF_AHEAD
    pending = [up(c) for c in range(min(ahead, n_chunk))]
    for c in range(n_chunk):
        us = pending.pop(0)
        if c + ahead < n_chunk:
            pending.append(up(c + ahead))
        y_act, y_gate = conv(c, us)
        act = _gelu_tanh(y_act) * y_gate
        acc = acc + _dot(act.astype(BF16), wdn_ref[c * FF_CHUNK:(c + 1) * FF_CHUNK, :])
    if final_norm:
        acc = _rms(acc, fg_ref[...])
    x3_ref[...] = acc


def _ffn(h2, x2, wup, wdn, cw, cb, *, layer, tm, tiles_per_batch=None, prev_rows=None,
         final_g=None):
    n = h2.shape[0]
    from_state = prev_rows is not None
    row = lambda i: (i, 0)
    in_specs = [
        pl.BlockSpec((tm, D_MODEL), row), pl.BlockSpec((tm, D_MODEL), row),
        _layer_resident(wup.shape, layer), _layer_resident(wdn.shape, layer),
        _resident(cw.shape), _resident(cb.shape),
    ]
    args = [h2, x2, wup, wdn, cw, cb]
    scratch = []
    if from_state:
        in_specs += [pl.BlockSpec((tm, 2 * D_FF), row)] * (CONV_F - 1)
        args += list(prev_rows)
        u_shape = jax.ShapeDtypeStruct((n, 2 * D_FF), F32)
        u_spec = pl.BlockSpec((tm, 2 * D_FF), row)
    else:
        n_batch = n // (tm * tiles_per_batch)
        u_shape = jax.ShapeDtypeStruct((n_batch, SUBLANES, 2 * D_FF), F32)
        u_spec = pl.BlockSpec((None, SUBLANES, 2 * D_FF), lambda i: (i // tiles_per_batch, 0, 0))
        scratch = [pltpu.VMEM((2 * D_FF // LANES, tm + SUBLANES, LANES), F32)]
    if final_g is not None:
        in_specs.append(_resident(final_g.shape))
        args.append(final_g)
    return pl.pallas_call(
        functools.partial(_ffn_kernel, tiles_per_batch=tiles_per_batch, from_state=from_state,
                          final_norm=final_g is not None),
        grid=(n // tm,),
        in_specs=in_specs,
        out_specs=[pl.BlockSpec((tm, D_MODEL), row), u_spec],
        out_shape=[jax.ShapeDtypeStruct((n, D_MODEL), F32), u_shape],
        scratch_shapes=scratch,
        compiler_params=_params(1),
        name="ffn",
    )(*args)


def _t5_bucket(dist):
    max_exact = NUM_BUCKETS // 2
    df = jnp.maximum(dist, 1).astype(F32)
    large = max_exact + (jnp.log(df / max_exact) / math.log(MAX_DISTANCE / max_exact)
                         * (NUM_BUCKETS - max_exact)).astype(jnp.int32)
    large = jnp.minimum(large, NUM_BUCKETS - 1)
    return jnp.where(dist < max_exact, dist, large)


def _bias_table(rel_bias, g, dist):
    bucket = _t5_bucket(jnp.asarray(dist, jnp.int32))
    table = rel_bias[:, g * H_G:(g + 1) * H_G].reshape((NUM_BUCKETS, H_G) + (1,) * bucket.ndim)
    ids = jnp.arange(NUM_BUCKETS).reshape((NUM_BUCKETS, 1) + (1,) * bucket.ndim)
    return jnp.sum(jnp.where(bucket[None, None] == ids, table, 0.0), axis=0)


def _prompt_bias(rel_bias, g):
    qi = np.arange(Q_BLOCK)[:, None]
    ki = np.arange(2 * Q_BLOCK)[None, :]
    rel = qi + Q_BLOCK - ki
    band = (rel >= 0) & (rel <= SPAN)
    bias = _bias_table(rel_bias, g, np.maximum(rel, 0) * DILATIONS[g])
    return jnp.where(band[None], bias, NEG)


def _sample_bias(rel_bias):
    tables, news = [], []
    for g in range(N_GROUPS):
        dil = DILATIONS[g]
        n_buf = SPAN * dil
        pos = np.arange(n_buf)
        bias = _bias_table(rel_bias, g, n_buf - pos)
        tables.append(jnp.where((pos % dil == 0)[None], bias, NEG))
        news.append(jnp.broadcast_to(_bias_table(rel_bias, g, np.zeros((1,), np.int32)), (H_G, DH_A)))
    return tables, jnp.stack(news)


def _head_indicator():
    e = np.zeros((LANES, A_GROUP), np.float32)
    for h in range(H_G):
        e[h, h * DH_A:(h + 1) * DH_A] = 1.0
    return e


def kernel(x_prompt, x_sample, cache_kv_w128, cache_kv_w512, cache_kv_w2048, state_mlstm_conv,
           state_mlstm_C, state_mlstm_n, state_mlstm_m, state_ffn_conv, rel_bias, norm1_g, w_in,
           mconv_w, mconv_b, mgate_b, w_pa, w_pb, w_o, norm2_g, w_up, fconv_w, fconv_b, w_down,
           final_norm_g):
    batch, seq, _ = x_prompt.shape
    nreq = x_sample.shape[0]
    depth = w_in.shape[0]
    n_p = batch * seq
    caches = (cache_kv_w128, cache_kv_w512, cache_kv_w2048)

    assert seq % (Q_BLOCK * DILATIONS[-1]) == 0 and seq >= WINDOWS[-1]
    e_bf = jnp.asarray(np.concatenate([_head_indicator()] * 2, axis=0), BF16)
    prompt_bias = [_prompt_bias(rel_bias, g) for g in range(N_GROUPS)]
    sbias_m, sbias_0 = _sample_bias(rel_bias)
    m0_all = jnp.pad(state_mlstm_m, ((0, 0), (0, 0), (0, LANES - NH_B)))[:, :, None, :]
    fg = final_norm_g.reshape(1, D_MODEL)

    wpa, wpb, wo = w_pa.astype(BF16), w_pb.astype(BF16), w_o.astype(BF16)
    wup, wdn = w_up.astype(BF16), w_down.astype(BF16)

    xp = x_prompt.reshape(n_p, D_MODEL)
    xs = x_sample.reshape(nreq, D_MODEL)
    p_st = [[] for _ in range(8)]
    s_st = [[] for _ in range(8)]

    for l in range(depth):
        last = l == depth - 1
        w_main, w_gate = _regroup_w_in(w_in, l)
        g1 = norm1_g[l].reshape(1, D_MODEL)
        g2 = norm2_g[l].reshape(1, D_MODEL)
        gate_bias = mgate_b[l].reshape(1, 2 * NH_B)
        gb_row = jnp.pad(gate_bias, ((0, 0), (0, LANES - 2 * NH_B)))
        gb_col = gate_bias.reshape(2 * NH_B, 1)
        mcw, mcb = mconv_w[l], mconv_b[l].reshape(1, COLS_QK)
        fcw, fcb = fconv_w[l], fconv_b[l].reshape(1, 2 * D_FF)

        *qkvs, q_b, vo, gab, gcol, grow, conv_tail, kt_b = _inproj(
            xp, g1, w_main, w_gate, tm=INPROJ_ROWS, act_dtype=BF16, seq=seq, conv=(mcw, mcb))
        os_, lses = [], []
        for g in range(N_GROUPS):
            d = DILATIONS[g]
            o_g, lse_g = _attn_prompt(qkvs[g].reshape(batch * d, seq // d, COLS_QKV_G), prompt_bias[g])
            shape = (n_p,) if g == 0 else (batch, d, seq // d)
            os_.append(o_g.reshape(shape + (A_GROUP,)))
            lses.append(lse_g.reshape(shape + (LANES,)))
        bout, c_p, n_p_state, m_p = _mlstm_prompt(q_b, kt_b, vo, gcol, grow, gb_row, gb_col,
                                                  batch, seq)
        x2, h2 = _merge(os_, lses, bout.reshape(n_p, B_WIDTH), gab, xp, wpa, wpb, wo, g2, e_bf,
                        layer=l, tm=MERGE_ROWS, seq=seq)
        xp, u_tail = _ffn(h2, x2, wup, wdn, fcw, fcb, layer=l, tm=FFN_ROWS,
                          tiles_per_batch=seq // FFN_ROWS, final_g=fg if last else None)

        for g in range(N_GROUPS):
            p_st[g].append(qkvs[g].reshape(batch, DILATIONS[g], seq // DILATIONS[g], COLS_QKV_G))
        p_st[3].append(conv_tail[:, SUBLANES - (CONV_B - 1):])
        p_st[4].append(c_p)
        p_st[5].append(n_p_state)
        p_st[6].append(m_p[:, 0, :NH_B])
        p_st[7].append(u_tail[:, SUBLANES - (CONV_F - 1):])

        *qkvs_s, qk_s, vo_s, gab_s, gcol_s = _inproj(xs, g1, w_main, w_gate, tm=nreq, act_dtype=F32)
        new_qkv = jnp.stack(qkvs_s, axis=1).reshape(nreq, N_GROUPS, 3, H_G, DH_A)
        o_s, lse_s = _attn_sample(new_qkv, caches, l, sbias_m, sbias_0)
        bout_s, c_s, n_s, m_s, conv_s = _mlstm_sample(qk_s, vo_s, gcol_s, gb_row, state_mlstm_conv,
                                                      mcw, mcb, state_mlstm_C, state_mlstm_n, m0_all, l)
        lse_pad = jnp.pad(lse_s[..., 0], ((0, 0), (0, 0), (0, LANES - H_G)))
        x2_s, h2_s = _merge([o_s[:, g].reshape(nreq, A_GROUP) for g in range(N_GROUPS)],
                            [lse_pad[:, g] for g in range(N_GROUPS)],
                            bout_s.reshape(nreq, B_WIDTH), gab_s, xs, wpa, wpb, wo, g2, e_bf,
                            layer=l, tm=nreq)
        fbuf = state_ffn_conv[l]
        xs, u_s = _ffn(h2_s, x2_s, wup, wdn, fcw, fcb, layer=l, tm=nreq,
                       prev_rows=[fbuf[:, i] for i in range(CONV_F - 1)],
                       final_g=fg if last else None)

        for g in range(N_GROUPS):
            s_st[g].append(new_qkv[:, g, 1:][:, None])
        s_st[3].append(conv_s)
        s_st[4].append(c_s)
        s_st[5].append(n_s)
        s_st[6].append(m_s[:, 0, :NH_B])
        s_st[7].append(jnp.concatenate([fbuf[:, 1:], u_s[:, None, :]], axis=1))

    outs = [xp.reshape(batch, seq, D_MODEL), xs.reshape(nreq, 1, D_MODEL)]
    for i in range(8):
        if i < N_GROUPS:
            kv = _kv_tail(p_st[i], i).reshape(depth, batch, 2, H_G, DH_A, SPAN * DILATIONS[i])
            outs.append(kv.transpose(0, 1, 5, 2, 3, 4))
        else:
            outs.append(jnp.stack(p_st[i], 0))
        outs.append(jnp.stack(s_st[i], 0))
    return tuple(outs)
```

```python
import functools
import math

import numpy as np
import jax
import jax.numpy as jnp
from jax import lax
from jax.experimental import pallas as pl
from jax.experimental.pallas import tpu as pltpu

F32 = jnp.float32
BF16 = jnp.bfloat16
HIGHEST = lax.Precision.HIGHEST

D_MODEL = 1024
WINDOWS = (128, 512, 2048)
DILATIONS = (1, 4, 16)
N_GROUPS = 3
H_G = 8
DH_A = 64
A_GROUP = H_G * DH_A
Q_BLOCK = 128
SPAN = 128
NH_B = 4
DK_B = 256
B_WIDTH = NH_B * DK_B
CONV_B = 4
D_FF = 2816
CONV_F = 3
NUM_BUCKETS = 32
MAX_DISTANCE = 2048
RMS_EPS = 1e-6
NEG = -1e30

LANES = 128
SUBLANES = 8
FF_CHUNK = 256
PROJ_CHUNK = 512
FF_AHEAD = 3
MLSTM_CHUNK = 256
MLSTM_SEQS = 2
ATTN_BLOCKS = 16
VMEM_LIMIT = 56 * 1024 * 1024
INPROJ_ROWS = 512
MERGE_ROWS = 512
FFN_ROWS = 512

COLS_QKV_G = 3 * A_GROUP
COLS_QKV = N_GROUPS * COLS_QKV_G
COLS_QK = 2 * B_WIDTH
COLS_VO = 2 * B_WIDTH
COLS_GAB = 2 * D_MODEL
GATE_COL0 = COLS_QKV + COLS_QK + COLS_VO


def _dot(a, b):
    return jnp.dot(a, b, preferred_element_type=F32)


def _hdot(a, b):
    return jnp.dot(a, b, precision=HIGHEST, preferred_element_type=F32)


def _dot_nt(a, b):
    return lax.dot_general(a, b, (((1,), (1,)), ((), ())), preferred_element_type=F32)


def _split3(x):
    hi = x.astype(BF16)
    rest = x - hi.astype(F32)
    mid = rest.astype(BF16)
    return hi, mid, (rest - mid.astype(F32)).astype(BF16)


def _sigmoid(x):
    return 1.0 / (1.0 + jnp.exp(-x))


def _log_sigmoid(x):
    return jnp.minimum(x, 0.0) - jnp.log1p(jnp.exp(-jnp.abs(x)))


def _gelu_tanh(x):
    return 0.5 * x * (1.0 + jnp.tanh(math.sqrt(2.0 / math.pi) * (x + 0.044715 * (x * x * x))))


def _rms(x, g):
    return x * lax.rsqrt(jnp.mean(x * x, axis=-1, keepdims=True) + RMS_EPS) * g


def _resident(shape):
    nd = len(shape)
    return pl.BlockSpec(shape, lambda *_: (0,) * nd, pipeline_mode=pl.Buffered(1))


def _layer_resident(stacked_shape, layer):
    nd = len(stacked_shape)
    return pl.BlockSpec((None,) + tuple(stacked_shape[1:]), lambda *_: (layer,) + (0,) * (nd - 1),
                        pipeline_mode=pl.Buffered(1))


def _params(n_grid):
    return pltpu.CompilerParams(dimension_semantics=("arbitrary",) * n_grid,
                                vmem_limit_bytes=VMEM_LIMIT)


W_BLOCK = 512
N_QKV_BLOCKS = COLS_QKV // W_BLOCK
N_HEAD_BLOCKS = GATE_COL0 // W_BLOCK


def _regroup_kernel(head_ref, tail_ref, gate_ref, o_ref, og_ref):
    j = pl.program_id(0)

    @pl.when(j == 0)
    def _():
        lane = lax.broadcasted_iota(jnp.int32, og_ref.shape, 1)
        og_ref[...] = jnp.where(lane < 2 * NH_B, gate_ref[...].T, 0.0).astype(og_ref.dtype)

    @pl.when(j < N_HEAD_BLOCKS)
    def _():
        is_q = (j < N_QKV_BLOCKS) & (j % 3 == 0)
        scale = jnp.where(is_q, DH_A ** -0.5, 1.0)
        o_ref[...] = (head_ref[...] * scale).T.astype(o_ref.dtype)

    @pl.when(j >= N_HEAD_BLOCKS)
    def _():
        o_ref[...] = tail_ref[0].T.astype(o_ref.dtype)


def _regroup_w_in(w_stack, layer):
    wt = w_stack.transpose(0, 2, 1)
    n_blocks = N_HEAD_BLOCKS + COLS_GAB // W_BLOCK
    tail0 = GATE_COL0 + 2 * NH_B
    assert GATE_COL0 % LANES == 0 and tail0 % SUBLANES == 0

    def head_block(j):
        regrouped = (j % 3) * N_GROUPS + j // 3
        return layer, jnp.where(j < N_QKV_BLOCKS, regrouped, jnp.minimum(j, N_HEAD_BLOCKS - 1)), 0

    return pl.pallas_call(
        _regroup_kernel,
        grid=(n_blocks,),
        in_specs=[
            pl.BlockSpec((None, W_BLOCK, D_MODEL), head_block),
            pl.BlockSpec((pl.Element(1), pl.Element(W_BLOCK), pl.Element(D_MODEL)),
                         lambda j: (layer, pl.multiple_of(
                             tail0 + jnp.maximum(j - N_HEAD_BLOCKS, 0) * W_BLOCK, SUBLANES), 0)),
            pl.BlockSpec((None, LANES, D_MODEL), lambda j: (layer, GATE_COL0 // LANES, 0)),
        ],
        out_specs=[pl.BlockSpec((D_MODEL, W_BLOCK), lambda j: (0, j)),
                   pl.BlockSpec((D_MODEL, LANES), lambda j: (0, 0))],
        out_shape=[jax.ShapeDtypeStruct((D_MODEL, n_blocks * W_BLOCK), BF16),
                   jax.ShapeDtypeStruct((D_MODEL, LANES), BF16)],
        compiler_params=_params(1),
        name="regroup_w_in",
    )(wt, wt, wt)


def _inproj_kernel(*refs, prompt, tiles_per_batch):
    if prompt:
        (x_ref, g_ref, w_ref, wg_ref, cw_ref, cb_ref, qkv0_ref, qkv1_ref, qkv2_ref, qk_ref, vo_ref,
         gab_ref, gcol_ref, grow_ref, ctail_ref, kt_ref, hs_ref, ext_ref) = refs
    else:
        (x_ref, g_ref, w_ref, wg_ref, qkv0_ref, qkv1_ref, qkv2_ref, qk_ref, vo_ref, gab_ref,
         gcol_ref) = refs
    if prompt:
        @pl.when(pl.program_id(0) % tiles_per_batch == 0)
        def _():
            ext_ref[:, 0:SUBLANES, :] = jnp.zeros((ext_ref.shape[0], SUBLANES, LANES), F32)

    hf = _rms(x_ref[...], g_ref[...])
    h = hf.astype(BF16)
    tm = hf.shape[0]

    pw = PROJ_CHUNK

    def project(lhs, col0, width, store):
        for c in range(0, width, pw):
            store(c, _dot(lhs, w_ref[:, col0 + c:col0 + c + pw]))

    def store_rows(ref):
        def store(c, res):
            ref[:, c:c + pw] = res.astype(ref.dtype)
        return store

    if prompt:
        pad = SUBLANES

        def store_slabs(c, res):
            for k in range(pw // LANES):
                ext_ref[c // LANES + k, pad:pad + tm, :] = res[:, k * LANES:(k + 1) * LANES]
        project(h, COLS_QKV, COLS_QK, store_slabs)
    project(h, 0, COLS_QKV_G, store_rows(qkv0_ref))
    if prompt:
        n_slab = hs_ref.shape[0]
        for k in range(n_slab):
            hs_ref[k] = hf[:, k * LANES:(k + 1) * LANES]
        for g, ref in ((1, qkv1_ref), (2, qkv2_ref)):
            d = DILATIONS[g]
            rows = tm // d
            hp = jnp.concatenate(
                [jnp.concatenate([hs_ref[k, pl.ds(r, rows, stride=d), :] for r in range(d)], axis=0)
                 for k in range(n_slab)], axis=1).astype(BF16)

            def store(c, res, ref=ref, d=d, rows=rows):
                res = res.astype(ref.dtype)
                for r in range(d):
                    ref[r, :, c:c + pw] = res[r * rows:(r + 1) * rows]
            project(hp, g * COLS_QKV_G, COLS_QKV_G, store)
    else:
        project(h, COLS_QKV_G, COLS_QKV_G, store_rows(qkv1_ref))
        project(h, 2 * COLS_QKV_G, COLS_QKV_G, store_rows(qkv2_ref))
    if prompt:
        for k in range(COLS_QK // LANES):
            ks = slice(k * LANES, (k + 1) * LANES)
            y = cb_ref[:, ks] + ext_ref[k, pad:pad + tm, :] * cw_ref[CONV_B - 1:CONV_B, ks]
            for i in range(CONV_B - 1):
                lo = pad - (CONV_B - 1) + i
                y = y + ext_ref[k, lo:lo + tm, :] * cw_ref[i:i + 1, ks]
            y = y * _sigmoid(y)
            if k * LANES < B_WIDTH:
                qk_ref[:, ks] = y.astype(qk_ref.dtype)
            else:
                kt_ref[k * LANES - B_WIDTH:(k + 1) * LANES - B_WIDTH, :] = (
                    (y * (DK_B ** -0.5)).T.astype(kt_ref.dtype))
            tail = ext_ref[k, tm:tm + pad, :]
            ctail_ref[:, ks] = tail
            ext_ref[k, 0:pad, :] = tail
    else:
        project(h, COLS_QKV, COLS_QK, store_rows(qk_ref))
    col = COLS_QKV + COLS_QK
    if prompt:
        def store_vo(c, res):
            vo_ref[:, c:c + pw] = (res if c < B_WIDTH else _sigmoid(res)).astype(vo_ref.dtype)
        project(h, col, COLS_VO, store_vo)
    else:
        project(h, col, COLS_VO, store_rows(vo_ref))
    project(h, col + COLS_VO, COLS_GAB, store_rows(gab_ref))
    gates = _dot(h, wg_ref[...])
    gcol_ref[...] = gates
    if prompt:
        grow_ref[...] = gates.T[:SUBLANES, :]


def _inproj(x, gain, w_main, w_gate, *, tm, act_dtype, seq=None, conv=None):
    n = x.shape[0]
    dilate = seq is not None
    row = lambda i: (i, 0)
    out_shape = [jax.ShapeDtypeStruct((n, COLS_QKV_G), act_dtype)]
    out_specs = [pl.BlockSpec((tm, COLS_QKV_G), row)]
    for g in (1, 2):
        if dilate:
            d = DILATIONS[g]
            tpb = seq // tm
            out_shape.append(jax.ShapeDtypeStruct((n // seq, d, seq // d, COLS_QKV_G), act_dtype))
            out_specs.append(pl.BlockSpec((None, d, tm // d, COLS_QKV_G),
                                          lambda i, tpb=tpb: (i // tpb, 0, i % tpb, 0)))
        else:
            out_shape.append(jax.ShapeDtypeStruct((n, COLS_QKV_G), act_dtype))
            out_specs.append(pl.BlockSpec((tm, COLS_QKV_G), row))
    for cols, dt in ((B_WIDTH, act_dtype) if dilate else (COLS_QK, F32), (COLS_VO, act_dtype),
                     (COLS_GAB, act_dtype), (LANES, F32)):
        out_shape.append(jax.ShapeDtypeStruct((n, cols), dt))
        out_specs.append(pl.BlockSpec((tm, cols), row))
    in_specs = [
        pl.BlockSpec((tm, D_MODEL), row),
        _resident((1, D_MODEL)),
        _resident(w_main.shape),
        _resident(w_gate.shape),
    ]
    args = [x, gain, w_main, w_gate]
    scratch = []
    tpb = None
    if dilate:
        tpb = seq // tm
        in_specs += [_resident(conv[0].shape), _resident(conv[1].shape)]
        args += list(conv)
        out_shape.append(jax.ShapeDtypeStruct((SUBLANES, n), F32))
        out_specs.append(pl.BlockSpec((SUBLANES, tm), lambda i: (0, i)))
        out_shape.append(jax.ShapeDtypeStruct((n // seq, SUBLANES, COLS_QK), F32))
        out_specs.append(pl.BlockSpec((None, SUBLANES, COLS_QK), lambda i: (i // tpb, 0, 0)))
        out_shape.append(jax.ShapeDtypeStruct((B_WIDTH, n), act_dtype))
        out_specs.append(pl.BlockSpec((B_WIDTH, tm), lambda i: (0, i)))
        scratch =[pltpu.VMEM((D_MODEL // LANES, tm, LANES), F32),
                   pltpu.VMEM((COLS_QK // LANES, tm + SUBLANES, LANES), F32)]
    return pl.pallas_call(
        functools.partial(_inproj_kernel, prompt=dilate, tiles_per_batch=tpb),
        grid=(n // tm,),
        in_specs=in_specs,
        out_specs=out_specs,
        out_shape=out_shape,
        scratch_shapes=scratch,
        compiler_params=_params(1),
        name="inproj",
    )(*args)


def _attn_kernel(q_ref, kp_ref, kc_ref, vp_ref, vc_ref, bias_ref, o_ref, lse_ref):
    n_blk = q_ref.shape[0] // Q_BLOCK
    has_prev = pl.program_id(1) > 0
    key_lane = lax.broadcasted_iota(jnp.int32, (1, 1, 2 * Q_BLOCK), 2)
    first_mask = jnp.where((key_lane < Q_BLOCK) & jnp.logical_not(has_prev), NEG, 0.0)
    lane = lax.broadcasted_iota(jnp.int32, (Q_BLOCK, LANES), 1)
    low_half = lane < DH_A
    pair = 2 * DH_A

    def window(i, cur_ref, first_ref):
        rows = slice(i * Q_BLOCK, (i + 1) * Q_BLOCK)
        prev_ref, prows = ((first_ref, slice(0, Q_BLOCK)) if i == 0
                           else (cur_ref, slice((i - 1) * Q_BLOCK, i * Q_BLOCK)))
        return [jnp.concatenate([prev_ref[prows, hp * pair:(hp + 1) * pair],
                                 cur_ref[rows, hp * pair:(hp + 1) * pair]], axis=0)
                for hp in range(H_G // 2)]

    def qk_scores(i):
        rows = slice(i * Q_BLOCK, (i + 1) * Q_BLOCK)
        scores = []
        for hp, kk in enumerate(window(i, kc_ref, kp_ref)):
            qp = q_ref[rows, hp * pair:(hp + 1) * pair]
            scores.append(_dot_nt(jnp.where(low_half, qp, jnp.zeros_like(qp)), kk))
            scores.append(_dot_nt(jnp.where(low_half, jnp.zeros_like(qp), qp), kk))
        return jnp.stack(scores)

    for i in range(n_blk):
        rows = slice(i * Q_BLOCK, (i + 1) * Q_BLOCK)
        s = qk_scores(i) + bias_ref[...]
        if i == 0:
            s = s + first_mask
        m = jnp.max(s, axis=-1, keepdims=True)
        p = jnp.exp(s - m)
        l = jnp.sum(p, axis=-1, keepdims=True)
        pb = p.astype(BF16)
        inv = 1.0 / l
        lse = m + jnp.log(l)
        lse_all = jnp.zeros((Q_BLOCK, LANES), F32)
        for hp, vv in enumerate(window(i, vc_ref, vp_ref)):
            cols = slice(hp * pair, (hp + 1) * pair)
            o_lo = _dot(pb[2 * hp], vv) * inv[2 * hp]
            o_hi = _dot(pb[2 * hp + 1], vv) * inv[2 * hp + 1]
            o_ref[rows, cols] = jnp.where(low_half, o_lo, o_hi).astype(o_ref.dtype)
        for h in range(H_G):
            lse_all = jnp.where(lane == h, lse[h], lse_all)
        lse_ref[rows, :] = lse_all


def _attn_prompt(qkv, bias):
    nsub, u_len, _ = qkv.shape
    n_blk = math.gcd(ATTN_BLOCKS, u_len // Q_BLOCK)
    rows = n_blk * Q_BLOCK
    nb = u_len // rows

    def spec(col_block, prev):
        if prev:
            return pl.BlockSpec((None, Q_BLOCK, A_GROUP),
                                lambda s, j: (s, jnp.maximum(j * n_blk - 1, 0), col_block))
        return pl.BlockSpec((None, rows, A_GROUP), lambda s, j: (s, j, col_block))

    return pl.pallas_call(
        _attn_kernel,
        grid=(nsub, nb),
        in_specs=[spec(0, False), spec(1, True), spec(1, False), spec(2, True), spec(2, False),
                  _resident(bias.shape)],
        out_specs=[
            pl.BlockSpec((None, rows, A_GROUP), lambda s, j: (s, j, 0)),
            pl.BlockSpec((None, rows, LANES), lambda s, j: (s, j, 0)),
        ],
        out_shape=[
            jax.ShapeDtypeStruct((nsub, u_len, A_GROUP), BF16),
            jax.ShapeDtypeStruct((nsub, u_len, LANES), F32),
        ],
        compiler_params=_params(2),
        name="attn_prompt",
    )(qkv, qkv, qkv, qkv, qkv, bias)


def _kvtail_kernel(*refs):
    out_ref, nat_s = refs[-2:]
    srcs = refs[:-2]
    n_slab = nat_s.shape[0]
    for layer in range(len(srcs) // 2):
        @pl.when(pl.program_id(0) == layer)
        def _():
            for j, ref in enumerate(srcs[2 * layer:2 * layer + 2]):
                dil = ref.shape[0]
                for r in range(dil):
                    val = ref[r].astype(F32)
                    for s in range(n_slab):
                        nat_s[s, pl.ds(r, SPAN, stride=dil), :] = val[:, s * LANES:(s + 1) * LANES]
                for s in range(n_slab):
                    out_ref[j, s * LANES:(s + 1) * LANES, :] = nat_s[s].T


def _kv_tail(qkv_layers, g):
    depth = len(qkv_layers)
    batch, dil, u_len, _ = qkv_layers[0].shape
    keep = SPAN * dil
    last_blk = u_len // SPAN - 1
    specs, args = [], []
    for layer, arr in enumerate(qkv_layers):
        idle_b = 0 if layer > 0 else batch - 1
        for col in (1, 2):
            specs.append(pl.BlockSpec(
                (None, dil, SPAN, A_GROUP),
                lambda l, b, col=col, layer=layer, idle_b=idle_b: (
                    jnp.where(l == layer, b, idle_b), 0, last_blk, col)))
            args.append(arr)
    return pl.pallas_call(
        _kvtail_kernel,
        grid=(depth, batch),
        in_specs=specs,
        out_specs=pl.BlockSpec((None, None, 2, A_GROUP, keep), lambda l, b: (l, b, 0, 0, 0)),
        out_shape=jax.ShapeDtypeStruct((depth, batch, 2, A_GROUP, keep), F32),
        scratch_shapes=[pltpu.VMEM((A_GROUP // LANES, keep, LANES), F32)],
        compiler_params=_params(2),
        name=f"kv_tail_g{g}",
    )(*args)


def _sattn_kernel(new_ref, kv0_ref, kv1_ref, kv2_ref, bt0_ref, bt1_ref, bt2_ref, b0_ref,
                  o_ref, lse_ref):
    eye = (lax.broadcasted_iota(jnp.int32, (DH_A, DH_A), 0)
           == lax.broadcasted_iota(jnp.int32, (DH_A, DH_A), 1))
    for g, (kv_ref, bt_ref) in enumerate(((kv0_ref, bt0_ref), (kv1_ref, bt1_ref),
                                          (kv2_ref, bt2_ref))):
        q = new_ref[g, 0]
        k_new = new_ref[g, 1]
        v_new = new_ref[g, 2]
        s_rows = []
        for h in range(H_G):
            q_col = jnp.sum(jnp.where(eye, q[h:h + 1, :], 0.0), axis=1, keepdims=True)
            s_rows.append(jnp.sum(kv_ref[0, h] * q_col, axis=0, keepdims=True))
        s = jnp.concatenate(s_rows, axis=0) + bt_ref[...]
        s0 = jnp.sum(k_new * q, axis=1, keepdims=True) + b0_ref[g]
        m = jnp.maximum(jnp.max(s, axis=1, keepdims=True), s0)
        p = jnp.exp(s - m[:, 0:1])
        p0 = jnp.exp(s0 - m)
        l = jnp.sum(p, axis=1, keepdims=True) + p0
        o_rows = []
        for h in range(H_G):
            o_col = jnp.sum(kv_ref[1, h] * p[h:h + 1, :], axis=1, keepdims=True)
            o_rows.append(jnp.sum(jnp.where(eye, o_col, 0.0), axis=0, keepdims=True))
        o_ref[g] = (jnp.concatenate(o_rows, axis=0) + p0 * v_new) / l
        lse_ref[g] = m + jnp.log(l)


def _attn_sample(new_qkv, caches, layer, bias_t, bias_0):
    nreq = new_qkv.shape[0]
    views = []
    specs = []
    for g, cache in enumerate(caches):
        n_buf = cache.shape[2]
        assert n_buf == SPAN * DILATIONS[g], "cache must hold exactly one window"
        views.append(cache.transpose(0, 1, 3, 4, 5, 2))
        specs.append(pl.BlockSpec((None, None, 2, H_G, DH_A, n_buf),
                                  lambda b, layer=layer: (layer, b, 0, 0, 0, 0)))
    out = jax.ShapeDtypeStruct((nreq, N_GROUPS, H_G, DH_A), F32)
    out_spec = pl.BlockSpec((None, N_GROUPS, H_G, DH_A), lambda b: (b, 0, 0, 0))
    return pl.pallas_call(
        _sattn_kernel,
        grid=(nreq,),
        in_specs=[pl.BlockSpec((None, N_GROUPS, 3, H_G, DH_A), lambda b: (b, 0, 0, 0, 0))] + specs
        + [_resident(t.shape) for t in bias_t] + [_resident(bias_0.shape)],
        out_specs=[out_spec, out_spec],
        out_shape=[out, out],
        compiler_params=_params(1),
        name="attn_sample",
    )(new_qkv, *views, *bias_t, bias_0)


def _mlstm_kernel(*refs, n_seq):
    q_ref = refs[0]
    kt_refs = refs[1:1 + n_seq]
    vo_ref, gcol_ref = refs[1 + n_seq:3 + n_seq]
    grow_refs = refs[3 + n_seq:3 + 2 * n_seq]
    gb_row_ref, gb_col_ref, bout_ref, c_ref, n_ref, m_ref, nrep_s = refs[3 + 2 * n_seq:]
    chunk = q_ref.shape[1]

    @pl.when(pl.program_id(1) == 0)
    def _():
        c_ref[...] = jnp.zeros_like(c_ref)
        nrep_s[...] = jnp.zeros_like(nrep_s)
        m_ref[...] = jnp.zeros_like(m_ref)

    ti = lax.broadcasted_iota(jnp.int32, (chunk, chunk), 0)
    si = lax.broadcasted_iota(jnp.int32, (chunk, chunk), 1)
    causal = ti >= si
    lower = causal.astype(BF16)
    upper = (si >= ti).astype(BF16)
    items = [(j, h) for j in range(n_seq) for h in range(NH_B)]
    hcol = lambda h: slice(h * DK_B, (h + 1) * DK_B)
    a_rows, b_ts, b_lasts = [], [], []
    for j in range(n_seq):
        z_col = gcol_ref[j] + gb_row_ref[...]
        z_row = grow_refs[j][...] + gb_col_ref[...]
        b_col = sum(_dot(lower, part) for part in _split3(_log_sigmoid(z_col)))
        b_row = sum(_dot(part, upper) for part in _split3(_log_sigmoid(z_row)))
        for h in range(NH_B):
            a_rows.append(z_row[h:h + 1, :] - b_row[NH_B + h:NH_B + h + 1, :])
            b_ts.append(b_col[:, NH_B + h:NH_B + h + 1])
            b_lasts.append(b_row[NH_B + h:NH_B + h + 1, chunk - 1:chunk])
    a_row = jnp.stack(a_rows)
    b_t = jnp.stack(b_ts)
    m_prev = jnp.stack([m_ref[j, :, h:h + 1] for j, h in items])
    a_mat = jnp.where(causal, a_row, NEG)
    gmax = jnp.maximum(m_prev, jnp.max(a_mat, axis=-1, keepdims=True))
    dw = jnp.exp(a_mat - gmax)
    iw = jnp.exp(m_prev - gmax)
    g_last = gmax[:, chunk - 1:chunk, :]
    decay = jnp.exp(m_prev - g_last)
    w_state = jnp.exp(a_row - g_last)

    qb = [q_ref[j, :, hcol(h)] for j, h in items]
    kt = [kt_refs[j][hcol(h), :] for j, h in items]
    vb = [vo_ref[j, :, hcol(h)] for j, h in items]
    c_old = [c_ref[j, h] for j, h in items]
    n_old = [nrep_s[j, h] for j, h in items]
    idx = range(len(items))
    qk = jnp.stack([_dot(qb[i], kt[i]) for i in idx]) * dw
    qkb = qk.astype(BF16)
    q_c = jnp.stack([_dot(qb[i], c_old[i].astype(BF16)) for i in idx])
    qk_v = jnp.stack([_dot(qkb[i], vb[i]) for i in idx])
    num = iw * q_c + qk_v
    q_n = jnp.stack([_dot(qb[i], n_old[i].astype(BF16))[:, 0:1] for i in idx])
    den = iw * q_n + jnp.sum(qk, axis=-1, keepdims=True)
    hid = num / jnp.maximum(jnp.abs(den), jnp.exp(-(b_t + gmax)))
    for i, (j, h) in enumerate(items):
        o_gate = vo_ref[j, :, B_WIDTH + h * DK_B:B_WIDTH + (h + 1) * DK_B].astype(F32)
        bout_ref[j, :, hcol(h)] = (o_gate * hid[i]).astype(bout_ref.dtype)

    kwt = (jnp.stack(kt).astype(F32) * w_state).astype(BF16)
    ones = jnp.ones((chunk, LANES), BF16)
    for i, (j, h) in enumerate(items):
        c_ref[j, h] = decay[i] * c_old[i] + _dot(kwt[i], vb[i])
        nrep_s[j, h] = decay[i] * n_old[i] + _dot(kwt[i], ones)
        m_ref[j, :, h:h + 1] = b_lasts[i] + g_last[i]

    @pl.when(pl.program_id(1) == pl.num_programs(1) - 1)
    def _():
        for j, h in items:
            n_ref[j, h:h + 1, :] = nrep_s[j, h].T[0:1, :]


def _mlstm_prompt(q, kt, vo, gcol, grow, gb_row, gb_col, batch, seq):
    chunk = MLSTM_CHUNK
    n_seq = math.gcd(MLSTM_SEQS, batch)
    nc = seq // chunk
    seq3 = lambda b, c: (b, c, 0)
    lanes = [lambda b, c, j=j: (0, (b * n_seq + j) * nc + c) for j in range(n_seq)]
    state = lambda b, c: (b, 0, 0)
    return pl.pallas_call(
        functools.partial(_mlstm_kernel, n_seq=n_seq),
        grid=(batch // n_seq, nc),
        in_specs=[pl.BlockSpec((n_seq, chunk, B_WIDTH), seq3)]
        + [pl.BlockSpec((B_WIDTH, chunk), lanes[j]) for j in range(n_seq)]
        + [pl.BlockSpec((n_seq, chunk, COLS_VO), seq3), pl.BlockSpec((n_seq, chunk, LANES), seq3)]
        + [pl.BlockSpec((SUBLANES, chunk), lanes[j]) for j in range(n_seq)]
        + [_resident(gb_row.shape), _resident(gb_col.shape)],
        out_specs=[
            pl.BlockSpec((n_seq, chunk, B_WIDTH), seq3),
            pl.BlockSpec((n_seq, NH_B, DK_B, DK_B), lambda b, c: (b, 0, 0, 0)),
            pl.BlockSpec((n_seq, NH_B, DK_B), state),
            pl.BlockSpec((n_seq, 1, LANES), state),
        ],
        out_shape=[
            jax.ShapeDtypeStruct((batch, seq, B_WIDTH), BF16),
            jax.ShapeDtypeStruct((batch, NH_B, DK_B, DK_B), F32),
            jax.ShapeDtypeStruct((batch, NH_B, DK_B), F32),
            jax.ShapeDtypeStruct((batch, 1, LANES), F32),
        ],
        scratch_shapes=[pltpu.VMEM((n_seq, NH_B, DK_B, LANES), F32)],
        compiler_params=_params(2),
        name="mlstm_prompt",
    )(q.reshape(batch, seq, B_WIDTH), *([kt] * n_seq), vo.reshape(batch, seq, COLS_VO),
      gcol.reshape(batch, seq, LANES), *([grow] * n_seq), gb_row, gb_col)


def _smlstm_kernel(qk_ref, vo_ref, g_ref, gb_ref, cs_ref, cw_ref, cb_ref, c0_ref, n0_ref, m0_ref,
                   bout_ref, c1_ref, n1_ref, m1_ref, ncs_ref):
    u = qk_ref[...]
    y = cb_ref[...] + u * cw_ref[CONV_B - 1:CONV_B, :]
    for i in range(CONV_B - 1):
        y = y + cs_ref[i:i + 1, :] * cw_ref[i:i + 1, :]
    ncs_ref[0:CONV_B - 2, :] = cs_ref[1:CONV_B - 1, :]
    ncs_ref[CONV_B - 2:CONV_B - 1, :] = u
    y = y * _sigmoid(y)
    qf = y[:, :B_WIDTH]
    kf = y[:, B_WIDTH:] * (DK_B ** -0.5)
    z = g_ref[...] + gb_ref[...]
    lf_all = _log_sigmoid(z)
    m0 = m0_ref[...]
    eye = (lax.broadcasted_iota(jnp.int32, (DK_B, DK_B), 0)
           == lax.broadcasted_iota(jnp.int32, (DK_B, DK_B), 1))
    for h in range(NH_B):
        hs = slice(h * DK_B, (h + 1) * DK_B)
        ig = z[:, h:h + 1]
        inter = lf_all[:, NH_B + h:NH_B + h + 1] + m0[:, h:h + 1]
        m_t = jnp.maximum(inter, ig)
        dw = jnp.exp(ig - m_t)
        iw = jnp.exp(inter - m_t)
        qh = qf[:, hs]
        kh = kf[:, hs]
        vh = vo_ref[:, hs]
        c_old = c0_ref[h]
        n_old = n0_ref[h:h + 1, :]
        qk = jnp.sum(qh * kh, axis=1, keepdims=True) * dw
        num = iw * _hdot(qh, c_old) + qk * vh
        den = iw * jnp.sum(qh * n_old, axis=1, keepdims=True) + qk
        hid = num / jnp.maximum(jnp.abs(den), jnp.exp(-m_t))
        o_gate = _sigmoid(vo_ref[:, B_WIDTH + h * DK_B:B_WIDTH + (h + 1) * DK_B])
        bout_ref[:, hs] = o_gate * hid
        k_col = jnp.sum(jnp.where(eye, kh, 0.0), axis=1, keepdims=True)
        c1_ref[h] = iw * c_old + (dw * k_col) * vh
        n1_ref[h:h + 1, :] = iw * n_old + dw * kh
        m1_ref[:, h:h + 1] = m_t


def _mlstm_sample(qk_pre, vo, gcol, gb_row, conv_state, cw, cb, c0, n0, m0, layer):
    nreq = qk_pre.shape[0]
    one = lambda b: (b, 0, 0)
    lay3 = lambda b, layer=layer: (layer, b, 0, 0)
    lay4 = lambda b, layer=layer: (layer, b, 0, 0, 0)
    return pl.pallas_call(
        _smlstm_kernel,
        grid=(nreq,),
        in_specs=[
            pl.BlockSpec((None, 1, COLS_QK), one),
            pl.BlockSpec((None, 1, COLS_VO), one),
            pl.BlockSpec((None, 1, LANES), one),
            _resident(gb_row.shape),
            pl.BlockSpec((None, None, CONV_B - 1, COLS_QK), lay3),
            _resident(cw.shape), _resident(cb.shape),
            pl.BlockSpec((None, None, NH_B, DK_B, DK_B), lay4),
            pl.BlockSpec((None, None, NH_B, DK_B), lay3),
            pl.BlockSpec((None, None, 1, LANES), lay3),
        ],
        out_specs=[
            pl.BlockSpec((None, 1, B_WIDTH), one),
            pl.BlockSpec((None, NH_B, DK_B, DK_B), lambda b: (b, 0, 0, 0)),
            pl.BlockSpec((None, NH_B, DK_B), one),
            pl.BlockSpec((None, 1, LANES), one),
            pl.BlockSpec((None, CONV_B - 1, COLS_QK), one),
        ],
        out_shape=[
            jax.ShapeDtypeStruct((nreq, 1, B_WIDTH), F32),
            jax.ShapeDtypeStruct((nreq, NH_B, DK_B, DK_B), F32),
            jax.ShapeDtypeStruct((nreq, NH_B, DK_B), F32),
            jax.ShapeDtypeStruct((nreq, 1, LANES), F32),
            jax.ShapeDtypeStruct((nreq, CONV_B - 1, COLS_QK), F32),
        ],
        compiler_params=_params(1),
        name="mlstm_sample",
    )(qk_pre.reshape(nreq, 1, COLS_QK), vo.reshape(nreq, 1, COLS_VO),
      gcol.reshape(nreq, 1, LANES), gb_row, conv_state, cw, cb, c0, n0, m0)


def _merge_kernel(o0_ref, o1_ref, o2_ref, l0_ref, l1_ref, l2_ref, bo_ref, gab_ref, x_ref,
                  wpa_ref, wpb_ref, wo_ref, g2_ref, e_ref, x2_ref, h2_ref, *scratch):
    tm = x_ref.shape[0]

    def natural(ref, buf):
        dil, rows, _ = ref.shape
        n_slab = buf.shape[0]
        for r in range(dil):
            val = ref[r].astype(F32)
            for k in range(n_slab):
                buf[k, pl.ds(r, rows, stride=dil), :] = val[:, k * LANES:(k + 1) * LANES]
        return jnp.concatenate([buf[k] for k in range(n_slab)], axis=1)

    if scratch:
        os_ = [o0_ref[...].astype(F32), natural(o1_ref, scratch[0]), natural(o2_ref, scratch[1])]
        lses = [l0_ref[...], natural(l1_ref, scratch[2]), natural(l2_ref, scratch[3])]
    else:
        os_ = [r[...].astype(F32) for r in (o0_ref, o1_ref, o2_ref)]
        lses = [r[...] for r in (l0_ref, l1_ref, l2_ref)]
    top = jnp.maximum(jnp.maximum(lses[0], lses[1]), lses[2])
    ws = [jnp.exp(l - top) for l in lses]
    inv = 1.0 / (ws[0] + ws[1] + ws[2])
    e2 = e_ref[...]
    a = None
    for w, o in zip(ws, os_):
        alpha = w * inv
        hi = alpha.astype(BF16)
        lo = (alpha - hi.astype(F32)).astype(BF16)
        term = _dot(jnp.concatenate([hi, lo], axis=1), e2) * o
        a = term if a is None else a + term
    pa = _dot(a.astype(BF16), wpa_ref[...])
    pb = _dot(bo_ref[...].astype(BF16), wpb_ref[...])
    merged = (_sigmoid(gab_ref[:, :D_MODEL].astype(F32)) * pa
              + _sigmoid(gab_ref[:, D_MODEL:].astype(F32)) * pb)
    x2 = x_ref[...] + _dot(merged.astype(BF16), wo_ref[...])
    x2_ref[...] = x2
    h2_ref[...] = _rms(x2, g2_ref[...]).astype(BF16)


def _merge(os_, lses, bout, gab, x, wpa, wpb, wo, g2, e_bf, *, layer, tm, seq=None):
    n = x.shape[0]
    row = lambda i: (i, 0)

    def group_specs(cols):
        specs = [pl.BlockSpec((tm, cols), row)]
        for g in (1, 2):
            if seq is None:
                specs.append(pl.BlockSpec((tm, cols), row))
            else:
                d = DILATIONS[g]
                tpb = seq // tm
                specs.append(pl.BlockSpec((None, d, tm // d, cols),
                                          lambda i, tpb=tpb: (i // tpb, 0, i % tpb, 0)))
        return specs

    scratch = []
    if seq is not None:
        scratch = ([pltpu.VMEM((A_GROUP // LANES, tm, LANES), F32)] * 2
                   + [pltpu.VMEM((1, tm, LANES), F32)] * 2)
    return pl.pallas_call(
        _merge_kernel,
        grid=(n // tm,),
        scratch_shapes=scratch,
        in_specs=group_specs(A_GROUP) + group_specs(LANES) + [
            pl.BlockSpec((tm, B_WIDTH), row),
            pl.BlockSpec((tm, COLS_GAB), row),
            pl.BlockSpec((tm, D_MODEL), row),
            _layer_resident(wpa.shape, layer), _layer_resident(wpb.shape, layer),
            _layer_resident(wo.shape, layer), _resident(g2.shape), _resident(e_bf.shape),
        ],
        out_specs=[pl.BlockSpec((tm, D_MODEL), row), pl.BlockSpec((tm, D_MODEL), row)],
        out_shape=[jax.ShapeDtypeStruct((n, D_MODEL), F32), jax.ShapeDtypeStruct((n, D_MODEL), BF16)],
        compiler_params=_params(1),
        name="merge",
    )(*os_, *lses, bout, gab, x, wpa, wpb, wo, g2, e_bf)


def _ffn_kernel(*refs, tiles_per_batch, from_state, final_norm):
    h2_ref, x2_ref, wup_ref, wdn_ref, cw_ref, cb_ref = refs[:6]
    pos = 6
    if from_state:
        prev_refs = refs[pos:pos + CONV_F - 1]
        pos += CONV_F - 1
    if final_norm:
        fg_ref = refs[pos]
        pos += 1
    x3_ref, u_ref = refs[pos:pos + 2]
    tm = h2_ref.shape[0]
    pad = SUBLANES
    if not from_state:
        ext_s = refs[pos + 2]

        @pl.when(pl.program_id(0) % tiles_per_batch == 0)
        def _():
            ext_s[:, 0:pad, :] = jnp.zeros((ext_s.shape[0], pad, LANES), F32)

    h2 = h2_ref[...]
    acc = x2_ref[...]
    n_chunk = D_FF // FF_CHUNK

    def halves(c):
        return (c * FF_CHUNK, D_FF + c * FF_CHUNK)

    def up(c):
        us = []
        for off in halves(c):
            u = _dot(h2, wup_ref[:, off:off + FF_CHUNK])
            if from_state:
                u_ref[:, off:off + FF_CHUNK] = u
                us.append(u)
            else:
                for k in range(FF_CHUNK // LANES):
                    ext_s[off // LANES + k, pad:pad + tm, :] = u[:, k * LANES:(k + 1) * LANES]
        return us

    def conv(c, us):
        ys = []
        for j, off in enumerate(halves(c)):
            cs = slice(off, off + FF_CHUNK)
            if from_state:
                y = cb_ref[:, cs] + us[j] * cw_ref[CONV_F - 1:CONV_F, cs]
                for i in range(CONV_F - 1):
                    y = y + prev_refs[i][:, cs] * cw_ref[i:i + 1, cs]
            else:
                parts = []
                for k in range(FF_CHUNK // LANES):
                    slab = off // LANES + k
                    ks = slice(off + k * LANES, off + (k + 1) * LANES)
                    part = cb_ref[:, ks]
                    for i in range(CONV_F):
                        lo = pad - (CONV_F - 1) + i
                        part = part + ext_s[slab, lo:lo + tm, :] * cw_ref[i:i + 1, ks]
                    parts.append(part)
                    tail = ext_s[slab, tm:tm + pad, :]
                    u_ref[:, ks] = tail
                    ext_s[slab, 0:pad, :] = tail
                y = jnp.concatenate(parts, axis=1)
            ys.append(y)
        return ys

    ahead = 1 if from_state else FF_AHEAD
    pending = [up(c) for c in range(min(ahead, n_chunk))]
    for c in range(n_chunk):
        us = pending.pop(0)
        if c + ahead < n_chunk:
            pending.append(up(c + ahead))
        y_act, y_gate = conv(c, us)
        act = _gelu_tanh(y_act) * y_gate
        acc = acc + _dot(act.astype(BF16), wdn_ref[c * FF_CHUNK:(c + 1) * FF_CHUNK, :])
    if final_norm:
        acc = _rms(acc, fg_ref[...])
    x3_ref[...] = acc


def _ffn(h2, x2, wup, wdn, cw, cb, *, layer, tm, tiles_per_batch=None, prev_rows=None,
         final_g=None):
    n = h2.shape[0]
    from_state = prev_rows is not None
    row = lambda i: (i, 0)
    in_specs = [
        pl.BlockSpec((tm, D_MODEL), row), pl.BlockSpec((tm, D_MODEL), row),
        _layer_resident(wup.shape, layer), _layer_resident(wdn.shape, layer),
        _resident(cw.shape), _resident(cb.shape),
    ]
    args = [h2, x2, wup, wdn, cw, cb]
    scratch = []
    if from_state:
        in_specs += [pl.BlockSpec((tm, 2 * D_FF), row)] * (CONV_F - 1)
        args += list(prev_rows)
        u_shape = jax.ShapeDtypeStruct((n, 2 * D_FF), F32)
        u_spec = pl.BlockSpec((tm, 2 * D_FF), row)
    else:
        n_batch = n // (tm * tiles_per_batch)
        u_shape = jax.ShapeDtypeStruct((n_batch, SUBLANES, 2 * D_FF), F32)
        u_spec = pl.BlockSpec((None, SUBLANES, 2 * D_FF), lambda i: (i // tiles_per_batch, 0, 0))
        scratch = [pltpu.VMEM((2 * D_FF // LANES, tm + SUBLANES, LANES), F32)]
    if final_g is not None:
        in_specs.append(_resident(final_g.shape))
        args.append(final_g)
    return pl.pallas_call(
        functools.partial(_ffn_kernel, tiles_per_batch=tiles_per_batch, from_state=from_state,
                          final_norm=final_g is not None),
        grid=(n // tm,),
        in_specs=in_specs,
        out_specs=[pl.BlockSpec((tm, D_MODEL), row), u_spec],
        out_shape=[jax.ShapeDtypeStruct((n, D_MODEL), F32), u_shape],
        scratch_shapes=scratch,
        compiler_params=_params(1),
        name="ffn",
    )(*args)


def _t5_bucket(dist):
    max_exact = NUM_BUCKETS // 2
    df = jnp.maximum(dist, 1).astype(F32)
    large = max_exact + (jnp.log(df / max_exact) / math.log(MAX_DISTANCE / max_exact)
                         * (NUM_BUCKETS - max_exact)).astype(jnp.int32)
    large = jnp.minimum(large, NUM_BUCKETS - 1)
    return jnp.where(dist < max_exact, dist, large)


def _bias_table(rel_bias, g, dist):
    bucket = _t5_bucket(jnp.asarray(dist, jnp.int32))
    table = rel_bias[:, g * H_G:(g + 1) * H_G].reshape((NUM_BUCKETS, H_G) + (1,) * bucket.ndim)
    ids = jnp.arange(NUM_BUCKETS).reshape((NUM_BUCKETS, 1) + (1,) * bucket.ndim)
    return jnp.sum(jnp.where(bucket[None, None] == ids, table, 0.0), axis=0)


def _prompt_bias(rel_bias, g):
    qi = np.arange(Q_BLOCK)[:, None]
    ki = np.arange(2 * Q_BLOCK)[None, :]
    rel = qi + Q_BLOCK - ki
    band = (rel >= 0) & (rel <= SPAN)
    bias = _bias_table(rel_bias, g, np.maximum(rel, 0) * DILATIONS[g])
    return jnp.where(band[None], bias, NEG)


def _sample_bias(rel_bias):
    tables, news = [], []
    for g in range(N_GROUPS):
        dil = DILATIONS[g]
        n_buf = SPAN * dil
        pos = np.arange(n_buf)
        bias = _bias_table(rel_bias, g, n_buf - pos)
        tables.append(jnp.where((pos % dil == 0)[None], bias, NEG))
        news.append(jnp.broadcast_to(_bias_table(rel_bias, g, np.zeros((1,), np.int32)), (H_G, DH_A)))
    return tables, jnp.stack(news)


def _head_indicator():
    e = np.zeros((LANES, A_GROUP), np.float32)
    for h in range(H_G):
        e[h, h * DH_A:(h + 1) * DH_A] = 1.0
    return e


def kernel(x_prompt, x_sample, cache_kv_w128, cache_kv_w512, cache_kv_w2048, state_mlstm_conv,
           state_mlstm_C, state_mlstm_n, state_mlstm_m, state_ffn_conv, rel_bias, norm1_g, w_in,
           mconv_w, mconv_b, mgate_b, w_pa, w_pb, w_o, norm2_g, w_up, fconv_w, fconv_b, w_down,
           final_norm_g):
    batch, seq, _ = x_prompt.shape
    nreq = x_sample.shape[0]
    depth = w_in.shape[0]
    n_p = batch * seq
    caches = (cache_kv_w128, cache_kv_w512, cache_kv_w2048)

    assert seq % (Q_BLOCK * DILATIONS[-1]) == 0 and seq >= WINDOWS[-1]
    e_bf = jnp.asarray(np.concatenate([_head_indicator()] * 2, axis=0), BF16)
    prompt_bias = [_prompt_bias(rel_bias, g) for g in range(N_GROUPS)]
    sbias_m, sbias_0 = _sample_bias(rel_bias)
    m0_all = jnp.pad(state_mlstm_m, ((0, 0), (0, 0), (0, LANES - NH_B)))[:, :, None, :]
    fg = final_norm_g.reshape(1, D_MODEL)

    wpa, wpb, wo = w_pa.astype(BF16), w_pb.astype(BF16), w_o.astype(BF16)
    wup, wdn = w_up.astype(BF16), w_down.astype(BF16)

    xp = x_prompt.reshape(n_p, D_MODEL)
    xs = x_sample.reshape(nreq, D_MODEL)
    p_st = [[] for _ in range(8)]
    s_st = [[] for _ in range(8)]

    for l in range(depth):
        last = l == depth - 1
        w_main, w_gate = _regroup_w_in(w_in, l)
        g1 = norm1_g[l].reshape(1, D_MODEL)
        g2 = norm2_g[l].reshape(1, D_MODEL)
        gate_bias = mgate_b[l].reshape(1, 2 * NH_B)
        gb_row = jnp.pad(gate_bias, ((0, 0), (0, LANES - 2 * NH_B)))
        gb_col = gate_bias.reshape(2 * NH_B, 1)
        mcw, mcb = mconv_w[l], mconv_b[l].reshape(1, COLS_QK)
        fcw, fcb = fconv_w[l], fconv_b[l].reshape(1, 2 * D_FF)

        *qkvs, q_b, vo, gab, gcol, grow, conv_tail, kt_b = _inproj(
            xp, g1, w_main, w_gate, tm=INPROJ_ROWS, act_dtype=BF16, seq=seq, conv=(mcw, mcb))
        os_, lses = [], []
        for g in range(N_GROUPS):
            d = DILATIONS[g]
            o_g, lse_g = _attn_prompt(qkvs[g].reshape(batch * d, seq // d, COLS_QKV_G), prompt_bias[g])
            shape = (n_p,) if g == 0 else (batch, d, seq // d)
            os_.append(o_g.reshape(shape + (A_GROUP,)))
            lses.append(lse_g.reshape(shape + (LANES,)))
        bout, c_p, n_p_state, m_p = _mlstm_prompt(q_b, kt_b, vo, gcol, grow, gb_row, gb_col,
                                                  batch, seq)
        x2, h2 = _merge(os_, lses, bout.reshape(n_p, B_WIDTH), gab, xp, wpa, wpb, wo, g2, e_bf,
                        layer=l, tm=MERGE_ROWS, seq=seq)
        xp, u_tail = _ffn(h2, x2, wup, wdn, fcw, fcb, layer=l, tm=FFN_ROWS,
                          tiles_per_batch=seq // FFN_ROWS, final_g=fg if last else None)

        for g in range(N_GROUPS):
            p_st[g].append(qkvs[g].reshape(batch, DILATIONS[g], seq // DILATIONS[g], COLS_QKV_G))
        p_st[3].append(conv_tail[:, SUBLANES - (CONV_B - 1):])
        p_st[4].append(c_p)
        p_st[5].append(n_p_state)
        p_st[6].append(m_p[:, 0, :NH_B])
        p_st[7].append(u_tail[:, SUBLANES - (CONV_F - 1):])

        *qkvs_s, qk_s, vo_s, gab_s, gcol_s = _inproj(xs, g1, w_main, w_gate, tm=nreq, act_dtype=F32)
        new_qkv = jnp.stack(qkvs_s, axis=1).reshape(nreq, N_GROUPS, 3, H_G, DH_A)
        o_s, lse_s = _attn_sample(new_qkv, caches, l, sbias_m, sbias_0)
        bout_s, c_s, n_s, m_s, conv_s = _mlstm_sample(qk_s, vo_s, gcol_s, gb_row, state_mlstm_conv,
                                                      mcw, mcb, state_mlstm_C, state_mlstm_n, m0_all, l)
        lse_pad = jnp.pad(lse_s[..., 0], ((0, 0), (0, 0), (0, LANES - H_G)))
        x2_s, h2_s = _merge([o_s[:, g].reshape(nreq, A_GROUP) for g in range(N_GROUPS)],
                            [lse_pad[:, g] for g in range(N_GROUPS)],
                            bout_s.reshape(nreq, B_WIDTH), gab_s, xs, wpa, wpb, wo, g2, e_bf,
                            layer=l, tm=nreq)
        fbuf = state_ffn_conv[l]
        xs, u_s = _ffn(h2_s, x2_s, wup, wdn, fcw, fcb, layer=l, tm=nreq,
                       prev_rows=[fbuf[:, i] for i in range(CONV_F - 1)],
                       final_g=fg if last else None)

        for g in range(N_GROUPS):
            s_st[g].append(new_qkv[:, g, 1:][:, None])
        s_st[3].append(conv_s)
        s_st[4].append(c_s)
        s_st[5].append(n_s)
        s_st[6].append(m_s[:, 0, :NH_B])
        s_st[7].append(jnp.concatenate([fbuf[:, 1:], u_s[:, None, :]], axis=1))

    outs = [xp.reshape(batch, seq, D_MODEL), xs.reshape(nreq, 1, D_MODEL)]
    for i in range(8):
        if i < N_GROUPS:
            kv = _kv_tail(p_st[i], i).reshape(depth, batch, 2, H_G, DH_A, SPAN * DILATIONS[i])
            outs.append(kv.transpose(0, 1, 5, 2, 3, 4))
        else:
            outs.append(jnp.stack(p_st[i], 0))
        outs.append(jnp.stack(s_st[i], 0))
    return tuple(outs)
```

```python
import functools
import math

import numpy as np
import jax
import jax.numpy as jnp
from jax import lax
from jax.experimental import pallas as pl
from jax.experimental.pallas import tpu as pltpu

F32 = jnp.float32
BF16 = jnp.bfloat16
HIGHEST = lax.Precision.HIGHEST

D_MODEL = 1024
WINDOWS = (128, 512, 2048)
DILATIONS = (1, 4, 16)
N_GROUPS = 3
H_G = 8
DH_A = 64
A_GROUP = H_G * DH_A
Q_BLOCK = 128
SPAN = 128
NH_B = 4
DK_B = 256
B_WIDTH = NH_B * DK_B
CONV_B = 4
D_FF = 2816
CONV_F = 3
NUM_BUCKETS = 32
MAX_DISTANCE = 2048
RMS_EPS = 1e-6
NEG = -1e30

LANES = 128
SUBLANES = 8
FF_CHUNK = 256
PROJ_CHUNK = 512
FF_AHEAD = 3
MLSTM_CHUNK = 256
MLSTM_SEQS = 2
ATTN_BLOCKS = 16
VMEM_LIMIT = 56 * 1024 * 1024
INPROJ_ROWS = 512
MERGE_ROWS = 512
FFN_ROWS = 512

COLS_QKV_G = 3 * A_GROUP
COLS_QKV = N_GROUPS * COLS_QKV_G
COLS_QK = 2 * B_WIDTH
COLS_VO = 2 * B_WIDTH
COLS_GAB = 2 * D_MODEL
GATE_COL0 = COLS_QKV + COLS_QK + COLS_VO


def _dot(a, b):
    return jnp.dot(a, b, preferred_element_type=F32)


def _hdot(a, b):
    return jnp.dot(a, b, precision=HIGHEST, preferred_element_type=F32)


def _dot_nt(a, b):
    return lax.dot_general(a, b, (((1,), (1,)), ((), ())), preferred_element_type=F32)


def _split3(x):
    hi = x.astype(BF16)
    rest = x - hi.astype(F32)
    mid = rest.astype(BF16)
    return hi, mid, (rest - mid.astype(F32)).astype(BF16)


def _sigmoid(x):
    return 1.0 / (1.0 + jnp.exp(-x))


def _log_sigmoid(x):
    return jnp.minimum(x, 0.0) - jnp.log1p(jnp.exp(-jnp.abs(x)))


def _gelu_tanh(x):
    return 0.5 * x * (1.0 + jnp.tanh(math.sqrt(2.0 / math.pi) * (x + 0.044715 * (x * x * x))))


def _rms(x, g):
    return x * lax.rsqrt(jnp.mean(x * x, axis=-1, keepdims=True) + RMS_EPS) * g


def _resident(shape):
    nd = len(shape)
    return pl.BlockSpec(shape, lambda *_: (0,) * nd, pipeline_mode=pl.Buffered(1))


def _layer_resident(stacked_shape, layer):
    nd = len(stacked_shape)
    return pl.BlockSpec((None,) + tuple(stacked_shape[1:]), lambda *_: (layer,) + (0,) * (nd - 1),
                        pipeline_mode=pl.Buffered(1))


def _params(n_grid):
    return pltpu.CompilerParams(dimension_semantics=("arbitrary",) * n_grid,
                                vmem_limit_bytes=VMEM_LIMIT)


W_BLOCK = 512
N_QKV_BLOCKS = COLS_QKV // W_BLOCK
N_HEAD_BLOCKS = GATE_COL0 // W_BLOCK


def _regroup_kernel(head_ref, tail_ref, gate_ref, o_ref, og_ref):
    j = pl.program_id(0)

    @pl.when(j == 0)
    def _():
        lane = lax.broadcasted_iota(jnp.int32, og_ref.shape, 1)
        og_ref[...] = jnp.where(lane < 2 * NH_B, gate_ref[...].T, 0.0).astype(og_ref.dtype)

    @pl.when(j < N_HEAD_BLOCKS)
    def _():
        is_q = (j < N_QKV_BLOCKS) & (j % 3 == 0)
        scale = jnp.where(is_q, DH_A ** -0.5, 1.0)
        o_ref[...] = (head_ref[...] * scale).T.astype(o_ref.dtype)

    @pl.when(j >= N_HEAD_BLOCKS)
    def _():
        o_ref[...] = tail_ref[0].T.astype(o_ref.dtype)


def _regroup_w_in(w_stack, layer):
    wt = w_stack.transpose(0, 2, 1)
    n_blocks = N_HEAD_BLOCKS + COLS_GAB // W_BLOCK
    tail0 = GATE_COL0 + 2 * NH_B
    assert GATE_COL0 % LANES == 0 and tail0 % SUBLANES == 0

    def head_block(j):
        regrouped = (j % 3) * N_GROUPS + j // 3
        return layer, jnp.where(j < N_QKV_BLOCKS, regrouped, jnp.minimum(j, N_HEAD_BLOCKS - 1)), 0

    return pl.pallas_call(
        _regroup_kernel,
        grid=(n_blocks,),
        in_specs=[
            pl.BlockSpec((None, W_BLOCK, D_MODEL), head_block),
            pl.BlockSpec((pl.Element(1), pl.Element(W_BLOCK), pl.Element(D_MODEL)),
                         lambda j: (layer, pl.multiple_of(
                             tail0 + jnp.maximum(j - N_HEAD_BLOCKS, 0) * W_BLOCK, SUBLANES), 0)),
            pl.BlockSpec((None, LANES, D_MODEL), lambda j: (layer, GATE_COL0 // LANES, 0)),
        ],
        out_specs=[pl.BlockSpec((D_MODEL, W_BLOCK), lambda j: (0, j)),
                   pl.BlockSpec((D_MODEL, LANES), lambda j: (0, 0))],
        out_shape=[jax.ShapeDtypeStruct((D_MODEL, n_blocks * W_BLOCK), BF16),
                   jax.ShapeDtypeStruct((D_MODEL, LANES), BF16)],
        compiler_params=_params(1),
        name="regroup_w_in",
    )(wt, wt, wt)


def _inproj_kernel(*refs, prompt, tiles_per_batch):
    if prompt:
        (x_ref, g_ref, w_ref, wg_ref, cw_ref, cb_ref, qkv0_ref, qkv1_ref, qkv2_ref, qk_ref, vo_ref,
         gab_ref, gcol_ref, grow_ref, ctail_ref, kt_ref, hs_ref, ext_ref) = refs
    else:
        (x_ref, g_ref, w_ref, wg_ref, qkv0_ref, qkv1_ref, qkv2_ref, qk_ref, vo_ref, gab_ref,
         gcol_ref) = refs
    if prompt:
        @pl.when(pl.program_id(0) % tiles_per_batch == 0)
        def _():
            ext_ref[:, 0:SUBLANES, :] = jnp.zeros((ext_ref.shape[0], SUBLANES, LANES), F32)

    hf = _rms(x_ref[...], g_ref[...])
    h = hf.astype(BF16)
    tm = hf.shape[0]

    pw = PROJ_CHUNK

    def project(lhs, col0, width, store):
        for c in range(0, width, pw):
            store(c, _dot(lhs, w_ref[:, col0 + c:col0 + c + pw]))

    def store_rows(ref):
        def store(c, res):
            ref[:, c:c + pw] = res.astype(ref.dtype)
        return store

    if prompt:
        pad = SUBLANES

        def store_slabs(c, res):
            for k in range(pw // LANES):
                ext_ref[c // LANES + k, pad:pad + tm, :] = res[:, k * LANES:(k + 1) * LANES]
        project(h, COLS_QKV, COLS_QK, store_slabs)
    project(h, 0, COLS_QKV_G, store_rows(qkv0_ref))
    if prompt:
        n_slab = hs_ref.shape[0]
        for k in range(n_slab):
            hs_ref[k] = hf[:, k * LANES:(k + 1) * LANES]
        for g, ref in ((1, qkv1_ref), (2, qkv2_ref)):
            d = DILATIONS[g]
            rows = tm // d
            hp = jnp.concatenate(
                [jnp.concatenate([hs_ref[k, pl.ds(r, rows, stride=d), :] for r in range(d)], axis=0)
                 for k in range(n_slab)], axis=1).astype(BF16)

            def store(c, res, ref=ref, d=d, rows=rows):
                res = res.astype(ref.dtype)
                for r in range(d):
                    ref[r, :, c:c + pw] = res[r * rows:(r + 1) * rows]
            project(hp, g * COLS_QKV_G, COLS_QKV_G, store)
    else:
        project(h, COLS_QKV_G, COLS_QKV_G, store_rows(qkv1_ref))
        project(h, 2 * COLS_QKV_G, COLS_QKV_G, store_rows(qkv2_ref))
    if prompt:
        for k in range(COLS_QK // LANES):
            ks = slice(k * LANES, (k + 1) * LANES)
            y = cb_ref[:, ks] + ext_ref[k, pad:pad + tm, :] * cw_ref[CONV_B - 1:CONV_B, ks]
            for i in range(CONV_B - 1):
                lo = pad - (CONV_B - 1) + i
                y = y + ext_ref[k, lo:lo + tm, :] * cw_ref[i:i + 1, ks]
            y = y * _sigmoid(y)
            if k * LANES < B_WIDTH:
                qk_ref[:, ks] = y.astype(qk_ref.dtype)
            else:
                kt_ref[k * LANES - B_WIDTH:(k + 1) * LANES - B_WIDTH, :] = (
                    (y * (DK_B ** -0.5)).T.astype(kt_ref.dtype))
            tail = ext_ref[k, tm:tm + pad, :]
            ctail_ref[:, ks] = tail
            ext_ref[k, 0:pad, :] = tail
    else:
        project(h, COLS_QKV, COLS_QK, store_rows(qk_ref))
    col = COLS_QKV + COLS_QK
    if prompt:
        def store_vo(c, res):
            vo_ref[:, c:c + pw] = (res if c < B_WIDTH else _sigmoid(res)).astype(vo_ref.dtype)
        project(h, col, COLS_VO, store_vo)

        def store_gab(c, res):
            gab_ref[:, c:c + pw] = _sigmoid(res).astype(gab_ref.dtype)
        project(h, col + COLS_VO, COLS_GAB, store_gab)
    else:
        project(h, col, COLS_VO, store_rows(vo_ref))
        project(h, col + COLS_VO, COLS_GAB, store_rows(gab_ref))
    gates = _dot(h, wg_ref[...])
    gcol_ref[...] = gates
    if prompt:
        grow_ref[...] = gates.T[:SUBLANES, :]


def _inproj(x, gain, w_main, w_gate, *, tm, act_dtype, seq=None, conv=None):
    n = x.shape[0]
    dilate = seq is not None
    row = lambda i: (i, 0)
    out_shape = [jax.ShapeDtypeStruct((n, COLS_QKV_G), act_dtype)]
    out_specs = [pl.BlockSpec((tm, COLS_QKV_G), row)]
    for g in (1, 2):
        if dilate:
            d = DILATIONS[g]
            tpb = seq // tm
            out_shape.append(jax.ShapeDtypeStruct((n // seq, d, seq // d, COLS_QKV_G), act_dtype))
            out_specs.append(pl.BlockSpec((None, d, tm // d, COLS_QKV_G),
                                          lambda i, tpb=tpb: (i // tpb, 0, i % tpb, 0)))
        else:
            out_shape.append(jax.ShapeDtypeStruct((n, COLS_QKV_G), act_dtype))
            out_specs.append(pl.BlockSpec((tm, COLS_QKV_G), row))
    for cols, dt in ((B_WIDTH, act_dtype) if dilate else (COLS_QK, F32), (COLS_VO, act_dtype),
                     (COLS_GAB, act_dtype), (LANES, F32)):
        out_shape.append(jax.ShapeDtypeStruct((n, cols), dt))
        out_specs.append(pl.BlockSpec((tm, cols), row))
    in_specs = [
        pl.BlockSpec((tm, D_MODEL), row),
        _resident((1, D_MODEL)),
        _resident(w_main.shape),
        _resident(w_gate.shape),
    ]
    args = [x, gain, w_main, w_gate]
    scratch = []
    tpb = None
    if dilate:
        tpb = seq // tm
        in_specs += [_resident(conv[0].shape), _resident(conv[1].shape)]
        args += list(conv)
        out_shape.append(jax.ShapeDtypeStruct((SUBLANES, n), F32))
        out_specs.append(pl.BlockSpec((SUBLANES, tm), lambda i: (0, i)))
        out_shape.append(jax.ShapeDtypeStruct((n // seq, SUBLANES, COLS_QK), F32))
        out_specs.append(pl.BlockSpec((None, SUBLANES, COLS_QK), lambda i: (i // tpb, 0, 0)))
        out_shape.append(jax.ShapeDtypeStruct((B_WIDTH, n), act_dtype))
        out_specs.append(pl.BlockSpec((B_WIDTH, tm), lambda i: (0, i)))
        scratch =[pltpu.VMEM((D_MODEL // LANES, tm, LANES), F32),
                   pltpu.VMEM((COLS_QK // LANES, tm + SUBLANES, LANES), F32)]
    return pl.pallas_call(
        functools.partial(_inproj_kernel, prompt=dilate, tiles_per_batch=tpb),
        grid=(n // tm,),
        in_specs=in_specs,
        out_specs=out_specs,
        out_shape=out_shape,
        scratch_shapes=scratch,
        compiler_params=_params(1),
        name="inproj",
    )(*args)


def _attn_kernel(q_ref, kp_ref, kc_ref, vp_ref, vc_ref, bias_ref, o_ref, lse_ref):
    n_blk = q_ref.shape[0] // Q_BLOCK
    has_prev = pl.program_id(1) > 0
    key_lane = lax.broadcasted_iota(jnp.int32, (1, 1, 2 * Q_BLOCK), 2)
    first_mask = jnp.where((key_lane < Q_BLOCK) & jnp.logical_not(has_prev), NEG, 0.0)
    lane = lax.broadcasted_iota(jnp.int32, (Q_BLOCK, LANES), 1)
    low_half = lane < DH_A
    pair = 2 * DH_A

    def window(i, cur_ref, first_ref):
        rows = slice(i * Q_BLOCK, (i + 1) * Q_BLOCK)
        prev_ref, prows = ((first_ref, slice(0, Q_BLOCK)) if i == 0
                           else (cur_ref, slice((i - 1) * Q_BLOCK, i * Q_BLOCK)))
        return [jnp.concatenate([prev_ref[prows, hp * pair:(hp + 1) * pair],
                                 cur_ref[rows, hp * pair:(hp + 1) * pair]], axis=0)
                for hp in range(H_G // 2)]

    def qk_scores(i):
        rows = slice(i * Q_BLOCK, (i + 1) * Q_BLOCK)
        scores = []
        for hp, kk in enumerate(window(i, kc_ref, kp_ref)):
            qp = q_ref[rows, hp * pair:(hp + 1) * pair]
            scores.append(_dot_nt(jnp.where(low_half, qp, jnp.zeros_like(qp)), kk))
            scores.append(_dot_nt(jnp.where(low_half, jnp.zeros_like(qp), qp), kk))
        return jnp.stack(scores)

    for i in range(n_blk):
        rows = slice(i * Q_BLOCK, (i + 1) * Q_BLOCK)
        s = qk_scores(i) + bias_ref[...]
        if i == 0:
            s = s + first_mask
        m = jnp.max(s, axis=-1, keepdims=True)
        p = jnp.exp(s - m)
        l = jnp.sum(p, axis=-1, keepdims=True)
        pb = p.astype(BF16)
        inv = 1.0 / l
        lse = m + jnp.log(l)
        lse_all = jnp.zeros((Q_BLOCK, LANES), F32)
        for hp, vv in enumerate(window(i, vc_ref, vp_ref)):
            cols = slice(hp * pair, (hp + 1) * pair)
            o_lo = _dot(pb[2 * hp], vv) * inv[2 * hp]
            o_hi = _dot(pb[2 * hp + 1], vv) * inv[2 * hp + 1]
            o_ref[rows, cols] = jnp.where(low_half, o_lo, o_hi).astype(o_ref.dtype)
        for h in range(H_G):
            lse_all = jnp.where(lane == h, lse[h], lse_all)
        lse_ref[rows, :] = lse_all


def _attn_prompt(qkv, bias):
    nsub, u_len, _ = qkv.shape
    n_blk = math.gcd(ATTN_BLOCKS, u_len // Q_BLOCK)
    rows = n_blk * Q_BLOCK
    nb = u_len // rows

    def spec(col_block, prev):
        if prev:
            return pl.BlockSpec((None, Q_BLOCK, A_GROUP),
                                lambda s, j: (s, jnp.maximum(j * n_blk - 1, 0), col_block))
        return pl.BlockSpec((None, rows, A_GROUP), lambda s, j: (s, j, col_block))

    return pl.pallas_call(
        _attn_kernel,
        grid=(nsub, nb),
        in_specs=[spec(0, False), spec(1, True), spec(1, False), spec(2, True), spec(2, False),
                  _resident(bias.shape)],
        out_specs=[
            pl.BlockSpec((None, rows, A_GROUP), lambda s, j: (s, j, 0)),
            pl.BlockSpec((None, rows, LANES), lambda s, j: (s, j, 0)),
        ],
        out_shape=[
            jax.ShapeDtypeStruct((nsub, u_len, A_GROUP), BF16),
            jax.ShapeDtypeStruct((nsub, u_len, LANES), F32),
        ],
        compiler_params=_params(2),
        name="attn_prompt",
    )(qkv, qkv, qkv, qkv, qkv, bias)


def _kvtail_kernel(*refs):
    out_ref, nat_s = refs[-2:]
    srcs = refs[:-2]
    n_slab = nat_s.shape[0]
    for layer in range(len(srcs) // 2):
        @pl.when(pl.program_id(0) == layer)
        def _():
            for j, ref in enumerate(srcs[2 * layer:2 * layer + 2]):
                dil = ref.shape[0]
                for r in range(dil):
                    val = ref[r].astype(F32)
                    for s in range(n_slab):
                        nat_s[s, pl.ds(r, SPAN, stride=dil), :] = val[:, s * LANES:(s + 1) * LANES]
                for s in range(n_slab):
                    out_ref[j, s * LANES:(s + 1) * LANES, :] = nat_s[s].T


def _kv_tail(qkv_layers, g):
    depth = len(qkv_layers)
    batch, dil, u_len, _ = qkv_layers[0].shape
    keep = SPAN * dil
    last_blk = u_len // SPAN - 1
    specs, args = [], []
    for layer, arr in enumerate(qkv_layers):
        idle_b = 0 if layer > 0 else batch - 1
        for col in (1, 2):
            specs.append(pl.BlockSpec(
                (None, dil, SPAN, A_GROUP),
                lambda l, b, col=col, layer=layer, idle_b=idle_b: (
                    jnp.where(l == layer, b, idle_b), 0, last_blk, col)))
            args.append(arr)
    return pl.pallas_call(
        _kvtail_kernel,
        grid=(depth, batch),
        in_specs=specs,
        out_specs=pl.BlockSpec((None, None, 2, A_GROUP, keep), lambda l, b: (l, b, 0, 0, 0)),
        out_shape=jax.ShapeDtypeStruct((depth, batch, 2, A_GROUP, keep), F32),
        scratch_shapes=[pltpu.VMEM((A_GROUP // LANES, keep, LANES), F32)],
        compiler_params=_params(2),
        name=f"kv_tail_g{g}",
    )(*args)


def _sattn_kernel(new_ref, kv0_ref, kv1_ref, kv2_ref, bt0_ref, bt1_ref, bt2_ref, b0_ref,
                  o_ref, lse_ref):
    eye = (lax.broadcasted_iota(jnp.int32, (DH_A, DH_A), 0)
           == lax.broadcasted_iota(jnp.int32, (DH_A, DH_A), 1))
    for g, (kv_ref, bt_ref) in enumerate(((kv0_ref, bt0_ref), (kv1_ref, bt1_ref),
                                          (kv2_ref, bt2_ref))):
        q = new_ref[g, 0]
        k_new = new_ref[g, 1]
        v_new = new_ref[g, 2]
        s_rows = []
        for h in range(H_G):
            q_col = jnp.sum(jnp.where(eye, q[h:h + 1, :], 0.0), axis=1, keepdims=True)
            s_rows.append(jnp.sum(kv_ref[0, h] * q_col, axis=0, keepdims=True))
        s = jnp.concatenate(s_rows, axis=0) + bt_ref[...]
        s0 = jnp.sum(k_new * q, axis=1, keepdims=True) + b0_ref[g]
        m = jnp.maximum(jnp.max(s, axis=1, keepdims=True), s0)
        p = jnp.exp(s - m[:, 0:1])
        p0 = jnp.exp(s0 - m)
        l = jnp.sum(p, axis=1, keepdims=True) + p0
        o_rows = []
        for h in range(H_G):
            o_col = jnp.sum(kv_ref[1, h] * p[h:h + 1, :], axis=1, keepdims=True)
            o_rows.append(jnp.sum(jnp.where(eye, o_col, 0.0), axis=0, keepdims=True))
        o_ref[g] = (jnp.concatenate(o_rows, axis=0) + p0 * v_new) / l
        lse_ref[g] = m + jnp.log(l)


def _attn_sample(new_qkv, caches, layer, bias_t, bias_0):
    nreq = new_qkv.shape[0]
    views = []
    specs = []
    for g, cache in enumerate(caches):
        n_buf = cache.shape[2]
        assert n_buf == SPAN * DILATIONS[g], "cache must hold exactly one window"
        views.append(cache.transpose(0, 1, 3, 4, 5, 2))
        specs.append(pl.BlockSpec((None, None, 2, H_G, DH_A, n_buf),
                                  lambda b, layer=layer: (layer, b, 0, 0, 0, 0)))
    out = jax.ShapeDtypeStruct((nreq, N_GROUPS, H_G, DH_A), F32)
    out_spec = pl.BlockSpec((None, N_GROUPS, H_G, DH_A), lambda b: (b, 0, 0, 0))
    return pl.pallas_call(
        _sattn_kernel,
        grid=(nreq,),
        in_specs=[pl.BlockSpec((None, N_GROUPS, 3, H_G, DH_A), lambda b: (b, 0, 0, 0, 0))] + specs
        + [_resident(t.shape) for t in bias_t] + [_resident(bias_0.shape)],
        out_specs=[out_spec, out_spec],
        out_shape=[out, out],
        compiler_params=_params(1),
        name="attn_sample",
    )(new_qkv, *views, *bias_t, bias_0)


def _mlstm_kernel(*refs, n_seq):
    q_ref = refs[0]
    kt_refs = refs[1:1 + n_seq]
    vo_ref, gcol_ref = refs[1 + n_seq:3 + n_seq]
    grow_refs = refs[3 + n_seq:3 + 2 * n_seq]
    gb_row_ref, gb_col_ref, bout_ref, c_ref, n_ref, m_ref, nrep_s = refs[3 + 2 * n_seq:]
    chunk = q_ref.shape[1]

    @pl.when(pl.program_id(1) == 0)
    def _():
        c_ref[...] = jnp.zeros_like(c_ref)
        nrep_s[...] = jnp.zeros_like(nrep_s)
        m_ref[...] = jnp.zeros_like(m_ref)

    ti = lax.broadcasted_iota(jnp.int32, (chunk, chunk), 0)
    si = lax.broadcasted_iota(jnp.int32, (chunk, chunk), 1)
    causal = ti >= si
    lower = causal.astype(BF16)
    upper = (si >= ti).astype(BF16)
    items = [(j, h) for j in range(n_seq) for h in range(NH_B)]
    hcol = lambda h: slice(h * DK_B, (h + 1) * DK_B)
    a_rows, b_ts, b_lasts = [], [], []
    for j in range(n_seq):
        z_col = gcol_ref[j] + gb_row_ref[...]
        z_row = grow_refs[j][...] + gb_col_ref[...]
        b_col = sum(_dot(lower, part) for part in _split3(_log_sigmoid(z_col)))
        b_row = sum(_dot(part, upper) for part in _split3(_log_sigmoid(z_row)))
        for h in range(NH_B):
            a_rows.append(z_row[h:h + 1, :] - b_row[NH_B + h:NH_B + h + 1, :])
            b_ts.append(b_col[:, NH_B + h:NH_B + h + 1])
            b_lasts.append(b_row[NH_B + h:NH_B + h + 1, chunk - 1:chunk])
    a_row = jnp.stack(a_rows)
    b_t = jnp.stack(b_ts)
    m_prev = jnp.stack([m_ref[j, :, h:h + 1] for j, h in items])
    a_mat = jnp.where(causal, a_row, NEG)
    gmax = jnp.maximum(m_prev, jnp.max(a_mat, axis=-1, keepdims=True))
    dw = jnp.exp(a_mat - gmax)
    iw = jnp.exp(m_prev - gmax)
    g_last = gmax[:, chunk - 1:chunk, :]
    decay = jnp.exp(m_prev - g_last)
    w_state = jnp.exp(a_row - g_last)

    qb = [q_ref[j, :, hcol(h)] for j, h in items]
    kt = [kt_refs[j][hcol(h), :] for j, h in items]
    vb = [vo_ref[j, :, hcol(h)] for j, h in items]
    c_old = [c_ref[j, h] for j, h in items]
    n_old = [nrep_s[j, h] for j, h in items]
    idx = range(len(items))
    qk = jnp.stack([_dot(qb[i], kt[i]) for i in idx]) * dw
    qkb = qk.astype(BF16)
    q_c = jnp.stack([_dot(qb[i], c_old[i].astype(BF16)) for i in idx])
    qk_v = jnp.stack([_dot(qkb[i], vb[i]) for i in idx])
    num = iw * q_c + qk_v
    q_n = jnp.stack([_dot(qb[i], n_old[i].astype(BF16))[:, 0:1] for i in idx])
    den = iw * q_n + jnp.sum(qk, axis=-1, keepdims=True)
    hid = num / jnp.maximum(jnp.abs(den), jnp.exp(-(b_t + gmax)))
    for i, (j, h) in enumerate(items):
        o_gate = vo_ref[j, :, B_WIDTH + h * DK_B:B_WIDTH + (h + 1) * DK_B].astype(F32)
        bout_ref[j, :, hcol(h)] = (o_gate * hid[i]).astype(bout_ref.dtype)

    kwt = (jnp.stack(kt).astype(F32) * w_state).astype(BF16)
    ones = jnp.ones((chunk, LANES), BF16)
    for i, (j, h) in enumerate(items):
        c_ref[j, h] = decay[i] * c_old[i] + _dot(kwt[i], vb[i])
        nrep_s[j, h] = decay[i] * n_old[i] + _dot(kwt[i], ones)
        m_ref[j, :, h:h + 1] = b_lasts[i] + g_last[i]

    @pl.when(pl.program_id(1) == pl.num_programs(1) - 1)
    def _():
        for j, h in items:
            n_ref[j, h:h + 1, :] = nrep_s[j, h].T[0:1, :]


def _mlstm_prompt(q, kt, vo, gcol, grow, gb_row, gb_col, batch, seq):
    chunk = MLSTM_CHUNK
    n_seq = math.gcd(MLSTM_SEQS, batch)
    nc = seq // chunk
    seq3 = lambda b, c: (b, c, 0)
    lanes = [lambda b, c, j=j: (0, (b * n_seq + j) * nc + c) for j in range(n_seq)]
    state = lambda b, c: (b, 0, 0)
    return pl.pallas_call(
        functools.partial(_mlstm_kernel, n_seq=n_seq),
        grid=(batch // n_seq, nc),
        in_specs=[pl.BlockSpec((n_seq, chunk, B_WIDTH), seq3)]
        + [pl.BlockSpec((B_WIDTH, chunk), lanes[j]) for j in range(n_seq)]
        + [pl.BlockSpec((n_seq, chunk, COLS_VO), seq3), pl.BlockSpec((n_seq, chunk, LANES), seq3)]
        + [pl.BlockSpec((SUBLANES, chunk), lanes[j]) for j in range(n_seq)]
        + [_resident(gb_row.shape), _resident(gb_col.shape)],
        out_specs=[
            pl.BlockSpec((n_seq, chunk, B_WIDTH), seq3),
            pl.BlockSpec((n_seq, NH_B, DK_B, DK_B), lambda b, c: (b, 0, 0, 0)),
            pl.BlockSpec((n_seq, NH_B, DK_B), state),
            pl.BlockSpec((n_seq, 1, LANES), state),
        ],
        out_shape=[
            jax.ShapeDtypeStruct((batch, seq, B_WIDTH), BF16),
            jax.ShapeDtypeStruct((batch, NH_B, DK_B, DK_B), F32),
            jax.ShapeDtypeStruct((batch, NH_B, DK_B), F32),
            jax.ShapeDtypeStruct((batch, 1, LANES), F32),
        ],
        scratch_shapes=[pltpu.VMEM((n_seq, NH_B, DK_B, LANES), F32)],
        compiler_params=_params(2),
        name="mlstm_prompt",
    )(q.reshape(batch, seq, B_WIDTH), *([kt] * n_seq), vo.reshape(batch, seq, COLS_VO),
      gcol.reshape(batch, seq, LANES), *([grow] * n_seq), gb_row, gb_col)


def _smlstm_kernel(qk_ref, vo_ref, g_ref, gb_ref, cs_ref, cw_ref, cb_ref, c0_ref, n0_ref, m0_ref,
                   bout_ref, c1_ref, n1_ref, m1_ref, ncs_ref):
    u = qk_ref[...]
    y = cb_ref[...] + u * cw_ref[CONV_B - 1:CONV_B, :]
    for i in range(CONV_B - 1):
        y = y + cs_ref[i:i + 1, :] * cw_ref[i:i + 1, :]
    ncs_ref[0:CONV_B - 2, :] = cs_ref[1:CONV_B - 1, :]
    ncs_ref[CONV_B - 2:CONV_B - 1, :] = u
    y = y * _sigmoid(y)
    qf = y[:, :B_WIDTH]
    kf = y[:, B_WIDTH:] * (DK_B ** -0.5)
    z = g_ref[...] + gb_ref[...]
    lf_all = _log_sigmoid(z)
    m0 = m0_ref[...]
    eye = (lax.broadcasted_iota(jnp.int32, (DK_B, DK_B), 0)
           == lax.broadcasted_iota(jnp.int32, (DK_B, DK_B), 1))
    for h in range(NH_B):
        hs = slice(h * DK_B, (h + 1) * DK_B)
        ig = z[:, h:h + 1]
        inter = lf_all[:, NH_B + h:NH_B + h + 1] + m0[:, h:h + 1]
        m_t = jnp.maximum(inter, ig)
        dw = jnp.exp(ig - m_t)
        iw = jnp.exp(inter - m_t)
        qh = qf[:, hs]
        kh = kf[:, hs]
        vh = vo_ref[:, hs]
        c_old = c0_ref[h]
        n_old = n0_ref[h:h + 1, :]
        qk = jnp.sum(qh * kh, axis=1, keepdims=True) * dw
        num = iw * _hdot(qh, c_old) + qk * vh
        den = iw * jnp.sum(qh * n_old, axis=1, keepdims=True) + qk
        hid = num / jnp.maximum(jnp.abs(den), jnp.exp(-m_t))
        o_gate = _sigmoid(vo_ref[:, B_WIDTH + h * DK_B:B_WIDTH + (h + 1) * DK_B])
        bout_ref[:, hs] = o_gate * hid
        k_col = jnp.sum(jnp.where(eye, kh, 0.0), axis=1, keepdims=True)
        c1_ref[h] = iw * c_old + (dw * k_col) * vh
        n1_ref[h:h + 1, :] = iw * n_old + dw * kh
        m1_ref[:, h:h + 1] = m_t


def _mlstm_sample(qk_pre, vo, gcol, gb_row, conv_state, cw, cb, c0, n0, m0, layer):
    nreq = qk_pre.shape[0]
    one = lambda b: (b, 0, 0)
    lay3 = lambda b, layer=layer: (layer, b, 0, 0)
    lay4 = lambda b, layer=layer: (layer, b, 0, 0, 0)
    return pl.pallas_call(
        _smlstm_kernel,
        grid=(nreq,),
        in_specs=[
            pl.BlockSpec((None, 1, COLS_QK), one),
            pl.BlockSpec((None, 1, COLS_VO), one),
            pl.BlockSpec((None, 1, LANES), one),
            _resident(gb_row.shape),
            pl.BlockSpec((None, None, CONV_B - 1, COLS_QK), lay3),
            _resident(cw.shape), _resident(cb.shape),
            pl.BlockSpec((None, None, NH_B, DK_B, DK_B), lay4),
            pl.BlockSpec((None, None, NH_B, DK_B), lay3),
            pl.BlockSpec((None, None, 1, LANES), lay3),
        ],
        out_specs=[
            pl.BlockSpec((None, 1, B_WIDTH), one),
            pl.BlockSpec((None, NH_B, DK_B, DK_B), lambda b: (b, 0, 0, 0)),
            pl.BlockSpec((None, NH_B, DK_B), one),
            pl.BlockSpec((None, 1, LANES), one),
            pl.BlockSpec((None, CONV_B - 1, COLS_QK), one),
        ],
        out_shape=[
            jax.ShapeDtypeStruct((nreq, 1, B_WIDTH), F32),
            jax.ShapeDtypeStruct((nreq, NH_B, DK_B, DK_B), F32),
            jax.ShapeDtypeStruct((nreq, NH_B, DK_B), F32),
            jax.ShapeDtypeStruct((nreq, 1, LANES), F32),
            jax.ShapeDtypeStruct((nreq, CONV_B - 1, COLS_QK), F32),
        ],
        compiler_params=_params(1),
        name="mlstm_sample",
    )(qk_pre.reshape(nreq, 1, COLS_QK), vo.reshape(nreq, 1, COLS_VO),
      gcol.reshape(nreq, 1, LANES), gb_row, conv_state, cw, cb, c0, n0, m0)


def _merge_kernel(o0_ref, o1_ref, o2_ref, l0_ref, l1_ref, l2_ref, bo_ref, gab_ref, x_ref,
                  wpa_ref, wpb_ref, wo_ref, g2_ref, e_ref, x2_ref, h2_ref, *scratch):
    tm = x_ref.shape[0]

    def natural(ref, buf):
        dil, rows, _ = ref.shape
        n_slab = buf.shape[0]
        for r in range(dil):
            val = ref[r].astype(F32)
            for k in range(n_slab):
                buf[k, pl.ds(r, rows, stride=dil), :] = val[:, k * LANES:(k + 1) * LANES]
        return jnp.concatenate([buf[k] for k in range(n_slab)], axis=1)

    if scratch:
        os_ = [o0_ref[...].astype(F32), natural(o1_ref, scratch[0]), natural(o2_ref, scratch[1])]
        lses = [l0_ref[...], natural(l1_ref, scratch[2]), natural(l2_ref, scratch[3])]
    else:
        os_ = [r[...].astype(F32) for r in (o0_ref, o1_ref, o2_ref)]
        lses = [r[...] for r in (l0_ref, l1_ref, l2_ref)]
    top = jnp.maximum(jnp.maximum(lses[0], lses[1]), lses[2])
    ws = [jnp.exp(l - top) for l in lses]
    inv = 1.0 / (ws[0] + ws[1] + ws[2])
    e2 = e_ref[...]
    a = None
    for w, o in zip(ws, os_):
        alpha = w * inv
        hi = alpha.astype(BF16)
        lo = (alpha - hi.astype(F32)).astype(BF16)
        term = _dot(jnp.concatenate([hi, lo], axis=1), e2) * o
        a = term if a is None else a + term
    pa = _dot(a.astype(BF16), wpa_ref[...])
    pb = _dot(bo_ref[...].astype(BF16), wpb_ref[...])
    gate_a = gab_ref[:, :D_MODEL].astype(F32)
    gate_b = gab_ref[:, D_MODEL:].astype(F32)
    if not scratch:
        gate_a, gate_b = _sigmoid(gate_a), _sigmoid(gate_b)
    merged = gate_a * pa + gate_b * pb
    x2 = x_ref[...] + _dot(merged.astype(BF16), wo_ref[...])
    x2_ref[...] = x2
    h2_ref[...] = _rms(x2, g2_ref[...]).astype(BF16)


def _merge(os_, lses, bout, gab, x, wpa, wpb, wo, g2, e_bf, *, layer, tm, seq=None):
    n = x.shape[0]
    row = lambda i: (i, 0)

    def group_specs(cols):
        specs = [pl.BlockSpec((tm, cols), row)]
        for g in (1, 2):
            if seq is None:
                specs.append(pl.BlockSpec((tm, cols), row))
            else:
                d = DILATIONS[g]
                tpb = seq // tm
                specs.append(pl.BlockSpec((None, d, tm // d, cols),
                                          lambda i, tpb=tpb: (i // tpb, 0, i % tpb, 0)))
        return specs

    scratch = []
    if seq is not None:
        scratch = ([pltpu.VMEM((A_GROUP // LANES, tm, LANES), F32)] * 2
                   + [pltpu.VMEM((1, tm, LANES), F32)] * 2)
    return pl.pallas_call(
        _merge_kernel,
        grid=(n // tm,),
        scratch_shapes=scratch,
        in_specs=group_specs(A_GROUP) + group_specs(LANES) + [
            pl.BlockSpec((tm, B_WIDTH), row),
            pl.BlockSpec((tm, COLS_GAB), row),
            pl.BlockSpec((tm, D_MODEL), row),
            _layer_resident(wpa.shape, layer), _layer_resident(wpb.shape, layer),
            _layer_resident(wo.shape, layer), _resident(g2.shape), _resident(e_bf.shape),
        ],
        out_specs=[pl.BlockSpec((tm, D_MODEL), row), pl.BlockSpec((tm, D_MODEL), row)],
        out_shape=[jax.ShapeDtypeStruct((n, D_MODEL), F32), jax.ShapeDtypeStruct((n, D_MODEL), BF16)],
        compiler_params=_params(1),
        name="merge",
    )(*os_, *lses, bout, gab, x, wpa, wpb, wo, g2, e_bf)


def _ffn_kernel(*refs, tiles_per_batch, from_state, final_norm):
    h2_ref, x2_ref, wup_ref, wdn_ref, cw_ref, cb_ref = refs[:6]
    pos = 6
    if from_state:
        prev_refs = refs[pos:pos + CONV_F - 1]
        pos += CONV_F - 1
    if final_norm:
        fg_ref = refs[pos]
        pos += 1
    x3_ref, u_ref = refs[pos:pos + 2]
    tm = h2_ref.shape[0]
    pad = SUBLANES
    if not from_state:
        ext_s = refs[pos + 2]

        @pl.when(pl.program_id(0) % tiles_per_batch == 0)
        def _():
            ext_s[:, 0:pad, :] = jnp.zeros((ext_s.shape[0], pad, LANES), F32)

    h2 = h2_ref[...]
    acc = x2_ref[...]
    n_chunk = D_FF // FF_CHUNK

    def halves(c):
        return (c * FF_CHUNK, D_FF + c * FF_CHUNK)

    def up(c):
        us = []
        for off in halves(c):
            u = _dot(h2, wup_ref[:, off:off + FF_CHUNK])
            if from_state:
                u_ref[:, off:off + FF_CHUNK] = u
                us.append(u)
            else:
                for k in range(FF_CHUNK // LANES):
                    ext_s[off // LANES + k, pad:pad + tm, :] = u[:, k * LANES:(k + 1) * LANES]
        return us

    def conv(c, us):
        ys = []
        for j, off in enumerate(halves(c)):
            cs = slice(off, off + FF_CHUNK)
            if from_state:
                y = cb_ref[:, cs] + us[j] * cw_ref[CONV_F - 1:CONV_F, cs]
                for i in range(CONV_F - 1):
                    y = y + prev_refs[i][:, cs] * cw_ref[i:i + 1, cs]
            else:
                parts = []
                for k in range(FF_CHUNK // LANES):
                    slab = off // LANES + k
                    ks = slice(off + k * LANES, off + (k + 1) * LANES)
                    part = cb_ref[:, ks]
                    for i in range(CONV_F):
                        lo = pad - (CONV_F - 1) + i
                        part = part + ext_s[slab, lo:lo + tm, :] * cw_ref[i:i + 1, ks]
                    parts.append(part)
                    tail = ext_s[slab, tm:tm + pad, :]
                    u_ref[:, ks] = tail
                    ext_s[slab, 0:pad, :] = tail
                y = jnp.concatenate(parts, axis=1)
            ys.append(y)
        return ys

    ahead = 1 if from_state else FF_AHEAD
    pending = [up(c) for c in range(min(ahead, n_chunk))]
    for c in range(n_chunk):
        us = pending.pop(0)
        if c + ahead < n_chunk:
            pending.append(up(c + ahead))
        y_act, y_gate = conv(c, us)
        act = _gelu_tanh(y_act) * y_gate
        acc = acc + _dot(act.astype(BF16), wdn_ref[c * FF_CHUNK:(c + 1) * FF_CHUNK, :])
    if final_norm:
        acc = _rms(acc, fg_ref[...])
    x3_ref[...] = acc


def _ffn(h2, x2, wup, wdn, cw, cb, *, layer, tm, tiles_per_batch=None, prev_rows=None,
         final_g=None):
    n = h2.shape[0]
    from_state = prev_rows is not None
    row = lambda i: (i, 0)
    in_specs = [
        pl.BlockSpec((tm, D_MODEL), row), pl.BlockSpec((tm, D_MODEL), row),
        _layer_resident(wup.shape, layer), _layer_resident(wdn.shape, layer),
        _resident(cw.shape), _resident(cb.shape),
    ]
    args = [h2, x2, wup, wdn, cw, cb]
    scratch = []
    if from_state:
        in_specs += [pl.BlockSpec((tm, 2 * D_FF), row)] * (CONV_F - 1)
        args += list(prev_rows)
        u_shape = jax.ShapeDtypeStruct((n, 2 * D_FF), F32)
        u_spec = pl.BlockSpec((tm, 2 * D_FF), row)
    else:
        n_batch = n // (tm * tiles_per_batch)
        u_shape = jax.ShapeDtypeStruct((n_batch, SUBLANES, 2 * D_FF), F32)
        u_spec = pl.BlockSpec((None, SUBLANES, 2 * D_FF), lambda i: (i // tiles_per_batch, 0, 0))
        scratch = [pltpu.VMEM((2 * D_FF // LANES, tm + SUBLANES, LANES), F32)]
    if final_g is not None:
        in_specs.append(_resident(final_g.shape))
        args.append(final_g)
    return pl.pallas_call(
        functools.partial(_ffn_kernel, tiles_per_batch=tiles_per_batch, from_state=from_state,
                          final_norm=final_g is not None),
        grid=(n // tm,),
        in_specs=in_specs,
        out_specs=[pl.BlockSpec((tm, D_MODEL), row), u_spec],
        out_shape=[jax.ShapeDtypeStruct((n, D_MODEL), F32), u_shape],
        scratch_shapes=scratch,
        compiler_params=_params(1),
        name="ffn",
    )(*args)


def _t5_bucket(dist):
    max_exact = NUM_BUCKETS // 2
    df = jnp.maximum(dist, 1).astype(F32)
    large = max_exact + (jnp.log(df / max_exact) / math.log(MAX_DISTANCE / max_exact)
                         * (NUM_BUCKETS - max_exact)).astype(jnp.int32)
    large = jnp.minimum(large, NUM_BUCKETS - 1)
    return jnp.where(dist < max_exact, dist, large)


def _bias_table(rel_bias, g, dist):
    bucket = _t5_bucket(jnp.asarray(dist, jnp.int32))
    table = rel_bias[:, g * H_G:(g + 1) * H_G].reshape((NUM_BUCKETS, H_G) + (1,) * bucket.ndim)
    ids = jnp.arange(NUM_BUCKETS).reshape((NUM_BUCKETS, 1) + (1,) * bucket.ndim)
    return jnp.sum(jnp.where(bucket[None, None] == ids, table, 0.0), axis=0)


def _prompt_bias(rel_bias, g):
    qi = np.arange(Q_BLOCK)[:, None]
    ki = np.arange(2 * Q_BLOCK)[None, :]
    rel = qi + Q_BLOCK - ki
    band = (rel >= 0) & (rel <= SPAN)
    bias = _bias_table(rel_bias, g, np.maximum(rel, 0) * DILATIONS[g])
    return jnp.where(band[None], bias, NEG)


def _sample_bias(rel_bias):
    tables, news = [], []
    for g in range(N_GROUPS):
        dil = DILATIONS[g]
        n_buf = SPAN * dil
        pos = np.arange(n_buf)
        bias = _bias_table(rel_bias, g, n_buf - pos)
        tables.append(jnp.where((pos % dil == 0)[None], bias, NEG))
        news.append(jnp.broadcast_to(_bias_table(rel_bias, g, np.zeros((1,), np.int32)), (H_G, DH_A)))
    return tables, jnp.stack(news)


def _head_indicator():
    e = np.zeros((LANES, A_GROUP), np.float32)
    for h in range(H_G):
        e[h, h * DH_A:(h + 1) * DH_A] = 1.0
    return e


def kernel(x_prompt, x_sample, cache_kv_w128, cache_kv_w512, cache_kv_w2048, state_mlstm_conv,
           state_mlstm_C, state_mlstm_n, state_mlstm_m, state_ffn_conv, rel_bias, norm1_g, w_in,
           mconv_w, mconv_b, mgate_b, w_pa, w_pb, w_o, norm2_g, w_up, fconv_w, fconv_b, w_down,
           final_norm_g):
    batch, seq, _ = x_prompt.shape
    nreq = x_sample.shape[0]
    depth = w_in.shape[0]
    n_p = batch * seq
    caches = (cache_kv_w128, cache_kv_w512, cache_kv_w2048)

    assert seq % (Q_BLOCK * DILATIONS[-1]) == 0 and seq >= WINDOWS[-1]
    e_bf = jnp.asarray(np.concatenate([_head_indicator()] * 2, axis=0), BF16)
    prompt_bias = [_prompt_bias(rel_bias, g) for g in range(N_GROUPS)]
    sbias_m, sbias_0 = _sample_bias(rel_bias)
    m0_all = jnp.pad(state_mlstm_m, ((0, 0), (0, 0), (0, LANES - NH_B)))[:, :, None, :]
    fg = final_norm_g.reshape(1, D_MODEL)

    wpa, wpb, wo = w_pa.astype(BF16), w_pb.astype(BF16), w_o.astype(BF16)
    wup, wdn = w_up.astype(BF16), w_down.astype(BF16)

    xp = x_prompt.reshape(n_p, D_MODEL)
    xs = x_sample.reshape(nreq, D_MODEL)
    p_st = [[] for _ in range(8)]
    s_st = [[] for _ in range(8)]

    for l in range(depth):
        last = l == depth - 1
        w_main, w_gate = _regroup_w_in(w_in, l)
        g1 = norm1_g[l].reshape(1, D_MODEL)
        g2 = norm2_g[l].reshape(1, D_MODEL)
        gate_bias = mgate_b[l].reshape(1, 2 * NH_B)
        gb_row = jnp.pad(gate_bias, ((0, 0), (0, LANES - 2 * NH_B)))
        gb_col = gate_bias.reshape(2 * NH_B, 1)
        mcw, mcb = mconv_w[l], mconv_b[l].reshape(1, COLS_QK)
        fcw, fcb = fconv_w[l], fconv_b[l].reshape(1, 2 * D_FF)

        *qkvs, q_b, vo, gab, gcol, grow, conv_tail, kt_b = _inproj(
            xp, g1, w_main, w_gate, tm=INPROJ_ROWS, act_dtype=BF16, seq=seq, conv=(mcw, mcb))
        os_, lses = [], []
        for g in range(N_GROUPS):
            d = DILATIONS[g]
            o_g, lse_g = _attn_prompt(qkvs[g].reshape(batch * d, seq // d, COLS_QKV_G), prompt_bias[g])
            shape = (n_p,) if g == 0 else (batch, d, seq // d)
            os_.append(o_g.reshape(shape + (A_GROUP,)))
            lses.append(lse_g.reshape(shape + (LANES,)))
        bout, c_p, n_p_state, m_p = _mlstm_prompt(q_b, kt_b, vo, gcol, grow, gb_row, gb_col,
                                                  batch, seq)
        x2, h2 = _merge(os_, lses, bout.reshape(n_p, B_WIDTH), gab, xp, wpa, wpb, wo, g2, e_bf,
                        layer=l, tm=MERGE_ROWS, seq=seq)
        xp, u_tail = _ffn(h2, x2, wup, wdn, fcw, fcb, layer=l, tm=FFN_ROWS,
                          tiles_per_batch=seq // FFN_ROWS, final_g=fg if last else None)

        for g in range(N_GROUPS):
            p_st[g].append(qkvs[g].reshape(batch, DILATIONS[g], seq // DILATIONS[g], COLS_QKV_G))
        p_st[3].append(conv_tail[:, SUBLANES - (CONV_B - 1):])
        p_st[4].append(c_p)
        p_st[5].append(n_p_state)
        p_st[6].append(m_p[:, 0, :NH_B])
        p_st[7].append(u_tail[:, SUBLANES - (CONV_F - 1):])

        *qkvs_s, qk_s, vo_s, gab_s, gcol_s = _inproj(xs, g1, w_main, w_gate, tm=nreq, act_dtype=F32)
        new_qkv = jnp.stack(qkvs_s, axis=1).reshape(nreq, N_GROUPS, 3, H_G, DH_A)
        o_s, lse_s = _attn_sample(new_qkv, caches, l, sbias_m, sbias_0)
        bout_s, c_s, n_s, m_s, conv_s = _mlstm_sample(qk_s, vo_s, gcol_s, gb_row, state_mlstm_conv,
                                                      mcw, mcb, state_mlstm_C, state_mlstm_n, m0_all, l)
        lse_pad = jnp.pad(lse_s[..., 0], ((0, 0), (0, 0), (0, LANES - H_G)))
        x2_s, h2_s = _merge([o_s[:, g].reshape(nreq, A_GROUP) for g in range(N_GROUPS)],
                            [lse_pad[:, g] for g in range(N_GROUPS)],
                            bout_s.reshape(nreq, B_WIDTH), gab_s, xs, wpa, wpb, wo, g2, e_bf,
                            layer=l, tm=nreq)
        fbuf = state_ffn_conv[l]
        xs, u_s = _ffn(h2_s, x2_s, wup, wdn, fcw, fcb, layer=l, tm=nreq,
                       prev_rows=[fbuf[:, i] for i in range(CONV_F - 1)],
                       final_g=fg if last else None)

        for g in range(N_GROUPS):
            s_st[g].append(new_qkv[:, g, 1:][:, None])
        s_st[3].append(conv_s)
        s_st[4].append(c_s)
        s_st[5].append(n_s)
        s_st[6].append(m_s[:, 0, :NH_B])
        s_st[7].append(jnp.concatenate([fbuf[:, 1:], u_s[:, None, :]], axis=1))

    outs = [xp.reshape(batch, seq, D_MODEL), xs.reshape(nreq, 1, D_MODEL)]
    for i in range(8):
        if i < N_GROUPS:
            kv = _kv_tail(p_st[i], i).reshape(depth, batch, 2, H_G, DH_A, SPAN * DILATIONS[i])
            outs.append(kv.transpose(0, 1, 5, 2, 3, 4))
        else:
            outs.append(jnp.stack(p_st[i], 0))
        outs.append(jnp.stack(s_st[i], 0))
    return tuple(outs)
```
